```python
import jax, jax.numpy as jnp
from jax import lax
import numpy as np

D_MODEL = 1024
BATCH = 4
SEQ = 8192
DEPTH = 2
DEC_BATCH = 2
DEC_SEQ = 16384
PAST_LEN = 128

A_PATTERNS = ((128, 1), (512, 4), (2048, 16))
A_GROUPS = 3
A_HEADS = 8
A_HD = 64
A_W = A_GROUPS * A_HEADS * A_HD
A_OUT = A_HEADS * A_HD
B_HEADS = 4
B_DK = 64
B_DV = 128
B_CHUNK = 128
B_QK = B_HEADS * B_DK
B_OUT = B_HEADS * B_DV
C_HEADS = 8
C_HD = 64
C_W = C_HEADS * C_HD
GRID_W = 64
C_KH_MAX = 8
C_KW = 16
C_QR = 2
C_QC = 16
C_KC = 32
N_BRANCH = 3
D_FF = 2816
LN_EPS = 1e-5
GN_EPS = 1e-5
ALPHA = (2 * DEPTH) ** 0.25
BETA = (8 * DEPTH) ** -0.25
IN_SPLITS = (A_W, A_W, A_W, B_QK, B_QK, B_OUT, B_OUT, C_W, C_W, C_W, N_BRANCH * D_MODEL)
D_IN = sum(IN_SPLITS)

kernel_name = 'hybrid_bidir_encoder_dilated_retention_natten'


def layer_norm(x, g, b):
    xf = x.astype(jnp.float32)
    mu = xf.mean(-1, keepdims=True)
    var = jnp.square(xf - mu).mean(-1, keepdims=True)
    y = (xf - mu) * lax.rsqrt(var + LN_EPS)
    return (y * g.astype(jnp.float32) + b.astype(jnp.float32)).astype(x.dtype)


def swiglu(x, w_gate, w_up, w_down):
    return (jax.nn.silu(x @ w_gate) * (x @ w_up)) @ w_down


def alibi_slopes(n):
    return 2.0 ** (-8.0 * jnp.arange(1, n + 1, dtype=jnp.float32) / n)


def split_columns(z):
    outs, start = [], 0
    for w in IN_SPLITS:
        outs.append(z[..., start:start + w])
        start += w
    return outs


def dilated_window_attention(q, k, v, dilation, half, slopes):
    b, s, h, hd = q.shape
    L = s // dilation
    nb = -(-L // half)
    Lp = nb * half

    def by_residue(t):
        return t.reshape(b, L, dilation, h, hd).transpose(0, 2, 3, 1, 4)

    qd, kd, vd = by_residue(q), by_residue(k), by_residue(v)
    qb = jnp.pad(qd, ((0, 0),) * 3 + ((0, Lp - L), (0, 0))).reshape(b, dilation, h, nb, half, hd)

    def key_blocks(t):
        tp = jnp.pad(t, ((0, 0),) * 3 + ((half, Lp - L + half), (0, 0))).reshape(b, dilation, h, nb + 2, half, hd)
        return jnp.concatenate([tp[:, :, :, :-2], tp[:, :, :, 1:-1], tp[:, :, :, 2:]], axis=4)

    kb, vb = key_blocks(kd), key_blocks(vd)
    qi = jnp.arange(nb)[:, None] * half + jnp.arange(half)[None, :]
    kj = (jnp.arange(nb)[:, None] - 1) * half + jnp.arange(3 * half)[None, :]
    rel = kj[:, None, :] - qi[:, :, None]
    valid = (jnp.abs(rel) <= half) & (kj[:, None, :] >= 0) & (kj[:, None, :] < L)
    dist = (jnp.abs(rel) * dilation).astype(jnp.float32)
    bias = -slopes[:, None, None, None] * dist
    scores = jnp.einsum('bdhnqe,bdhnke->bdhnqk', qb, kb, preferred_element_type=jnp.float32) * (hd ** -0.5) + bias
    scores = jnp.where(valid, scores, -jnp.inf)
    lse = jax.nn.logsumexp(scores, axis=-1)
    p = jnp.exp(scores - lse[..., None])
    out = jnp.einsum('bdhnqk,bdhnke->bdhnqe', p.astype(v.dtype), vb)
    out = out.reshape(b, dilation, h, Lp, hd)[:, :, :, :L].transpose(0, 3, 1, 2, 4).reshape(b, s, h, hd)
    lse = lse.reshape(b, dilation, h, Lp)[..., :L].transpose(0, 3, 1, 2).reshape(b, s, h)
    return out, lse


def mixer_a(qa, ka, va):
    b, s, _ = qa.shape
    shp = (b, s, A_GROUPS, A_HEADS, A_HD)
    qa, ka, va = qa.reshape(shp), ka.reshape(shp), va.reshape(shp)
    slopes = alibi_slopes(A_HEADS)
    outs, lses = [], []
    for g, (window, dil) in enumerate(A_PATTERNS):
        o, l = dilated_window_attention(qa[:, :, g], ka[:, :, g], va[:, :, g], dil, window // (2 * dil), slopes)
        outs.append(o.astype(jnp.float32))
        lses.append(l)
    w = jax.nn.softmax(jnp.stack(lses, 0), axis=0)
    o = jnp.einsum('gbsh,gbshe->bshe', w, jnp.stack(outs, 0))
    return o.reshape(b, s, A_OUT).astype(qa.dtype)


def retention_chunkwise(q, k, v, log_gamma, include_diag):
    c = q.shape[3]
    idx = jnp.arange(c, dtype=jnp.float32)
    diff = idx[:, None] - idx[None, :]
    mask = diff >= 0 if include_diag else diff > 0
    decay = jnp.where(mask, jnp.exp(log_gamma[:, None, None] * jnp.where(mask, diff, 0.0)), 0.0)
    inner = jnp.einsum('bhnid,bhnjd->bhnij', q, k) * decay[:, None]
    inner = jnp.einsum('bhnij,bhnje->bhnie', inner, v)
    k_decay = jnp.exp(log_gamma[:, None] * (c - 1 - idx))
    q_decay = jnp.exp(log_gamma[:, None] * (idx + 1))
    chunk_kv = jnp.einsum('bhnjd,bhnje->nbhde', k * k_decay[:, None, :, None], v)
    chunk_decay = jnp.exp(log_gamma * c)[:, None, None]

    def step(state, kv):
        return chunk_decay * state + kv, state

    _, prev = lax.scan(step, jnp.zeros_like(chunk_kv[0]), chunk_kv)
    cross = jnp.einsum('bhnid,nbhde->bhnie', q * q_decay[:, None, :, None], prev)
    return inner + cross


def mixer_b(qb, kb, vb, gb, logit_fwd, logit_bwd):
    b, s, _ = qb.shape
    n = s // B_CHUNK

    def heads(t, dh):
        return t.astype(jnp.float32).reshape(b, n, B_CHUNK, B_HEADS, dh).transpose(0, 3, 1, 2, 4)

    def flip(t):
        return t[:, :, ::-1, ::-1]

    q = heads(qb, B_DK)
    k = heads(kb, B_DK) * (B_DK ** -0.5)
    v = heads(vb, B_DV)
    lg_f = jax.nn.log_sigmoid(logit_fwd.astype(jnp.float32))
    lg_b = jax.nn.log_sigmoid(logit_bwd.astype(jnp.float32))
    y = retention_chunkwise(q, k, v, lg_f, True) + flip(retention_chunkwise(flip(q), flip(k), flip(v), lg_b, False))
    mu = y.mean(-1, keepdims=True)
    var = jnp.square(y - mu).mean(-1, keepdims=True)
    y = (y - mu) * lax.rsqrt(var + GN_EPS)
    y = y.transpose(0, 2, 3, 1, 4).reshape(b, s, B_OUT)
    return (jax.nn.silu(gb.astype(jnp.float32)) * y).astype(qb.dtype)


def mixer_c(qc, kc, vc, rpb):
    b, s, _ = qc.shape
    rows = s // GRID_W
    kh = min(C_KH_MAX, rows)
    rr = min(kh + 1, rows)
    nrb = rows // C_QR
    ncb = GRID_W // C_QC

    def grid(t):
        return t.reshape(b, rows, GRID_W, C_HEADS, C_HD).transpose(0, 3, 1, 2, 4)

    qg, kg, vg = grid(qc), grid(kc), grid(vc)
    row_start = jnp.clip(jnp.arange(rows) - kh // 2, 0, rows - kh)
    col_start = jnp.clip(jnp.arange(GRID_W) - C_KW // 2, 0, GRID_W - C_KW)
    rb_start = jnp.clip(row_start[::C_QR], 0, rows - rr)
    cb_start = jnp.clip(jnp.arange(ncb) * C_QC - C_KW // 2, 0, GRID_W - C_KC)
    key_rows = rb_start[:, None] + jnp.arange(rr)[None, :]
    key_cols = cb_start[:, None] + jnp.arange(C_KC)[None, :]
    ri = key_rows[:, :, None, None]
    ci = key_cols[None, None, :, :]
    kr = kg[:, :, ri, ci]
    vr = vg[:, :, ri, ci]
    qr = qg.reshape(b, C_HEADS, nrb, C_QR, ncb, C_QC, C_HD)
    scores = jnp.einsum('bhnqcwe,bhnrcke->bhnqcwrk', qr, kr, preferred_element_type=jnp.float32) * (C_HD ** -0.5)

    qrow = jnp.arange(nrb)[:, None] * C_QR + jnp.arange(C_QR)[None, :]
    qcol = jnp.arange(ncb)[:, None] * C_QC + jnp.arange(C_QC)[None, :]
    rs = row_start[qrow][..., None]
    cs = col_start[qcol][..., None]
    kr_idx = key_rows[:, None, :]
    kc_idx = key_cols[:, None, :]
    row_ok = (kr_idx >= rs) & (kr_idx < rs + kh)
    col_ok = (kc_idx >= cs) & (kc_idx < cs + C_KW)
    mask = row_ok[:, :, None, None, :, None] & col_ok[None, None, :, :, None, :]
    dr = jnp.clip(kr_idx - qrow[..., None], -(C_KH_MAX - 1), C_KH_MAX - 1) + (C_KH_MAX - 1)
    dc = jnp.clip(kc_idx - qcol[..., None], -(C_KW - 1), C_KW - 1) + (C_KW - 1)
    bias = rpb.astype(jnp.float32)[:, dr[:, :, None, None, :, None], dc[None, None, :, :, None, :]]
    scores = jnp.where(mask, scores + bias, -jnp.inf)
    p = jax.nn.softmax(scores, axis=(-2, -1))
    out = jnp.einsum('bhnqcwrk,bhnrcke->bhnqcwe', p.astype(vr.dtype), vr)
    out = out.reshape(b, C_HEADS, rows, GRID_W, C_HD).transpose(0, 2, 3, 1, 4).reshape(b, s, C_W)
    return out.astype(qc.dtype)


def encoder_layer(x, ffn1_w_gate, ffn1_w_up, ffn1_w_down, ln1_g, ln1_b, w_in, ret_logit_fwd, ret_logit_bwd,
                  na_rpb, w_branch_a, w_branch_b, w_branch_c, w_out, ln2_g, ln2_b,
                  ffn2_w_gate, ffn2_w_up, ffn2_w_down, ln3_g, ln3_b):
    b, s, _ = x.shape
    x = layer_norm(ALPHA * x + 0.5 * swiglu(x, ffn1_w_gate, ffn1_w_up, ffn1_w_down), ln1_g, ln1_b)
    z = x @ w_in
    aq, ak, av, bq, bk, bv, bg, cq, ck, cv, gates = split_columns(z)
    ya = mixer_a(aq, ak, av)
    yb = mixer_b(bq, bk, bv, bg, ret_logit_fwd, ret_logit_bwd)
    yc = mixer_c(cq, ck, cv, na_rpb)
    g = jax.nn.sigmoid(gates).reshape(b, s, N_BRANCH, D_MODEL)
    merged = g[:, :, 0] * (ya @ w_branch_a) + g[:, :, 1] * (yb @ w_branch_b) + g[:, :, 2] * (yc @ w_branch_c)
    x = layer_norm(ALPHA * x + merged @ w_out, ln2_g, ln2_b)
    x = layer_norm(ALPHA * x + 0.5 * swiglu(x, ffn2_w_gate, ffn2_w_up, ffn2_w_down), ln3_g, ln3_b)
    return x


def run_trunk(x, ffn1_w_gate, ffn1_w_up, ffn1_w_down, ln1_g, ln1_b, w_in, ret_logit_fwd, ret_logit_bwd,
              na_rpb, w_branch_a, w_branch_b, w_branch_c, w_out, ln2_g, ln2_b,
              ffn2_w_gate, ffn2_w_up, ffn2_w_down, ln3_g, ln3_b):
    for i in range(DEPTH):
        x = encoder_layer(x, ffn1_w_gate[i], ffn1_w_up[i], ffn1_w_down[i], ln1_g[i], ln1_b[i], w_in[i],
                          ret_logit_fwd[i], ret_logit_bwd[i], na_rpb[i], w_branch_a[i], w_branch_b[i],
                          w_branch_c[i], w_out[i], ln2_g[i], ln2_b[i], ffn2_w_gate[i], ffn2_w_up[i],
                          ffn2_w_down[i], ln3_g[i], ln3_b[i])
    return x


def setup_inputs(seed: int = 0) -> dict:
    key = jax.random.key(seed)
    ks = jax.random.split(key, 22)
    f32 = jnp.float32

    def nrm(k, shape, scale):
        return jax.random.normal(k, shape, f32) * scale

    levels = jnp.arange(5, 5 + B_HEADS, dtype=f32)
    base_logit = jnp.log(2.0 ** levels - 1.0)
    return {
        'x_prompt': nrm(ks[0], (BATCH, SEQ, D_MODEL), 1.0),
        'x_sample': nrm(ks[1], (DEC_BATCH, DEC_SEQ, D_MODEL), 1.0),
        'ffn1_w_gate': nrm(ks[2], (DEPTH, D_MODEL, D_FF), D_MODEL ** -0.5),
        'ffn1_w_up': nrm(ks[3], (DEPTH, D_MODEL, D_FF), D_MODEL ** -0.5),
        'ffn1_w_down': nrm(ks[4], (DEPTH, D_FF, D_MODEL), BETA * D_FF ** -0.5),
        'ln1_g': 1.0 + nrm(ks[5], (DEPTH, D_MODEL), 0.02),
        'ln1_b': nrm(ks[6], (DEPTH, D_MODEL), 0.02),
        'w_in': nrm(ks[7], (DEPTH, D_MODEL, D_IN), D_MODEL ** -0.5),
        'ret_logit_fwd': base_logit + nrm(ks[8], (DEPTH, B_HEADS), 0.1),
        'ret_logit_bwd': base_logit + nrm(ks[9], (DEPTH, B_HEADS), 0.1),
        'na_rpb': nrm(ks[10], (DEPTH, C_HEADS, 2 * C_KH_MAX - 1, 2 * C_KW - 1), 0.02),
        'w_branch_a': nrm(ks[11], (DEPTH, A_OUT, D_MODEL), A_OUT ** -0.5),
        'w_branch_b': nrm(ks[12], (DEPTH, B_OUT, D_MODEL), B_OUT ** -0.5),
        'w_branch_c': nrm(ks[13], (DEPTH, C_W, D_MODEL), C_W ** -0.5),
        'w_out': nrm(ks[14], (DEPTH, D_MODEL, D_MODEL), BETA * D_MODEL ** -0.5),
        'ln2_g': 1.0 + nrm(ks[15], (DEPTH, D_MODEL), 0.02),
        'ln2_b': nrm(ks[16], (DEPTH, D_MODEL), 0.02),
        'ffn2_w_gate': nrm(ks[17], (DEPTH, D_MODEL, D_FF), D_MODEL ** -0.5),
        'ffn2_w_up': nrm(ks[18], (DEPTH, D_MODEL, D_FF), D_MODEL ** -0.5),
        'ffn2_w_down': nrm(ks[19], (DEPTH, D_FF, D_MODEL), BETA * D_FF ** -0.5),
        'ln3_g': 1.0 + nrm(ks[20], (DEPTH, D_MODEL), 0.02),
        'ln3_b': nrm(ks[21], (DEPTH, D_MODEL), 0.02),
    }


def reference(x_prompt, x_sample, ffn1_w_gate, ffn1_w_up, ffn1_w_down, ln1_g, ln1_b, w_in, ret_logit_fwd,
              ret_logit_bwd, na_rpb, w_branch_a, w_branch_b, w_branch_c, w_out, ln2_g, ln2_b,
              ffn2_w_gate, ffn2_w_up, ffn2_w_down, ln3_g, ln3_b):
    y_prompt = run_trunk(x_prompt, ffn1_w_gate, ffn1_w_up, ffn1_w_down, ln1_g, ln1_b, w_in, ret_logit_fwd,
                         ret_logit_bwd, na_rpb, w_branch_a, w_branch_b, w_branch_c, w_out, ln2_g, ln2_b,
                         ffn2_w_gate, ffn2_w_up, ffn2_w_down, ln3_g, ln3_b)
    y_sample = run_trunk(x_sample, ffn1_w_gate, ffn1_w_up, ffn1_w_down, ln1_g, ln1_b, w_in, ret_logit_fwd,
                         ret_logit_bwd, na_rpb, w_branch_a, w_branch_b, w_branch_c, w_out, ln2_g, ln2_b,
                         ffn2_w_gate, ffn2_w_up, ffn2_w_down, ln3_g, ln3_b)
    return (y_prompt, y_sample)
```

```python
import functools

import jax
import jax.numpy as jnp
from jax import lax
from jax.experimental import pallas as pl
from jax.experimental.pallas import tpu as pltpu

F32 = jnp.float32
BF16 = jnp.bfloat16

D_MODEL = 1024
DEPTH = 2
D_FF = 2816
LN_EPS = 1e-5
GN_EPS = 1e-5
ALPHA = (2 * DEPTH) ** 0.25

A_PATTERNS = ((128, 1), (512, 4), (2048, 16))
A_HEADS = 8
A_HD = 64
A_HALF = 64
B_HEADS = 4
B_DK = 64
B_DV = 128
B_CHUNK = 128
C_HEADS = 8
C_HD = 64
GRID_W = 64
C_KH = 8
C_KW = 16
C_QR = 4
D_IN = 10752
NEG = -1e30

GATE_W = 3 * D_MODEL
HEAD_BLOCK = 512
AQ_BLK, AK_BLK, AV_BLK = 6, 9, 12
BQ_BLK256, BK_BLK256 = 30, 31
BV_BLK, BG_BLK = 16, 17
CQ_BLK, CK_BLK, CV_BLK = 18, 19, 20
IN_BLOCKS = D_IN // HEAD_BLOCK

VMEM_LIMIT = 56 * 1024 * 1024
FF_CHUNKS = ((0, 512), (512, 1024), (1024, 1536), (1536, 2048), (2048, 2560), (2560, 2816))


def _params(sem):
    return pltpu.CompilerParams(dimension_semantics=sem, vmem_limit_bytes=VMEM_LIMIT)


def _const_spec(shape):
    zeros = (0,) * len(shape)
    return pl.BlockSpec(shape, lambda *_: zeros)


def _layer_norm(r, g, b):
    mu = jnp.mean(r, axis=-1, keepdims=True)
    c = r - mu
    var = jnp.mean(c * c, axis=-1, keepdims=True)
    return c * lax.rsqrt(var + LN_EPS) * g + b


def _ffn_ln_kernel(x_ref, wg_ref, wu_ref, wd_ref, g_ref, b_ref, o_ref, *maybe_ob_ref):
    x = x_ref[...]
    xb = x.astype(BF16)
    acc = None
    for c0, c1 in FF_CHUNKS:
        gate = jnp.dot(xb, wg_ref[:, c0:c1], preferred_element_type=F32)
        up = jnp.dot(xb, wu_ref[:, c0:c1], preferred_element_type=F32)
        h = (gate * jax.nn.sigmoid(gate) * up).astype(BF16)
        part = jnp.dot(h, wd_ref[c0:c1, :], preferred_element_type=F32)
        acc = part if acc is None else acc + part
    y = _layer_norm(ALPHA * x + 0.5 * acc, g_ref[...], b_ref[...])
    o_ref[...] = y
    if maybe_ob_ref:
        maybe_ob_ref[0][...] = y.astype(BF16)


def ffn_ln(x, wg, wu, wd, g, b, with_bf16):
    t = x.shape[0]
    tm = 512
    row = pl.BlockSpec((tm, D_MODEL), lambda i: (i, 0))
    out_shape = [jax.ShapeDtypeStruct((t, D_MODEL), F32)]
    out_specs = [row]
    if with_bf16:
        out_shape.append(jax.ShapeDtypeStruct((t, D_MODEL), BF16))
        out_specs.append(row)
    return pl.pallas_call(
        _ffn_ln_kernel,
        grid=(t // tm,),
        in_specs=[row, _const_spec((D_MODEL, D_FF)), _const_spec((D_MODEL, D_FF)), _const_spec((D_FF, D_MODEL)),
                  _const_spec((1, D_MODEL)), _const_spec((1, D_MODEL))],
        out_specs=out_specs,
        out_shape=out_shape,
        compiler_params=_params(("arbitrary",)),
        name="ffn_ln",
    )(x, wg, wu, wd, g, b)


def _in_proj_kernel(x_ref, w_ref, z_ref):
    z_ref[...] = jnp.dot(x_ref[...], w_ref[...], preferred_element_type=F32).astype(BF16)


def in_proj(xb, w_in):
    t = xb.shape[0]
    tm, tn = 1024, 1536
    return pl.pallas_call(
        _in_proj_kernel,
        grid=(t // tm, D_IN // tn),
        in_specs=[pl.BlockSpec((tm, D_MODEL), lambda i, j: (i, 0)), pl.BlockSpec((D_MODEL, tn), lambda i, j: (0, j))],
        out_specs=pl.BlockSpec((tm, tn), lambda i, j: (i, j)),
        out_shape=jax.ShapeDtypeStruct((t, D_IN), BF16),
        compiler_params=_params(("arbitrary", "arbitrary")),
        name="in_proj",
    )(xb, w_in)


A_QB = 128
A_KB = A_QB + 2 * A_HALF


def _mixer_a_kernel(*refs, tl, seq_len, first, last):
    q_ref, kp_ref, kc_ref, kn_ref, vp_ref, vc_ref, vn_ref, bias_ref = refs[:8]
    refs = refs[8:]
    if not first:
        op_ref, lp_ref = refs[:2]
        refs = refs[2:]
    o_ref = refs[0]
    refs = refs[1:]
    if not last:
        l_ref = refs[0]
        refs = refs[1:]
    kf, vf = refs

    i = pl.program_id(2)
    kf[0:A_HALF, :] = kp_ref[0]
    kf[A_HALF:A_HALF + tl, :] = kc_ref[0]
    kf[A_HALF + tl:, :] = kn_ref[0]
    vf[0:A_HALF, :] = vp_ref[0]
    vf[A_HALF:A_HALF + tl, :] = vc_ref[0]
    vf[A_HALF + tl:, :] = vn_ref[0]

    left = lax.broadcasted_iota(jnp.int32, (1, 2 * A_HD), 1) < A_HD
    key_off = lax.broadcasted_iota(jnp.int32, (1, A_KB), 1) - A_HALF

    def block(j, carry):
        q0 = pl.multiple_of(j * A_QB, A_QB)
        key_pos = i * tl + q0 + key_off
        pen = jnp.where((key_pos >= 0) & (key_pos < seq_len), 0.0, NEG).astype(F32)
        for hp in range(A_HEADS // 2):
            cols = slice(hp * 2 * A_HD, (hp + 1) * 2 * A_HD)
            qp = q_ref[0, pl.ds(q0, A_QB), cols] * (A_HD ** -0.5)
            kp = kf[pl.ds(q0, A_KB), cols]
            vp = vf[pl.ds(q0, A_KB), cols]
            o_pair, lse_pair = None, None
            for hh in range(2):
                sel = left if hh == 0 else jnp.logical_not(left)
                qm = jnp.where(sel, qp, jnp.zeros_like(qp))
                s = lax.dot_general(qm, kp, (((1,), (1,)), ((), ())), preferred_element_type=F32)
                s = s + bias_ref[hp * 2 + hh] + pen
                m = jnp.max(s, axis=-1, keepdims=True)
                p = jnp.exp(s - m)
                l = jnp.sum(p, axis=-1, keepdims=True)
                o = jnp.dot(p.astype(BF16), vp, preferred_element_type=F32) / l
                lse = jnp.broadcast_to(m + jnp.log(l), o.shape)
                o_pair = o if hh == 0 else jnp.where(left, o_pair, o)
                lse_pair = lse if hh == 0 else jnp.where(left, lse_pair, lse)
            if not first:
                o_prev = op_ref[0, pl.ds(q0, A_QB), cols].astype(F32)
                lse_prev = lp_ref[0, pl.ds(q0, A_QB), cols]
                top = jnp.maximum(lse_prev, lse_pair)
                w_prev = jnp.exp(lse_prev - top)
                w_new = jnp.exp(lse_pair - top)
                den = w_prev + w_new
                o_pair = (w_prev * o_prev + w_new * o_pair) / den
                lse_pair = top + jnp.log(den)
            o_ref[0, pl.ds(q0, A_QB), cols] = o_pair.astype(o_ref.dtype)
            if not last:
                l_ref[0, pl.ds(q0, A_QB), cols] = lse_pair
        return carry

    lax.fori_loop(0, tl // A_QB, block, 0)


def _alibi_bias(dilation):
    slopes = 2.0 ** (-8.0 * jnp.arange(1, A_HEADS + 1, dtype=F32) / A_HEADS)
    rel = (jnp.arange(A_KB)[None, :] - A_HALF) - jnp.arange(A_QB)[:, None]
    dist = (jnp.abs(rel) * dilation).astype(F32)
    bias = -slopes[:, None, None] * dist[None]
    return jnp.where((jnp.abs(rel) <= A_HALF)[None], bias, NEG)


def mixer_a_group(z, batch, seq, group, prev, last):
    dilation = A_PATTERNS[group][1]
    first = prev is None
    strided_len = seq // dilation
    tl = min(512, strided_len)
    halo_per_tile = tl // A_HALF
    n_halo = strided_len // A_HALF
    zv = z.reshape(batch, strided_len, dilation * D_IN)

    def col(base):
        return lambda b, r, i: (b, i, r * IN_BLOCKS + base + group)

    def col_prev(base):
        return lambda b, r, i: (b, jnp.maximum(i * halo_per_tile - 1, 0), r * IN_BLOCKS + base + group)

    def col_next(base):
        return lambda b, r, i: (b, jnp.minimum((i + 1) * halo_per_tile, n_halo - 1), r * IN_BLOCKS + base + group)

    tile = (1, tl, HEAD_BLOCK)
    halo = (1, A_HALF, HEAD_BLOCK)
    state_spec = pl.BlockSpec(tile, lambda b, r, i: (b, i, r))
    in_specs = [pl.BlockSpec(tile, col(AQ_BLK)),
                pl.BlockSpec(halo, col_prev(AK_BLK)), pl.BlockSpec(tile, col(AK_BLK)), pl.BlockSpec(halo, col_next(AK_BLK)),
                pl.BlockSpec(halo, col_prev(AV_BLK)), pl.BlockSpec(tile, col(AV_BLK)), pl.BlockSpec(halo, col_next(AV_BLK)),
                _const_spec((A_HEADS, A_QB, A_KB))]
    args = [zv] * 7 + [_alibi_bias(dilation)]
    view = (batch, strided_len, dilation * HEAD_BLOCK)
    if not first:
        in_specs += [state_spec, state_spec]
        args += [prev[0].reshape(view), prev[1].reshape(view)]
    out_shape = [jax.ShapeDtypeStruct(view, BF16)]
    out_specs = [state_spec]
    if not last:
        out_shape.append(jax.ShapeDtypeStruct(view, F32))
        out_specs.append(state_spec)
    outs = pl.pallas_call(
        functools.partial(_mixer_a_kernel, tl=tl, seq_len=strided_len, first=first, last=last),
        grid=(batch, dilation, strided_len // tl),
        in_specs=in_specs,
        out_specs=out_specs,
        out_shape=out_shape,
        scratch_shapes=[pltpu.VMEM((tl + 2 * A_HALF, HEAD_BLOCK), BF16), pltpu.VMEM((tl + 2 * A_HALF, HEAD_BLOCK), BF16)],
        compiler_params=_params(("arbitrary", "arbitrary", "arbitrary")),
        name=f"mixer_a_g{group}",
    )(*args)
    return [o.reshape(batch * seq, HEAD_BLOCK) for o in outs]


def mixer_a(z, batch, seq):
    state = None
    for group in range(len(A_PATTERNS)):
        state = mixer_a_group(z, batch, seq, group, state, last=group == len(A_PATTERNS) - 1)
    return state[0]


def _mixer_b_kernel(q_ref, k_ref, v_ref, g_ref, dmat_ref, qdf_ref, qdb_ref, kdf_ref, kdb_ref, cdf_ref, cdb_ref,
                    o_ref, fwd_state, bwd_state, bwd_store, *, n_chunks):
    phase = pl.program_id(1)
    n = pl.program_id(2)
    qk_w = B_HEADS * B_DK
    head_of_lane = lax.broadcasted_iota(jnp.int32, (1, qk_w), 1) // B_DK

    def masked(t, h):
        return jnp.where(head_of_lane == h, t, jnp.zeros_like(t))

    def state_update(k_decayed, v):
        upd = None
        for h in range(B_HEADS):
            part = lax.dot_general(masked(k_decayed, h), v[:, h * B_DV:(h + 1) * B_DV], (((0,), (0,)), ((), ())),
                                   preferred_element_type=F32)
            upd = part if upd is None else upd + part
        return upd

    @pl.when(phase == 0)
    def _():
        @pl.when(n == 0)
        def _():
            bwd_state[...] = jnp.zeros_like(bwd_state)

        chunk = n_chunks - 1 - n
        bwd_store[chunk] = bwd_state[...].astype(BF16)
        k = k_ref[0] * (B_DK ** -0.5)
        k_dec = (k.astype(F32) * kdb_ref[...]).astype(BF16)
        bwd_state[...] = cdb_ref[...] * bwd_state[...] + state_update(k_dec, v_ref[0])

    @pl.when(phase == 1)
    def _():
        @pl.when(n == 0)
        def _():
            fwd_state[...] = jnp.zeros_like(fwd_state)

        q = q_ref[0]
        k = k_ref[0] * (B_DK ** -0.5)
        v = v_ref[0]
        q32 = q.astype(F32)
        q_fwd = (q32 * qdf_ref[...]).astype(BF16)
        q_bwd = (q32 * qdb_ref[...]).astype(BF16)
        k_dec = (k.astype(F32) * kdf_ref[...]).astype(BF16)
        states = jnp.concatenate([fwd_state[...].astype(BF16), bwd_store[n]], axis=0)
        for h in range(B_HEADS):
            vh = v[:, h * B_DV:(h + 1) * B_DV]
            s = lax.dot_general(masked(q, h), k, (((1,), (1,)), ((), ())), preferred_element_type=F32)
            inner = jnp.dot((s * dmat_ref[h]).astype(BF16), vh, preferred_element_type=F32)
            q_cross = jnp.concatenate([masked(q_fwd, h), masked(q_bwd, h)], axis=1)
            y = inner + jnp.dot(q_cross, states, preferred_element_type=F32)
            mu = jnp.mean(y, axis=-1, keepdims=True)
            c = y - mu
            var = jnp.mean(c * c, axis=-1, keepdims=True)
            yn = c * lax.rsqrt(var + GN_EPS)
            gate = g_ref[0, :, h * B_DV:(h + 1) * B_DV].astype(F32)
            o_ref[0, :, h * B_DV:(h + 1) * B_DV] = (gate * jax.nn.sigmoid(gate) * yn).astype(BF16)
        fwd_state[...] = cdf_ref[...] * fwd_state[...] + state_update(k_dec, v)


def _retention_tables(logit_fwd, logit_bwd):
    lg_f = jax.nn.log_sigmoid(logit_fwd.astype(F32))
    lg_b = jax.nn.log_sigmoid(logit_bwd.astype(F32))
    idx = jnp.arange(B_CHUNK, dtype=F32)
    diff = idx[:, None] - idx[None, :]
    causal = diff >= 0
    dmat = jnp.where(causal[None],
                     jnp.exp(lg_f[:, None, None] * jnp.where(causal, diff, 0.0)[None]),
                     jnp.exp(lg_b[:, None, None] * jnp.where(causal, 0.0, -diff)[None]))

    def per_lane(lg, power):
        return jnp.repeat(jnp.exp(lg[None, :] * power[:, None]), B_DK, axis=1)

    def per_row(lg):
        return jnp.broadcast_to(jnp.repeat(jnp.exp(lg * B_CHUNK), B_DK)[:, None], (B_HEADS * B_DK, B_DV))

    return (dmat, per_lane(lg_f, idx + 1), per_lane(lg_b, B_CHUNK - idx), per_lane(lg_f, B_CHUNK - 1 - idx),
            per_lane(lg_b, idx), per_row(lg_f), per_row(lg_b))


def mixer_b(z, batch, seq, logit_fwd, logit_bwd):
    n_chunks = seq // B_CHUNK
    zv = z.reshape(batch, seq, D_IN)
    qk_w = B_HEADS * B_DK
    v_w = B_HEADS * B_DV

    def scan_chunk(ph, n):
        return (1 - ph) * (n_chunks - 1 - n) + ph * n

    tables = _retention_tables(logit_fwd, logit_bwd)
    in_specs = [pl.BlockSpec((1, B_CHUNK, qk_w), lambda b, ph, n: (b, ph * n, BQ_BLK256)),
                pl.BlockSpec((1, B_CHUNK, qk_w), lambda b, ph, n: (b, scan_chunk(ph, n), BK_BLK256)),
                pl.BlockSpec((1, B_CHUNK, v_w), lambda b, ph, n: (b, scan_chunk(ph, n), BV_BLK)),
                pl.BlockSpec((1, B_CHUNK, v_w), lambda b, ph, n: (b, ph * n, BG_BLK))]
    in_specs += [_const_spec(t.shape) for t in tables]
    out = pl.pallas_call(
        functools.partial(_mixer_b_kernel, n_chunks=n_chunks),
        grid=(batch, 2, n_chunks),
        in_specs=in_specs,
        out_specs=pl.BlockSpec((1, B_CHUNK, v_w), lambda b, ph, n: (b, ph * n, 0)),
        out_shape=jax.ShapeDtypeStruct((batch, seq, v_w), BF16),
        scratch_shapes=[pltpu.VMEM((qk_w, B_DV), F32), pltpu.VMEM((qk_w, B_DV), F32),
                        pltpu.VMEM((n_chunks, qk_w, B_DV), BF16)],
        compiler_params=_params(("arbitrary", "arbitrary", "arbitrary")),
        name="mixer_b",
    )(zv, zv, zv, zv, *tables)
    return out.reshape(batch * seq, v_w)


C_TQ = C_QR * GRID_W
C_KROWS = 3 * C_QR
C_TK = C_KROWS * GRID_W
C_PAIRS = C_KROWS // 2
C_NTAB = 2 * C_KH - 2


def _mixer_c_kernel(q_ref, kp_ref, kc_ref, kn_ref, vp_ref, vc_ref, vn_ref, tab_ref, o_ref, kf, vf, *, rows):
    blk = pl.program_id(1)
    kf[0:C_TQ, :] = kp_ref[0]
    kf[C_TQ:2 * C_TQ, :] = kc_ref[0]
    kf[2 * C_TQ:, :] = kn_ref[0]
    vf[0:C_TQ, :] = vp_ref[0]
    vf[C_TQ:2 * C_TQ, :] = vc_ref[0]
    vf[2 * C_TQ:, :] = vn_ref[0]

    left = lax.broadcasted_iota(jnp.int32, (1, 2 * C_HD), 1) < C_HD
    key_row = blk * C_QR - C_QR + lax.broadcasted_iota(jnp.int32, (1, C_TK), 1) // GRID_W
    pens = []
    for a in range(C_QR):
        row_start = jnp.clip(blk * C_QR + a - C_KH // 2, 0, rows - C_KH)
        pens.append(jnp.where((key_row >= row_start) & (key_row < row_start + C_KH), 0.0, NEG).astype(F32))

    for hp in range(C_HEADS // 2):
        cols = slice(hp * 2 * C_HD, (hp + 1) * 2 * C_HD)
        qp = q_ref[0, :, cols] * (C_HD ** -0.5)
        kp = kf[:, cols]
        vp = vf[:, cols]
        o_pair = None
        for hh in range(2):
            h = hp * 2 + hh
            sel = left if hh == 0 else jnp.logical_not(left)
            qm = jnp.where(sel, qp, jnp.zeros_like(qp))
            s = lax.dot_general(qm, kp, (((1,), (1,)), ((), ())), preferred_element_type=F32)
            probs, dens = [], []
            for a in range(C_QR):
                bias = jnp.concatenate([tab_ref[h, 2 * t - C_QR - a + C_KH - 1] for t in range(C_PAIRS)], axis=1)
                sa = s[a * GRID_W:(a + 1) * GRID_W, :] + bias + pens[a]
                m = jnp.max(sa, axis=-1, keepdims=True)
                p = jnp.exp(sa - m)
                dens.append(jnp.sum(p, axis=-1, keepdims=True))
                probs.append(p.astype(BF16))
            p_all = jnp.concatenate(probs, axis=0)
            den = jnp.concatenate(dens, axis=0)
            o = jnp.dot(p_all, vp, preferred_element_type=F32) / den
            o_pair = o if hh == 0 else jnp.where(left, o_pair, o)
        o_ref[0, :, cols] = o_pair.astype(BF16)


def _neighbourhood_bias(rpb):
    qc = jnp.arange(GRID_W)[:, None]
    lane = jnp.arange(2 * GRID_W)[None, :]
    kc = lane % GRID_W
    col_start = jnp.clip(qc - C_KW // 2, 0, GRID_W - C_KW)
    col_ok = (kc >= col_start) & (kc < col_start + C_KW)
    dc = jnp.clip(kc - qc, -(C_KW - 1), C_KW - 1) + (C_KW - 1)
    dr = jnp.arange(C_NTAB)[:, None, None] + (lane // GRID_W)[None]
    dr = jnp.broadcast_to(dr, (C_NTAB, GRID_W, 2 * GRID_W))
    vals = rpb.astype(F32)[:, dr, jnp.broadcast_to(dc[None], dr.shape)]
    return jnp.where(col_ok[None, None], vals, NEG)


def mixer_c(z, batch, seq, rpb):
    rows = seq // GRID_W
    n_blk = rows // C_QR
    zv = z.reshape(batch, seq, D_IN)
    tile = (1, C_TQ, HEAD_BLOCK)
    tab = _neighbourhood_bias(rpb)

    def above(c):
        return lambda b, i: (b, jnp.maximum(i - 1, 0), c)

    def here(c):
        return lambda b, i: (b, i, c)

    def below(c):
        return lambda b, i: (b, jnp.minimum(i + 1, n_blk - 1), c)

    out = pl.pallas_call(
        functools.partial(_mixer_c_kernel, rows=rows),
        grid=(batch, n_blk),
        in_specs=[pl.BlockSpec(tile, here(CQ_BLK)),
                  pl.BlockSpec(tile, above(CK_BLK)), pl.BlockSpec(tile, here(CK_BLK)), pl.BlockSpec(tile, below(CK_BLK)),
                  pl.BlockSpec(tile, above(CV_BLK)), pl.BlockSpec(tile, here(CV_BLK)), pl.BlockSpec(tile, below(CV_BLK)),
                  _const_spec(tab.shape)],
        out_specs=pl.BlockSpec(tile, lambda b, i: (b, i, 0)),
        out_shape=jax.ShapeDtypeStruct((batch, seq, HEAD_BLOCK), BF16),
        scratch_shapes=[pltpu.VMEM((C_TK, HEAD_BLOCK), BF16), pltpu.VMEM((C_TK, HEAD_BLOCK), BF16)],
        compiler_params=_params(("arbitrary", "arbitrary")),
        name="mixer_c",
    )(zv, zv, zv, zv, zv, zv, zv, tab)
    return out.reshape(batch * seq, HEAD_BLOCK)


def _merge_kernel(x_ref, ya_ref, yb_ref, yc_ref, gates_ref, wa_ref, wb_ref, wc_ref, wo_ref, g_ref, b_ref, o_ref):
    merged = None
    for br, (y_ref, w_ref) in enumerate(((ya_ref, wa_ref), (yb_ref, wb_ref), (yc_ref, wc_ref))):
        proj = jnp.dot(y_ref[...], w_ref[...], preferred_element_type=F32)
        gate = jax.nn.sigmoid(gates_ref[:, br * D_MODEL:(br + 1) * D_MODEL].astype(F32))
        merged = gate * proj if merged is None else merged + gate * proj
    out = jnp.dot(merged.astype(BF16), wo_ref[...], preferred_element_type=F32)
    o_ref[...] = _layer_norm(ALPHA * x_ref[...] + out, g_ref[...], b_ref[...])


def merge_out_ln(x, ya, yb, yc, z, wa, wb, wc, wo, g, b):
    t = x.shape[0]
    tm = 512
    row = pl.BlockSpec((tm, D_MODEL), lambda i: (i, 0))
    br = pl.BlockSpec((tm, HEAD_BLOCK), lambda i: (i, 0))
    w_br = _const_spec((HEAD_BLOCK, D_MODEL))
    return pl.pallas_call(
        _merge_kernel,
        grid=(t // tm,),
        in_specs=[row, br, br, br, pl.BlockSpec((tm, GATE_W), lambda i: (i, 0)), w_br, w_br, w_br,
                  _const_spec((D_MODEL, D_MODEL)), _const_spec((1, D_MODEL)), _const_spec((1, D_MODEL))],
        out_specs=row,
        out_shape=jax.ShapeDtypeStruct((t, D_MODEL), F32),
        compiler_params=_params(("arbitrary",)),
        name="merge_out_ln",
    )(x, ya, yb, yc, z, wa, wb, wc, wo, g, b)


def _permute_in_columns(w_in):
    return jnp.concatenate([w_in[:, D_IN - GATE_W:], w_in[:, :D_IN - GATE_W]], axis=1)


def _trunk(x, layers):
    batch, seq, _ = x.shape
    x = x.reshape(batch * seq, D_MODEL)
    for p in layers:
        x1, x1b = ffn_ln(x, p["wg1"], p["wu1"], p["wd1"], p["g1"], p["b1"], with_bf16=True)
        z = in_proj(x1b, p["w_in"])
        ya = mixer_a(z, batch, seq)
        yb = mixer_b(z, batch, seq, p["logit_fwd"], p["logit_bwd"])
        yc = mixer_c(z, batch, seq, p["rpb"])
        x2 = merge_out_ln(x1, ya, yb, yc, z, p["wa"], p["wb"], p["wc"], p["wo"], p["g2"], p["b2"])
        (x,) = ffn_ln(x2, p["wg2"], p["wu2"], p["wd2"], p["g3"], p["b3"], with_bf16=False)
    return x.reshape(batch, seq, D_MODEL)


def kernel(x_prompt, x_sample, ffn1_w_gate, ffn1_w_up, ffn1_w_down, ln1_g, ln1_b, w_in, ret_logit_fwd, ret_logit_bwd, na_rpb, w_branch_a, w_branch_b, w_branch_c, w_out, ln2_g, ln2_b, ffn2_w_gate, ffn2_w_up, ffn2_w_down, ln3_g, ln3_b):
    def vec(v):
        return v.astype(F32).reshape(1, D_MODEL)

    layers = []
    for i in range(DEPTH):
        layers.append(dict(
            wg1=ffn1_w_gate[i].astype(BF16), wu1=ffn1_w_up[i].astype(BF16), wd1=ffn1_w_down[i].astype(BF16),
            g1=vec(ln1_g[i]), b1=vec(ln1_b[i]),
            w_in=_permute_in_columns(w_in[i]).astype(BF16),
            logit_fwd=ret_logit_fwd[i], logit_bwd=ret_logit_bwd[i], rpb=na_rpb[i],
            wa=w_branch_a[i].astype(BF16), wb=w_branch_b[i].astype(BF16), wc=w_branch_c[i].astype(BF16),
            wo=w_out[i].astype(BF16), g2=vec(ln2_g[i]), b2=vec(ln2_b[i]),
            wg2=ffn2_w_gate[i].astype(BF16), wu2=ffn2_w_up[i].astype(BF16), wd2=ffn2_w_down[i].astype(BF16),
            g3=vec(ln3_g[i]), b3=vec(ln3_b[i])))
    return (_trunk(x_prompt, layers), _trunk(x_sample, layers))
```

```python
import functools

import jax
import jax.numpy as jnp
from jax import lax
from jax.experimental import pallas as pl
from jax.experimental.pallas import tpu as pltpu

F32 = jnp.float32
BF16 = jnp.bfloat16

D_MODEL = 1024
DEPTH = 2
D_FF = 2816
LN_EPS = 1e-5
GN_EPS = 1e-5
ALPHA = (2 * DEPTH) ** 0.25

A_PATTERNS = ((128, 1), (512, 4), (2048, 16))
A_GROUPS = len(A_PATTERNS)
A_HEADS = 8
A_HD = 64
A_HALF = 64
B_HEADS = 4
B_DK = 64
B_DV = 128
B_CHUNK = 128
C_HEADS = 8
C_HD = 64
GRID_W = 64
C_KH = 8
C_KW = 16
C_QR = 4
D_IN = 10752
NEG = -1e30

GATE_W = 3 * D_MODEL
HEAD_BLOCK = 512
A_QKV_W = 3 * HEAD_BLOCK
MAIN_W = D_IN - (A_GROUPS - 1) * A_QKV_W
AQ_BLK, AK_BLK, AV_BLK = 6, 7, 8
BQ_BLK256, BK_BLK256 = 18, 19
BV_BLK, BG_BLK = 10, 11
CQ_BLK, CK_BLK, CV_BLK = 12, 13, 14

VMEM_LIMIT = 56 * 1024 * 1024
FF_CHUNKS = ((0, 512), (512, 1024), (1024, 1536), (1536, 2048), (2048, 2560), (2560, 2816))


def _params(sem):
    return pltpu.CompilerParams(dimension_semantics=sem, vmem_limit_bytes=VMEM_LIMIT)


def _const_spec(shape):
    zeros = (0,) * len(shape)
    return pl.BlockSpec(shape, lambda *_: zeros)


def _layer_norm(r, g, b):
    mu = jnp.mean(r, axis=-1, keepdims=True)
    c = r - mu
    var = jnp.mean(c * c, axis=-1, keepdims=True)
    return c * lax.rsqrt(var + LN_EPS) * g + b


def _ffn_ln_kernel(x_ref, wg_ref, wu_ref, wd_ref, g_ref, b_ref, o_ref, *maybe_ob_ref):
    x = x_ref[...]
    xb = x.astype(BF16)
    acc = None
    for c0, c1 in FF_CHUNKS:
        gate = jnp.dot(xb, wg_ref[:, c0:c1], preferred_element_type=F32)
        up = jnp.dot(xb, wu_ref[:, c0:c1], preferred_element_type=F32)
        h = (gate * jax.nn.sigmoid(gate) * up).astype(BF16)
        part = jnp.dot(h, wd_ref[c0:c1, :], preferred_element_type=F32)
        acc = part if acc is None else acc + part
    y = _layer_norm(ALPHA * x + 0.5 * acc, g_ref[...], b_ref[...])
    o_ref[...] = y
    if maybe_ob_ref:
        maybe_ob_ref[0][...] = y.astype(BF16)


def ffn_ln(x, wg, wu, wd, g, b, with_bf16):
    t = x.shape[0]
    tm = 512
    row = pl.BlockSpec((tm, D_MODEL), lambda i: (i, 0))
    out_shape = [jax.ShapeDtypeStruct((t, D_MODEL), F32)]
    out_specs = [row]
    if with_bf16:
        out_shape.append(jax.ShapeDtypeStruct((t, D_MODEL), BF16))
        out_specs.append(row)
    return pl.pallas_call(
        _ffn_ln_kernel,
        grid=(t // tm,),
        in_specs=[row, _const_spec((D_MODEL, D_FF)), _const_spec((D_MODEL, D_FF)), _const_spec((D_FF, D_MODEL)),
                  _const_spec((1, D_MODEL)), _const_spec((1, D_MODEL))],
        out_specs=out_specs,
        out_shape=out_shape,
        compiler_params=_params(("arbitrary",)),
        name="ffn_ln",
    )(x, wg, wu, wd, g, b)


def _in_proj_kernel(x_ref, w_ref, z_ref):
    z_ref[...] = jnp.dot(x_ref[...], w_ref[...], preferred_element_type=F32).astype(BF16)


def in_proj(xb, w_in):
    t = xb.shape[0]
    width = w_in.shape[1]
    tm, tn = 1024, A_QKV_W
    return pl.pallas_call(
        _in_proj_kernel,
        grid=(t // tm, width // tn),
        in_specs=[pl.BlockSpec((tm, D_MODEL), lambda i, j: (i, 0)), pl.BlockSpec((D_MODEL, tn), lambda i, j: (0, j))],
        out_specs=pl.BlockSpec((tm, tn), lambda i, j: (i, j)),
        out_shape=jax.ShapeDtypeStruct((t, width), BF16),
        compiler_params=_params(("arbitrary", "arbitrary")),
        name="in_proj",
    )(xb, w_in)


A_QB = 128
A_KB = A_QB + 2 * A_HALF


def _mixer_a_kernel(q_ref, kp_ref, kc_ref, kn_ref, vp_ref, vc_ref, vn_ref, bias_ref, o_ref, l_ref, kf, vf, *,
                    tl, seq_len):
    i = pl.program_id(2)
    kf[0:A_HALF, :] = kp_ref[0, 0]
    kf[A_HALF:A_HALF + tl, :] = kc_ref[0, 0]
    kf[A_HALF + tl:, :] = kn_ref[0, 0]
    vf[0:A_HALF, :] = vp_ref[0, 0]
    vf[A_HALF:A_HALF + tl, :] = vc_ref[0, 0]
    vf[A_HALF + tl:, :] = vn_ref[0, 0]

    left = lax.broadcasted_iota(jnp.int32, (1, 2 * A_HD), 1) < A_HD
    key_off = lax.broadcasted_iota(jnp.int32, (1, A_KB), 1) - A_HALF

    def block(j, carry):
        q0 = pl.multiple_of(j * A_QB, A_QB)
        key_pos = i * tl + q0 + key_off
        pen = jnp.where((key_pos >= 0) & (key_pos < seq_len), 0.0, NEG).astype(F32)
        for hp in range(A_HEADS // 2):
            cols = slice(hp * 2 * A_HD, (hp + 1) * 2 * A_HD)
            qp = q_ref[0, 0, pl.ds(q0, A_QB), cols] * (A_HD ** -0.5)
            kp = kf[pl.ds(q0, A_KB), cols]
            vp = vf[pl.ds(q0, A_KB), cols]
            o_pair, lse_pair = None, None
            for hh in range(2):
                sel = left if hh == 0 else jnp.logical_not(left)
                qm = jnp.where(sel, qp, jnp.zeros_like(qp))
                s = lax.dot_general(qm, kp, (((1,), (1,)), ((), ())), preferred_element_type=F32)
                s = s + bias_ref[hp * 2 + hh] + pen
                m = jnp.max(s, axis=-1, keepdims=True)
                p = jnp.exp(s - m)
                l = jnp.sum(p, axis=-1, keepdims=True)
                o = jnp.dot(p.astype(BF16), vp, preferred_element_type=F32) / l
                lse = jnp.broadcast_to(m + jnp.log(l), o.shape)
                o_pair = o if hh == 0 else jnp.where(left, o_pair, o)
                lse_pair = lse if hh == 0 else jnp.where(left, lse_pair, lse)
            o_ref[0, 0, pl.ds(q0, A_QB), cols] = o_pair.astype(BF16)
            l_ref[0, 0, pl.ds(q0, A_QB), cols] = lse_pair
        return carry

    lax.fori_loop(0, tl // A_QB, block, 0)


def _alibi_bias(dilation):
    slopes = 2.0 ** (-8.0 * jnp.arange(1, A_HEADS + 1, dtype=F32) / A_HEADS)
    rel = (jnp.arange(A_KB)[None, :] - A_HALF) - jnp.arange(A_QB)[:, None]
    dist = (jnp.abs(rel) * dilation).astype(F32)
    bias = -slopes[:, None, None] * dist[None]
    return jnp.where((jnp.abs(rel) <= A_HALF)[None], bias, NEG)


def mixer_a_group(zg, dilation, q_blk, k_blk, v_blk):
    batch, _, strided_len, _ = zg.shape
    tl = min(512, strided_len)
    halo_per_tile = tl // A_HALF
    n_halo = strided_len // A_HALF

    def cur(c):
        return lambda b, r, i: (b, r, i, c)

    def before(c):
        return lambda b, r, i: (b, r, jnp.maximum(i * halo_per_tile - 1, 0), c)

    def after(c):
        return lambda b, r, i: (b, r, jnp.minimum((i + 1) * halo_per_tile, n_halo - 1), c)

    tile = (1, 1, tl, HEAD_BLOCK)
    halo = (1, 1, A_HALF, HEAD_BLOCK)
    out_spec = pl.BlockSpec(tile, cur(0))
    out_dims = (batch, dilation, strided_len, HEAD_BLOCK)
    return pl.pallas_call(
        functools.partial(_mixer_a_kernel, tl=tl, seq_len=strided_len),
        grid=(batch, dilation, strided_len // tl),
        in_specs=[pl.BlockSpec(tile, cur(q_blk)),
                  pl.BlockSpec(halo, before(k_blk)), pl.BlockSpec(tile, cur(k_blk)), pl.BlockSpec(halo, after(k_blk)),
                  pl.BlockSpec(halo, before(v_blk)), pl.BlockSpec(tile, cur(v_blk)), pl.BlockSpec(halo, after(v_blk)),
                  _const_spec((A_HEADS, A_QB, A_KB))],
        out_specs=[out_spec, out_spec],
        out_shape=[jax.ShapeDtypeStruct(out_dims, BF16), jax.ShapeDtypeStruct(out_dims, F32)],
        scratch_shapes=[pltpu.VMEM((tl + 2 * A_HALF, HEAD_BLOCK), BF16), pltpu.VMEM((tl + 2 * A_HALF, HEAD_BLOCK), BF16)],
        compiler_params=_params(("arbitrary", "arbitrary", "arbitrary")),
        name=f"mixer_a_d{dilation}",
    )(zg, zg, zg, zg, zg, zg, zg, _alibi_bias(dilation))


def _by_residue(x, batch, seq, dilation):
    w = x.shape[-1]
    return x.reshape(batch, seq // dilation, dilation, w).transpose(0, 2, 1, 3).reshape(batch * seq, w)


def _by_token(x, batch, seq):
    return x.transpose(0, 2, 1, 3).reshape(batch * seq, x.shape[-1])


def _mixer_b_kernel(q_ref, k_ref, v_ref, g_ref, dmat_ref, qdf_ref, qdb_ref, kdf_ref, kdb_ref, cdf_ref, cdb_ref,
                    o_ref, fwd_state, bwd_state, bwd_store, *, n_chunks):
    phase = pl.program_id(1)
    n = pl.program_id(2)
    qk_w = B_HEADS * B_DK
    head_of_lane = lax.broadcasted_iota(jnp.int32, (1, qk_w), 1) // B_DK

    def masked(t, h):
        return jnp.where(head_of_lane == h, t, jnp.zeros_like(t))

    def state_update(k_decayed, v):
        upd = None
        for h in range(B_HEADS):
            part = lax.dot_general(masked(k_decayed, h), v[:, h * B_DV:(h + 1) * B_DV], (((0,), (0,)), ((), ())),
                                   preferred_element_type=F32)
            upd = part if upd is None else upd + part
        return upd

    @pl.when(phase == 0)
    def _():
        @pl.when(n == 0)
        def _():
            bwd_state[...] = jnp.zeros_like(bwd_state)

        chunk = n_chunks - 1 - n
        bwd_store[chunk] = bwd_state[...].astype(BF16)
        k = k_ref[0] * (B_DK ** -0.5)
        k_dec = (k.astype(F32) * kdb_ref[...]).astype(BF16)
        bwd_state[...] = cdb_ref[...] * bwd_state[...] + state_update(k_dec, v_ref[0])

    @pl.when(phase == 1)
    def _():
        @pl.when(n == 0)
        def _():
            fwd_state[...] = jnp.zeros_like(fwd_state)

        q = q_ref[0]
        k = k_ref[0] * (B_DK ** -0.5)
        v = v_ref[0]
        q32 = q.astype(F32)
        q_fwd = (q32 * qdf_ref[...]).astype(BF16)
        q_bwd = (q32 * qdb_ref[...]).astype(BF16)
        k_dec = (k.astype(F32) * kdf_ref[...]).astype(BF16)
        states = jnp.concatenate([fwd_state[...].astype(BF16), bwd_store[n]], axis=0)
        for h in range(B_HEADS):
            vh = v[:, h * B_DV:(h + 1) * B_DV]
            s = lax.dot_general(masked(q, h), k, (((1,), (1,)), ((), ())), preferred_element_type=F32)
            inner = jnp.dot((s * dmat_ref[h]).astype(BF16), vh, preferred_element_type=F32)
            q_cross = jnp.concatenate([masked(q_fwd, h), masked(q_bwd, h)], axis=1)
            y = inner + jnp.dot(q_cross, states, preferred_element_type=F32)
            mu = jnp.mean(y, axis=-1, keepdims=True)
            c = y - mu
            var = jnp.mean(c * c, axis=-1, keepdims=True)
            yn = c * lax.rsqrt(var + GN_EPS)
            gate = g_ref[0, :, h * B_DV:(h + 1) * B_DV].astype(F32)
            o_ref[0, :, h * B_DV:(h + 1) * B_DV] = (gate * jax.nn.sigmoid(gate) * yn).astype(BF16)
        fwd_state[...] = cdf_ref[...] * fwd_state[...] + state_update(k_dec, v)


def _retention_tables(logit_fwd, logit_bwd):
    lg_f = jax.nn.log_sigmoid(logit_fwd.astype(F32))
    lg_b = jax.nn.log_sigmoid(logit_bwd.astype(F32))
    idx = jnp.arange(B_CHUNK, dtype=F32)
    diff = idx[:, None] - idx[None, :]
    causal = diff >= 0
    dmat = jnp.where(causal[None],
                     jnp.exp(lg_f[:, None, None] * jnp.where(causal, diff, 0.0)[None]),
                     jnp.exp(lg_b[:, None, None] * jnp.where(causal, 0.0, -diff)[None]))

    def per_lane(lg, power):
        return jnp.repeat(jnp.exp(lg[None, :] * power[:, None]), B_DK, axis=1)

    def per_row(lg):
        return jnp.broadcast_to(jnp.repeat(jnp.exp(lg * B_CHUNK), B_DK)[:, None], (B_HEADS * B_DK, B_DV))

    return (dmat, per_lane(lg_f, idx + 1), per_lane(lg_b, B_CHUNK - idx), per_lane(lg_f, B_CHUNK - 1 - idx),
            per_lane(lg_b, idx), per_row(lg_f), per_row(lg_b))


def mixer_b(z, batch, seq, logit_fwd, logit_bwd):
    n_chunks = seq // B_CHUNK
    zv = z.reshape(batch, seq, MAIN_W)
    qk_w = B_HEADS * B_DK
    v_w = B_HEADS * B_DV

    def scan_chunk(ph, n):
        return (1 - ph) * (n_chunks - 1 - n) + ph * n

    tables = _retention_tables(logit_fwd, logit_bwd)
    in_specs = [pl.BlockSpec((1, B_CHUNK, qk_w), lambda b, ph, n: (b, ph * n, BQ_BLK256)),
                pl.BlockSpec((1, B_CHUNK, qk_w), lambda b, ph, n: (b, scan_chunk(ph, n), BK_BLK256)),
                pl.BlockSpec((1, B_CHUNK, v_w), lambda b, ph, n: (b, scan_chunk(ph, n), BV_BLK)),
                pl.BlockSpec((1, B_CHUNK, v_w), lambda b, ph, n: (b, ph * n, BG_BLK))]
    in_specs += [_const_spec(t.shape) for t in tables]
    out = pl.pallas_call(
        functools.partial(_mixer_b_kernel, n_chunks=n_chunks),
        grid=(batch, 2, n_chunks),
        in_specs=in_specs,
        out_specs=pl.BlockSpec((1, B_CHUNK, v_w), lambda b, ph, n: (b, ph * n, 0)),
        out_shape=jax.ShapeDtypeStruct((batch, seq, v_w), BF16),
        scratch_shapes=[pltpu.VMEM((qk_w, B_DV), F32), pltpu.VMEM((qk_w, B_DV), F32),
                        pltpu.VMEM((n_chunks, qk_w, B_DV), BF16)],
        compiler_params=_params(("arbitrary", "arbitrary", "arbitrary")),
        name="mixer_b",
    )(zv, zv, zv, zv, *tables)
    return out.reshape(batch * seq, v_w)


C_TQ = C_QR * GRID_W
C_KROWS = 3 * C_QR
C_TK = C_KROWS * GRID_W
C_PAIRS = C_KROWS // 2
C_NTAB = 2 * C_KH - 2


def _mixer_c_kernel(q_ref, kp_ref, kc_ref, kn_ref, vp_ref, vc_ref, vn_ref, tab_ref, o_ref, kf, vf, *, rows):
    blk = pl.program_id(1)
    kf[0:C_TQ, :] = kp_ref[0]
    kf[C_TQ:2 * C_TQ, :] = kc_ref[0]
    kf[2 * C_TQ:, :] = kn_ref[0]
    vf[0:C_TQ, :] = vp_ref[0]
    vf[C_TQ:2 * C_TQ, :] = vc_ref[0]
    vf[2 * C_TQ:, :] = vn_ref[0]

    left = lax.broadcasted_iota(jnp.int32, (1, 2 * C_HD), 1) < C_HD
    key_row = blk * C_QR - C_QR + lax.broadcasted_iota(jnp.int32, (1, C_TK), 1) // GRID_W
    pens = []
    for a in range(C_QR):
        row_start = jnp.clip(blk * C_QR + a - C_KH // 2, 0, rows - C_KH)
        pens.append(jnp.where((key_row >= row_start) & (key_row < row_start + C_KH), 0.0, NEG).astype(F32))

    for hp in range(C_HEADS // 2):
        cols = slice(hp * 2 * C_HD, (hp + 1) * 2 * C_HD)
        qp = q_ref[0, :, cols] * (C_HD ** -0.5)
        kp = kf[:, cols]
        vp = vf[:, cols]
        o_pair = None
        for hh in range(2):
            h = hp * 2 + hh
            sel = left if hh == 0 else jnp.logical_not(left)
            qm = jnp.where(sel, qp, jnp.zeros_like(qp))
            s = lax.dot_general(qm, kp, (((1,), (1,)), ((), ())), preferred_element_type=F32)
            probs, dens = [], []
            for a in range(C_QR):
                bias = jnp.concatenate([tab_ref[h, 2 * t - C_QR - a + C_KH - 1] for t in range(C_PAIRS)], axis=1)
                sa = s[a * GRID_W:(a + 1) * GRID_W, :] + bias + pens[a]
                m = jnp.max(sa, axis=-1, keepdims=True)
                p = jnp.exp(sa - m)
                dens.append(jnp.sum(p, axis=-1, keepdims=True))
                probs.append(p.astype(BF16))
            p_all = jnp.concatenate(probs, axis=0)
            den = jnp.concatenate(dens, axis=0)
            o = jnp.dot(p_all, vp, preferred_element_type=F32) / den
            o_pair = o if hh == 0 else jnp.where(left, o_pair, o)
        o_ref[0, :, cols] = o_pair.astype(BF16)


def _neighbourhood_bias(rpb):
    qc = jnp.arange(GRID_W)[:, None]
    kc = jnp.arange(GRID_W)[None, :]
    col_start = jnp.clip(qc - C_KW // 2, 0, GRID_W - C_KW)
    col_ok = (kc >= col_start) & (kc < col_start + C_KW)
    onehot = ((kc - qc + (C_KW - 1))[:, :, None] == jnp.arange(2 * C_KW - 1)[None, None, :]).astype(F32)
    band = jnp.einsum("hrd,qkd->hrqk", rpb.astype(F32), onehot, precision=lax.Precision.HIGHEST)
    band = jnp.where(col_ok[None, None], band, NEG)
    return jnp.concatenate([band[:, :C_NTAB], band[:, 1:C_NTAB + 1]], axis=-1)


def mixer_c(z, batch, seq, rpb):
    rows = seq // GRID_W
    n_blk = rows // C_QR
    zv = z.reshape(batch, seq, MAIN_W)
    tile = (1, C_TQ, HEAD_BLOCK)
    tab = _neighbourhood_bias(rpb)

    def above(c):
        return lambda b, i: (b, jnp.maximum(i - 1, 0), c)

    def here(c):
        return lambda b, i: (b, i, c)

    def below(c):
        return lambda b, i: (b, jnp.minimum(i + 1, n_blk - 1), c)

    out = pl.pallas_call(
        functools.partial(_mixer_c_kernel, rows=rows),
        grid=(batch, n_blk),
        in_specs=[pl.BlockSpec(tile, here(CQ_BLK)),
                  pl.BlockSpec(tile, above(CK_BLK)), pl.BlockSpec(tile, here(CK_BLK)), pl.BlockSpec(tile, below(CK_BLK)),
                  pl.BlockSpec(tile, above(CV_BLK)), pl.BlockSpec(tile, here(CV_BLK)), pl.BlockSpec(tile, below(CV_BLK)),
                  _const_spec(tab.shape)],
        out_specs=pl.BlockSpec(tile, lambda b, i: (b, i, 0)),
        out_shape=jax.ShapeDtypeStruct((batch, seq, HEAD_BLOCK), BF16),
        scratch_shapes=[pltpu.VMEM((C_TK, HEAD_BLOCK), BF16), pltpu.VMEM((C_TK, HEAD_BLOCK), BF16)],
        compiler_params=_params(("arbitrary", "arbitrary")),
        name="mixer_c",
    )(zv, zv, zv, zv, zv, zv, zv, tab)
    return out.reshape(batch * seq, HEAD_BLOCK)


def _merge_kernel(x_ref, oa0_ref, oa1_ref, oa2_ref, la0_ref, la1_ref, la2_ref, yb_ref, yc_ref, gates_ref,
                  wa_ref, wb_ref, wc_ref, wo_ref, g_ref, b_ref, o_ref):
    lses = [la0_ref[...], la1_ref[...], la2_ref[...]]
    top = jnp.maximum(jnp.maximum(lses[0], lses[1]), lses[2])
    num, den = None, None
    for o_g_ref, lse in zip((oa0_ref, oa1_ref, oa2_ref), lses):
        w = jnp.exp(lse - top)
        num = w * o_g_ref[...].astype(F32) if num is None else num + w * o_g_ref[...].astype(F32)
        den = w if den is None else den + w
    ya = (num / den).astype(BF16)

    merged = None
    for br, (y, w_ref) in enumerate(((ya, wa_ref), (yb_ref[...], wb_ref), (yc_ref[...], wc_ref))):
        proj = jnp.dot(y, w_ref[...], preferred_element_type=F32)
        gate = jax.nn.sigmoid(gates_ref[:, br * D_MODEL:(br + 1) * D_MODEL].astype(F32))
        merged = gate * proj if merged is None else merged + gate * proj
    out = jnp.dot(merged.astype(BF16), wo_ref[...], preferred_element_type=F32)
    o_ref[...] = _layer_norm(ALPHA * x_ref[...] + out, g_ref[...], b_ref[...])


def merge_out_ln(x, oa, la, yb, yc, z, wa, wb, wc, wo, g, b):
    t = x.shape[0]
    tm = 512
    row = pl.BlockSpec((tm, D_MODEL), lambda i: (i, 0))
    br = pl.BlockSpec((tm, HEAD_BLOCK), lambda i: (i, 0))
    w_br = _const_spec((HEAD_BLOCK, D_MODEL))
    return pl.pallas_call(
        _merge_kernel,
        grid=(t // tm,),
        in_specs=[row] + [br] * 8 + [pl.BlockSpec((tm, GATE_W), lambda i: (i, 0)), w_br, w_br, w_br,
                                      _const_spec((D_MODEL, D_MODEL)), _const_spec((1, D_MODEL)), _const_spec((1, D_MODEL))],
        out_specs=row,
        out_shape=jax.ShapeDtypeStruct((t, D_MODEL), F32),
        compiler_params=_params(("arbitrary",)),
        name="merge_out_ln",
    )(x, *oa, *la, yb, yc, z, wa, wb, wc, wo, g, b)


def _split_in_weights(w_in):
    a_w = A_GROUPS * HEAD_BLOCK
    aq, ak, av = w_in[:, :a_w], w_in[:, a_w:2 * a_w], w_in[:, 2 * a_w:3 * a_w]
    rest = w_in[:, 3 * a_w:D_IN - GATE_W]

    def group(g):
        cols = slice(g * HEAD_BLOCK, (g + 1) * HEAD_BLOCK)
        return jnp.concatenate([aq[:, cols], ak[:, cols], av[:, cols]], axis=1)

    main = jnp.concatenate([w_in[:, D_IN - GATE_W:], group(0), rest], axis=1)
    return main.astype(BF16), [group(g).astype(BF16) for g in range(1, A_GROUPS)]


def _trunk(x, layers):
    batch, seq, _ = x.shape
    x = x.reshape(batch * seq, D_MODEL)
    for p in layers:
        x1, x1b = ffn_ln(x, p["wg1"], p["wu1"], p["wd1"], p["g1"], p["b1"], with_bf16=True)
        z = in_proj(x1b, p["w_main"])
        oa, la = [], []
        o, l = mixer_a_group(z.reshape(batch, 1, seq, MAIN_W), 1, AQ_BLK, AK_BLK, AV_BLK)
        oa.append(o.reshape(batch * seq, HEAD_BLOCK))
        la.append(l.reshape(batch * seq, HEAD_BLOCK))
        for g in range(1, A_GROUPS):
            dilation = A_PATTERNS[g][1]
            zg = in_proj(_by_residue(x1b, batch, seq, dilation), p["w_groups"][g - 1])
            o, l = mixer_a_group(zg.reshape(batch, dilation, seq // dilation, A_QKV_W), dilation, 0, 1, 2)
            oa.append(_by_token(o, batch, seq))
            la.append(_by_token(l, batch, seq))
        yb = mixer_b(z, batch, seq, p["logit_fwd"], p["logit_bwd"])
        yc = mixer_c(z, batch, seq, p["rpb"])
        x2 = merge_out_ln(x1, oa, la, yb, yc, z, p["wa"], p["wb"], p["wc"], p["wo"], p["g2"], p["b2"])
        (x,) = ffn_ln(x2, p["wg2"], p["wu2"], p["wd2"], p["g3"], p["b3"], with_bf16=False)
    return x.reshape(batch, seq, D_MODEL)


def kernel(x_prompt, x_sample, ffn1_w_gate, ffn1_w_up, ffn1_w_down, ln1_g, ln1_b, w_in, ret_logit_fwd, ret_logit_bwd, na_rpb, w_branch_a, w_branch_b, w_branch_c, w_out, ln2_g, ln2_b, ffn2_w_gate, ffn2_w_up, ffn2_w_down, ln3_g, ln3_b):
    def vec(v):
        return v.astype(F32).reshape(1, D_MODEL)

    layers = []
    for i in range(DEPTH):
        w_main, w_groups = _split_in_weights(w_in[i])
        layers.append(dict(
            wg1=ffn1_w_gate[i].astype(BF16), wu1=ffn1_w_up[i].astype(BF16), wd1=ffn1_w_down[i].astype(BF16),
            g1=vec(ln1_g[i]), b1=vec(ln1_b[i]),
            w_main=w_main, w_groups=w_groups,
            logit_fwd=ret_logit_fwd[i], logit_bwd=ret_logit_bwd[i], rpb=na_rpb[i],
            wa=w_branch_a[i].astype(BF16), wb=w_branch_b[i].astype(BF16), wc=w_branch_c[i].astype(BF16),
            wo=w_out[i].astype(BF16), g2=vec(ln2_g[i]), b2=vec(ln2_b[i]),
            wg2=ffn2_w_gate[i].astype(BF16), wu2=ffn2_w_up[i].astype(BF16), wd2=ffn2_w_down[i].astype(BF16),
            g3=vec(ln3_g[i]), b3=vec(ln3_b[i])))
    return (_trunk(x_prompt, layers), _trunk(x_sample, layers))
```

```python
import functools

import jax
import jax.numpy as jnp
from jax import lax
from jax.experimental import pallas as pl
from jax.experimental.pallas import tpu as pltpu

F32 = jnp.float32
BF16 = jnp.bfloat16

D_MODEL = 1024
DEPTH = 2
D_FF = 2816
LN_EPS = 1e-5
GN_EPS = 1e-5
ALPHA = (2 * DEPTH) ** 0.25

A_PATTERNS = ((128, 1), (512, 4), (2048, 16))
A_GROUPS = len(A_PATTERNS)
A_HEADS = 8
A_HD = 64
A_HALF = 64
B_HEADS = 4
B_DK = 64
B_DV = 128
B_CHUNK = 128
B_BLOCK = 4
C_HEADS = 8
C_HD = 64
GRID_W = 64
C_KH = 8
C_KW = 16
C_QR = 4
D_IN = 10752
NEG = -1e30
LOG2E = 1.4426950408889634

GATE_W = 3 * D_MODEL
HEAD_BLOCK = 512
A_QKV_W = 3 * HEAD_BLOCK
MAIN_W = D_IN - (A_GROUPS - 1) * A_QKV_W
AQ_BLK, AK_BLK, AV_BLK = 6, 7, 8
BQ_BLK256, BK_BLK256 = 18, 19
BV_BLK, BG_BLK = 10, 11
CQ_BLK, CK_BLK, CV_BLK = 12, 13, 14

VMEM_LIMIT = 56 * 1024 * 1024
FF_CHUNKS = ((0, 512), (512, 1024), (1024, 1536), (1536, 2048), (2048, 2560), (2560, 2816))


def _params(sem):
    return pltpu.CompilerParams(dimension_semantics=sem, vmem_limit_bytes=VMEM_LIMIT)


def _const_spec(shape):
    zeros = (0,) * len(shape)
    return pl.BlockSpec(shape, lambda *_: zeros)


def _layer_norm(r, g, b):
    mu = jnp.mean(r, axis=-1, keepdims=True)
    c = r - mu
    var = jnp.mean(c * c, axis=-1, keepdims=True)
    return c * lax.rsqrt(var + LN_EPS) * g + b


def _ffn_ln_kernel(x_ref, wg_ref, wu_ref, wd_ref, g_ref, b_ref, o_ref, *maybe_ob_ref):
    x = x_ref[...]
    xb = x.astype(BF16)
    acc = None
    for c0, c1 in FF_CHUNKS:
        gate = jnp.dot(xb, wg_ref[:, c0:c1], preferred_element_type=F32)
        up = jnp.dot(xb, wu_ref[:, c0:c1], preferred_element_type=F32)
        h = (gate * jax.nn.sigmoid(gate) * up).astype(BF16)
        part = jnp.dot(h, wd_ref[c0:c1, :], preferred_element_type=F32)
        acc = part if acc is None else acc + part
    y = _layer_norm(ALPHA * x + 0.5 * acc, g_ref[...], b_ref[...])
    o_ref[...] = y
    if maybe_ob_ref:
        maybe_ob_ref[0][...] = y.astype(BF16)


def ffn_ln(x, wg, wu, wd, g, b, with_bf16):
    t = x.shape[0]
    tm = 512
    row = pl.BlockSpec((tm, D_MODEL), lambda i: (i, 0))
    out_shape = [jax.ShapeDtypeStruct((t, D_MODEL), F32)]
    out_specs = [row]
    if with_bf16:
        out_shape.append(jax.ShapeDtypeStruct((t, D_MODEL), BF16))
        out_specs.append(row)
    return pl.pallas_call(
        _ffn_ln_kernel,
        grid=(t // tm,),
        in_specs=[row, _const_spec((D_MODEL, D_FF)), _const_spec((D_MODEL, D_FF)), _const_spec((D_FF, D_MODEL)),
                  _const_spec((1, D_MODEL)), _const_spec((1, D_MODEL))],
        out_specs=out_specs,
        out_shape=out_shape,
        compiler_params=_params(("arbitrary",)),
        name="ffn_ln",
    )(x, wg, wu, wd, g, b)


def _in_proj_kernel(x_ref, w_ref, z_ref):
    z_ref[...] = jnp.dot(x_ref[...], w_ref[...], preferred_element_type=F32).astype(BF16)


def in_proj(xb, w_in):
    t = xb.shape[0]
    width = w_in.shape[1]
    tm, tn = 1024, A_QKV_W
    return pl.pallas_call(
        _in_proj_kernel,
        grid=(t // tm, width // tn),
        in_specs=[pl.BlockSpec((tm, D_MODEL), lambda i, j: (i, 0)), pl.BlockSpec((D_MODEL, tn), lambda i, j: (0, j))],
        out_specs=pl.BlockSpec((tm, tn), lambda i, j: (i, j)),
        out_shape=jax.ShapeDtypeStruct((t, width), BF16),
        compiler_params=_params(("arbitrary", "arbitrary")),
        name="in_proj",
    )(xb, w_in)


def _pair_weighted_sum(probs, v_pair, left):
    rhs = []
    for hh in range(2):
        sel = left if hh == 0 else jnp.logical_not(left)
        ones = jnp.broadcast_to(jnp.where(sel, 1.0, 0.0).astype(BF16), v_pair.shape)
        rhs.append(jnp.concatenate([jnp.where(sel, v_pair, jnp.zeros_like(v_pair)), ones], axis=1))
    out = jnp.dot(jnp.concatenate(probs, axis=1), jnp.concatenate(rhs, axis=0), preferred_element_type=F32)
    width = v_pair.shape[1]
    return out[:, :width], out[:, width:]


A_QB = 128
A_KB = A_QB + 2 * A_HALF


def _mixer_a_kernel(q_ref, kp_ref, kc_ref, kn_ref, vp_ref, vc_ref, vn_ref, bias_ref, o_ref, l_ref, kf, vf, *,
                    tl, seq_len):
    i = pl.program_id(2)
    kf[0:A_HALF, :] = kp_ref[0, 0]
    kf[A_HALF:A_HALF + tl, :] = kc_ref[0, 0]
    kf[A_HALF + tl:, :] = kn_ref[0, 0]
    vf[0:A_HALF, :] = vp_ref[0, 0]
    vf[A_HALF:A_HALF + tl, :] = vc_ref[0, 0]
    vf[A_HALF + tl:, :] = vn_ref[0, 0]

    left = lax.broadcasted_iota(jnp.int32, (1, 2 * A_HD), 1) < A_HD

    def block(j, carry):
        q0 = pl.multiple_of(j * A_QB, A_QB)
        start = i * tl + q0
        variant = (start == 0).astype(jnp.int32) + 2 * (start + A_QB == seq_len).astype(jnp.int32)
        for hp in range(A_HEADS // 2):
            cols = slice(hp * 2 * A_HD, (hp + 1) * 2 * A_HD)
            qp = q_ref[0, 0, pl.ds(q0, A_QB), cols]
            kp = kf[pl.ds(q0, A_KB), cols]
            probs, tops = [], []
            for hh in range(2):
                sel = left if hh == 0 else jnp.logical_not(left)
                qm = jnp.where(sel, qp, jnp.zeros_like(qp))
                s = lax.dot_general(qm, kp, (((1,), (1,)), ((), ())), preferred_element_type=F32)
                s = s + bias_ref[variant, hp * 2 + hh]
                m = jnp.max(s, axis=-1, keepdims=True)
                probs.append(jnp.exp2(s - m).astype(BF16))
                tops.append(m)
            num, den = _pair_weighted_sum(probs, vf[pl.ds(q0, A_KB), cols], left)
            o_ref[0, 0, pl.ds(q0, A_QB), cols] = (num / den).astype(BF16)
            l_ref[0, 0, pl.ds(q0, A_QB), cols] = jnp.where(left, tops[0], tops[1]) + jnp.log2(den)
        return carry

    lax.fori_loop(0, tl // A_QB, block, 0, unroll=True)


def _alibi_bias(dilation):
    slopes = 2.0 ** (-8.0 * jnp.arange(1, A_HEADS + 1, dtype=F32) / A_HEADS)
    key = jnp.arange(A_KB)[None, :] - A_HALF
    rel = key - jnp.arange(A_QB)[:, None]
    dist = (jnp.abs(rel) * dilation).astype(F32)
    bias = -slopes[:, None, None] * dist[None] * LOG2E
    in_window = jnp.abs(rel) <= A_HALF
    variants = []
    for v in range(4):
        ok = in_window
        if v & 1:
            ok = ok & (key >= 0)
        if v & 2:
            ok = ok & (key < A_QB)
        variants.append(jnp.where(ok[None], bias, NEG))
    return jnp.stack(variants, 0)


def mixer_a_group(zg, dilation, q_blk, k_blk, v_blk):
    batch, _, strided_len, _ = zg.shape
    tl = min(512, strided_len)
    halo_per_tile = tl // A_HALF
    n_halo = strided_len // A_HALF

    def cur(c):
        return lambda b, r, i: (b, r, i, c)

    def before(c):
        return lambda b, r, i: (b, r, jnp.maximum(i * halo_per_tile - 1, 0), c)

    def after(c):
        return lambda b, r, i: (b, r, jnp.minimum((i + 1) * halo_per_tile, n_halo - 1), c)

    tile = (1, 1, tl, HEAD_BLOCK)
    halo = (1, 1, A_HALF, HEAD_BLOCK)
    out_spec = pl.BlockSpec(tile, cur(0))
    out_dims = (batch, dilation, strided_len, HEAD_BLOCK)
    return pl.pallas_call(
        functools.partial(_mixer_a_kernel, tl=tl, seq_len=strided_len),
        grid=(batch, dilation, strided_len // tl),
        in_specs=[pl.BlockSpec(tile, cur(q_blk)),
                  pl.BlockSpec(halo, before(k_blk)), pl.BlockSpec(tile, cur(k_blk)), pl.BlockSpec(halo, after(k_blk)),
                  pl.BlockSpec(halo, before(v_blk)), pl.BlockSpec(tile, cur(v_blk)), pl.BlockSpec(halo, after(v_blk)),
                  _const_spec((4, A_HEADS, A_QB, A_KB))],
        out_specs=[out_spec, out_spec],
        out_shape=[jax.ShapeDtypeStruct(out_dims, BF16), jax.ShapeDtypeStruct(out_dims, F32)],
        scratch_shapes=[pltpu.VMEM((tl + 2 * A_HALF, HEAD_BLOCK), BF16), pltpu.VMEM((tl + 2 * A_HALF, HEAD_BLOCK), BF16)],
        compiler_params=_params(("arbitrary", "arbitrary", "arbitrary")),
        name=f"mixer_a_d{dilation}",
    )(zg, zg, zg, zg, zg, zg, zg, _alibi_bias(dilation))


def _by_residue(x, batch, seq, dilation):
    w = x.shape[-1]
    return x.reshape(batch, seq // dilation, dilation, w).transpose(0, 2, 1, 3).reshape(batch * seq, w)


def _by_token(x, batch, seq):
    return x.transpose(0, 2, 1, 3).reshape(batch * seq, x.shape[-1])


def _mixer_b_kernel(q_ref, k_ref, v_ref, g_ref, dmat_ref, qdf_ref, qdb_ref, kdf_ref, kdb_ref, cdf_ref, cdb_ref,
                    o_ref, fwd_state, bwd_state, bwd_store, *, n_blocks):
    phase = pl.program_id(1)
    n = pl.program_id(2)
    qk_w = B_HEADS * B_DK
    head_of_lane = lax.broadcasted_iota(jnp.int32, (1, qk_w), 1) // B_DK

    def rows(c):
        return slice(c * B_CHUNK, (c + 1) * B_CHUNK)

    def stacked_heads(t):
        lane_head = jnp.concatenate([head_of_lane] * (t.shape[1] // qk_w), axis=1)
        return jnp.concatenate([jnp.where(lane_head == h, t, jnp.zeros_like(t)) for h in range(B_HEADS)], axis=0)

    def kv_outer(k_decayed, v):
        full = lax.dot_general(k_decayed, v, (((0,), (0,)), ((), ())), preferred_element_type=F32)
        return jnp.concatenate([full[h * B_DK:(h + 1) * B_DK, h * B_DV:(h + 1) * B_DV] for h in range(B_HEADS)], axis=0)

    def scaled_k(c):
        return k_ref[0, rows(c), :] * (B_DK ** -0.5)

    @pl.when(phase == 0)
    def _():
        @pl.when(n == 0)
        def _():
            bwd_state[...] = jnp.zeros_like(bwd_state)

        blk = n_blocks - 1 - n
        state = bwd_state[...]
        for c in reversed(range(B_BLOCK)):
            bwd_store[blk * B_BLOCK + c] = state.astype(BF16)
            k_dec = (scaled_k(c).astype(F32) * kdb_ref[...]).astype(BF16)
            state = cdb_ref[...] * state + kv_outer(k_dec, v_ref[0, rows(c), :])
        bwd_state[...] = state

    @pl.when(phase == 1)
    def _():
        @pl.when(n == 0)
        def _():
            fwd_state[...] = jnp.zeros_like(fwd_state)

        state = fwd_state[...]
        for c in range(B_BLOCK):
            q = q_ref[0, rows(c), :]
            k = scaled_k(c)
            v = v_ref[0, rows(c), :]
            q32 = q.astype(F32)
            q_dec = jnp.concatenate([(q32 * qdf_ref[...]).astype(BF16), (q32 * qdb_ref[...]).astype(BF16)], axis=1)
            k_dec = (k.astype(F32) * kdf_ref[...]).astype(BF16)
            states = jnp.concatenate([state.astype(BF16), bwd_store[n * B_BLOCK + c]], axis=0)
            s_all = lax.dot_general(stacked_heads(q), k, (((1,), (1,)), ((), ())), preferred_element_type=F32)
            cross_all = jnp.dot(stacked_heads(q_dec), states, preferred_element_type=F32)
            for h in range(B_HEADS):
                vh = v[:, h * B_DV:(h + 1) * B_DV]
                inner = jnp.dot((s_all[rows(h), :] * dmat_ref[h]).astype(BF16), vh, preferred_element_type=F32)
                y = inner + cross_all[rows(h), :]
                mu = jnp.mean(y, axis=-1, keepdims=True)
                cen = y - mu
                var = jnp.mean(cen * cen, axis=-1, keepdims=True)
                yn = cen * lax.rsqrt(var + GN_EPS)
                gate = g_ref[0, rows(c), h * B_DV:(h + 1) * B_DV].astype(F32)
                o_ref[0, rows(c), h * B_DV:(h + 1) * B_DV] = (gate * jax.nn.sigmoid(gate) * yn).astype(BF16)
            state = cdf_ref[...] * state + kv_outer(k_dec, v)
        fwd_state[...] = state


def _retention_tables(logit_fwd, logit_bwd):
    lg_f = jax.nn.log_sigmoid(logit_fwd.astype(F32))
    lg_b = jax.nn.log_sigmoid(logit_bwd.astype(F32))
    idx = jnp.arange(B_CHUNK, dtype=F32)
    diff = idx[:, None] - idx[None, :]
    causal = diff >= 0
    dmat = jnp.where(causal[None],
                     jnp.exp(lg_f[:, None, None] * jnp.where(causal, diff, 0.0)[None]),
                     jnp.exp(lg_b[:, None, None] * jnp.where(causal, 0.0, -diff)[None]))

    def per_lane(lg, power):
        return jnp.repeat(jnp.exp(lg[None, :] * power[:, None]), B_DK, axis=1)

    def per_row(lg):
        return jnp.broadcast_to(jnp.repeat(jnp.exp(lg * B_CHUNK), B_DK)[:, None], (B_HEADS * B_DK, B_DV))

    return (dmat, per_lane(lg_f, idx + 1), per_lane(lg_b, B_CHUNK - idx), per_lane(lg_f, B_CHUNK - 1 - idx),
            per_lane(lg_b, idx), per_row(lg_f), per_row(lg_b))


def mixer_b(z, batch, seq, logit_fwd, logit_bwd):
    n_chunks = seq // B_CHUNK
    n_blocks = n_chunks // B_BLOCK
    block_rows = B_BLOCK * B_CHUNK
    zv = z.reshape(batch, seq, MAIN_W)
    qk_w = B_HEADS * B_DK
    v_w = B_HEADS * B_DV

    def scan_block(ph, n):
        return (1 - ph) * (n_blocks - 1 - n) + ph * n

    tables = _retention_tables(logit_fwd, logit_bwd)
    in_specs = [pl.BlockSpec((1, block_rows, qk_w), lambda b, ph, n: (b, ph * n, BQ_BLK256)),
                pl.BlockSpec((1, block_rows, qk_w), lambda b, ph, n: (b, scan_block(ph, n), BK_BLK256)),
                pl.BlockSpec((1, block_rows, v_w), lambda b, ph, n: (b, scan_block(ph, n), BV_BLK)),
                pl.BlockSpec((1, block_rows, v_w), lambda b, ph, n: (b, ph * n, BG_BLK))]
    in_specs += [_const_spec(t.shape) for t in tables]
    out = pl.pallas_call(
        functools.partial(_mixer_b_kernel, n_blocks=n_blocks),
        grid=(batch, 2, n_blocks),
        in_specs=in_specs,
        out_specs=pl.BlockSpec((1, block_rows, v_w), lambda b, ph, n: (b, ph * n, 0)),
        out_shape=jax.ShapeDtypeStruct((batch, seq, v_w), BF16),
        scratch_shapes=[pltpu.VMEM((qk_w, B_DV), F32), pltpu.VMEM((qk_w, B_DV), F32),
                        pltpu.VMEM((n_chunks, qk_w, B_DV), BF16)],
        compiler_params=_params(("arbitrary", "arbitrary", "arbitrary")),
        name="mixer_b",
    )(zv, zv, zv, zv, *tables)
    return out.reshape(batch * seq, v_w)


C_TQ = C_QR * GRID_W
C_KROWS = 3 * C_QR
C_TK = C_KROWS * GRID_W
C_PAIRS = C_KROWS // 2
C_NTAB = 2 * C_KH - 2


def _mixer_c_kernel(q_ref, kp_ref, kc_ref, kn_ref, vp_ref, vc_ref, vn_ref, tab_ref, o_ref, kf, vf, *, rows):
    blk = pl.program_id(1)
    kf[0:C_TQ, :] = kp_ref[0]
    kf[C_TQ:2 * C_TQ, :] = kc_ref[0]
    kf[2 * C_TQ:, :] = kn_ref[0]
    vf[0:C_TQ, :] = vp_ref[0]
    vf[C_TQ:2 * C_TQ, :] = vc_ref[0]
    vf[2 * C_TQ:, :] = vn_ref[0]

    left = lax.broadcasted_iota(jnp.int32, (1, 2 * C_HD), 1) < C_HD
    key_row = blk * C_QR - C_QR + lax.broadcasted_iota(jnp.int32, (1, C_TK), 1) // GRID_W
    pens = []
    for a in range(C_QR):
        row_start = jnp.clip(blk * C_QR + a - C_KH // 2, 0, rows - C_KH)
        pens.append(jnp.where((key_row >= row_start) & (key_row < row_start + C_KH), 0.0, NEG).astype(F32))

    for hp in range(C_HEADS // 2):
        cols = slice(hp * 2 * C_HD, (hp + 1) * 2 * C_HD)
        qp = q_ref[0, :, cols]
        kp = kf[:, cols]
        probs = []
        for hh in range(2):
            h = hp * 2 + hh
            sel = left if hh == 0 else jnp.logical_not(left)
            qm = jnp.where(sel, qp, jnp.zeros_like(qp))
            s = lax.dot_general(qm, kp, (((1,), (1,)), ((), ())), preferred_element_type=F32)
            rows_p = []
            for a in range(C_QR):
                bias = jnp.concatenate([tab_ref[h, 2 * t - C_QR - a + C_KH - 1] for t in range(C_PAIRS)], axis=1)
                sa = s[a * GRID_W:(a + 1) * GRID_W, :] + bias + pens[a]
                m = jnp.max(sa, axis=-1, keepdims=True)
                rows_p.append(jnp.exp2(sa - m).astype(BF16))
            probs.append(jnp.concatenate(rows_p, axis=0))
        num, den = _pair_weighted_sum(probs, vf[:, cols], left)
        o_ref[0, :, cols] = (num / den).astype(BF16)


def _neighbourhood_bias(rpb):
    qc = jnp.arange(GRID_W)[:, None]
    kc = jnp.arange(GRID_W)[None, :]
    col_start = jnp.clip(qc - C_KW // 2, 0, GRID_W - C_KW)
    col_ok = (kc >= col_start) & (kc < col_start + C_KW)
    onehot = ((kc - qc + (C_KW - 1))[:, :, None] == jnp.arange(2 * C_KW - 1)[None, None, :]).astype(F32)
    band = jnp.einsum("hrd,qkd->hrqk", rpb.astype(F32), onehot, precision=lax.Precision.HIGHEST)
    band = jnp.where(col_ok[None, None], band * LOG2E, NEG)
    return jnp.concatenate([band[:, :C_NTAB], band[:, 1:C_NTAB + 1]], axis=-1)


def mixer_c(z, batch, seq, rpb):
    rows = seq // GRID_W
    n_blk = rows // C_QR
    zv = z.reshape(batch, seq, MAIN_W)
    tile = (1, C_TQ, HEAD_BLOCK)
    tab = _neighbourhood_bias(rpb)

    def above(c):
        return lambda b, i: (b, jnp.maximum(i - 1, 0), c)

    def here(c):
        return lambda b, i: (b, i, c)

    def below(c):
        return lambda b, i: (b, jnp.minimum(i + 1, n_blk - 1), c)

    out = pl.pallas_call(
        functools.partial(_mixer_c_kernel, rows=rows),
        grid=(batch, n_blk),
        in_specs=[pl.BlockSpec(tile, here(CQ_BLK)),
                  pl.BlockSpec(tile, above(CK_BLK)), pl.BlockSpec(tile, here(CK_BLK)), pl.BlockSpec(tile, below(CK_BLK)),
                  pl.BlockSpec(tile, above(CV_BLK)), pl.BlockSpec(tile, here(CV_BLK)), pl.BlockSpec(tile, below(CV_BLK)),
                  _const_spec(tab.shape)],
        out_specs=pl.BlockSpec(tile, lambda b, i: (b, i, 0)),
        out_shape=jax.ShapeDtypeStruct((batch, seq, HEAD_BLOCK), BF16),
        scratch_shapes=[pltpu.VMEM((C_TK, HEAD_BLOCK), BF16), pltpu.VMEM((C_TK, HEAD_BLOCK), BF16)],
        compiler_params=_params(("arbitrary", "arbitrary")),
        name="mixer_c",
    )(zv, zv, zv, zv, zv, zv, zv, tab)
    return out.reshape(batch * seq, HEAD_BLOCK)


def _merge_kernel(x_ref, oa0_ref, oa1_ref, oa2_ref, la0_ref, la1_ref, la2_ref, yb_ref, yc_ref, gates_ref,
                  wa_ref, wb_ref, wc_ref, wo_ref, g_ref, b_ref, o_ref):
    lses = [la0_ref[...], la1_ref[...], la2_ref[...]]
    top = jnp.maximum(jnp.maximum(lses[0], lses[1]), lses[2])
    num, den = None, None
    for o_g_ref, lse in zip((oa0_ref, oa1_ref, oa2_ref), lses):
        w = jnp.exp2(lse - top)
        num = w * o_g_ref[...].astype(F32) if num is None else num + w * o_g_ref[...].astype(F32)
        den = w if den is None else den + w
    ya = (num / den).astype(BF16)

    merged = None
    for br, (y, w_ref) in enumerate(((ya, wa_ref), (yb_ref[...], wb_ref), (yc_ref[...], wc_ref))):
        proj = jnp.dot(y, w_ref[...], preferred_element_type=F32)
        gate = jax.nn.sigmoid(gates_ref[:, br * D_MODEL:(br + 1) * D_MODEL].astype(F32))
        merged = gate * proj if merged is None else merged + gate * proj
    out = jnp.dot(merged.astype(BF16), wo_ref[...], preferred_element_type=F32)
    o_ref[...] = _layer_norm(ALPHA * x_ref[...] + out, g_ref[...], b_ref[...])


def merge_out_ln(x, oa, la, yb, yc, z, wa, wb, wc, wo, g, b):
    t = x.shape[0]
    tm = 512
    row = pl.BlockSpec((tm, D_MODEL), lambda i: (i, 0))
    br = pl.BlockSpec((tm, HEAD_BLOCK), lambda i: (i, 0))
    w_br = _const_spec((HEAD_BLOCK, D_MODEL))
    return pl.pallas_call(
        _merge_kernel,
        grid=(t // tm,),
        in_specs=[row] + [br] * 8 + [pl.BlockSpec((tm, GATE_W), lambda i: (i, 0)), w_br, w_br, w_br,
                                      _const_spec((D_MODEL, D_MODEL)), _const_spec((1, D_MODEL)), _const_spec((1, D_MODEL))],
        out_specs=row,
        out_shape=jax.ShapeDtypeStruct((t, D_MODEL), F32),
        compiler_params=_params(("arbitrary",)),
        name="merge_out_ln",
    )(x, *oa, *la, yb, yc, z, wa, wb, wc, wo, g, b)


def _split_in_weights(w_in):
    a_w = A_GROUPS * HEAD_BLOCK
    aq, ak, av = w_in[:, :a_w] * (A_HD ** -0.5 * LOG2E), w_in[:, a_w:2 * a_w], w_in[:, 2 * a_w:3 * a_w]
    b_w = 2 * B_HEADS * (B_DK + B_DV)
    rest_b = w_in[:, 3 * a_w:3 * a_w + b_w]
    cq = w_in[:, 3 * a_w + b_w:3 * a_w + b_w + HEAD_BLOCK] * (C_HD ** -0.5 * LOG2E)
    rest = jnp.concatenate([rest_b, cq, w_in[:, 3 * a_w + b_w + HEAD_BLOCK:D_IN - GATE_W]], axis=1)

    def group(g):
        cols = slice(g * HEAD_BLOCK, (g + 1) * HEAD_BLOCK)
        return jnp.concatenate([aq[:, cols], ak[:, cols], av[:, cols]], axis=1)

    main = jnp.concatenate([w_in[:, D_IN - GATE_W:], group(0), rest], axis=1)
    return main.astype(BF16), [group(g).astype(BF16) for g in range(1, A_GROUPS)]


def _trunk(x, layers):
    batch, seq, _ = x.shape
    x = x.reshape(batch * seq, D_MODEL)
    for p in layers:
        x1, x1b = ffn_ln(x, p["wg1"], p["wu1"], p["wd1"], p["g1"], p["b1"], with_bf16=True)
        z = in_proj(x1b, p["w_main"])
        oa, la = [], []
        o, l = mixer_a_group(z.reshape(batch, 1, seq, MAIN_W), 1, AQ_BLK, AK_BLK, AV_BLK)
        oa.append(o.reshape(batch * seq, HEAD_BLOCK))
        la.append(l.reshape(batch * seq, HEAD_BLOCK))
        for g in range(1, A_GROUPS):
            dilation = A_PATTERNS[g][1]
            zg = in_proj(_by_residue(x1b, batch, seq, dilation), p["w_groups"][g - 1])
            o, l = mixer_a_group(zg.reshape(batch, dilation, seq // dilation, A_QKV_W), dilation, 0, 1, 2)
            oa.append(_by_token(o, batch, seq))
            la.append(_by_token(l, batch, seq))
        yb = mixer_b(z, batch, seq, p["logit_fwd"], p["logit_bwd"])
        yc = mixer_c(z, batch, seq, p["rpb"])
        x2 = merge_out_ln(x1, oa, la, yb, yc, z, p["wa"], p["wb"], p["wc"], p["wo"], p["g2"], p["b2"])
        (x,) = ffn_ln(x2, p["wg2"], p["wu2"], p["wd2"], p["g3"], p["b3"], with_bf16=False)
    return x.reshape(batch, seq, D_MODEL)


def kernel(x_prompt, x_sample, ffn1_w_gate, ffn1_w_up, ffn1_w_down, ln1_g, ln1_b, w_in, ret_logit_fwd, ret_logit_bwd, na_rpb, w_branch_a, w_branch_b, w_branch_c, w_out, ln2_g, ln2_b, ffn2_w_gate, ffn2_w_up, ffn2_w_down, ln3_g, ln3_b):
    def vec(v):
        return v.astype(F32).reshape(1, D_MODEL)

    layers = []
    for i in range(DEPTH):
        w_main, w_groups = _split_in_weights(w_in[i])
        layers.append(dict(
            wg1=ffn1_w_gate[i].astype(BF16), wu1=ffn1_w_up[i].astype(BF16), wd1=ffn1_w_down[i].astype(BF16),
            g1=vec(ln1_g[i]), b1=vec(ln1_b[i]),
            w_main=w_main, w_groups=w_groups,
            logit_fwd=ret_logit_fwd[i], logit_bwd=ret_logit_bwd[i], rpb=na_rpb[i],
            wa=w_branch_a[i].astype(BF16), wb=w_branch_b[i].astype(BF16), wc=w_branch_c[i].astype(BF16),
            wo=w_out[i].astype(BF16), g2=vec(ln2_g[i]), b2=vec(ln2_b[i]),
            wg2=ffn2_w_gate[i].astype(BF16), wu2=ffn2_w_up[i].astype(BF16), wd2=ffn2_w_down[i].astype(BF16),
            g3=vec(ln3_g[i]), b3=vec(ln3_b[i])))
    return (_trunk(x_prompt, layers), _trunk(x_sample, layers))
```

```python
import functools

import jax
import jax.numpy as jnp
from jax import lax
from jax.experimental import pallas as pl
from jax.experimental.pallas import tpu as pltpu

F32 = jnp.float32
BF16 = jnp.bfloat16

D_MODEL = 1024
DEPTH = 2
D_FF = 2816
LN_EPS = 1e-5
GN_EPS = 1e-5
ALPHA = (2 * DEPTH) ** 0.25

A_PATTERNS = ((128, 1), (512, 4), (2048, 16))
A_GROUPS = len(A_PATTERNS)
A_HEADS = 8
A_HD = 64
A_HALF = 64
LSE_LANES = A_HD // (A_HEADS // 2)
B_HEADS = 4
B_DK = 64
B_DV = 128
B_CHUNK = 128
B_BLOCK = 4
C_HEADS = 8
C_HD = 64
GRID_W = 64
C_KH = 8
C_KW = 16
C_QR = 4
D_IN = 10752
NEG = -1e30
LOG2E = 1.4426950408889634

GATE_W = 3 * D_MODEL
HEAD_BLOCK = 512
A_QKV_W = 3 * HEAD_BLOCK
MAIN_W = D_IN - (A_GROUPS - 1) * A_QKV_W
AQ_BLK, AK_BLK, AV_BLK = 6, 7, 8
BQ_BLK256, BK_BLK256 = 18, 19
BV_BLK, BG_BLK = 10, 11
CQ_BLK, CK_BLK, CV_BLK = 12, 13, 14

VMEM_LIMIT = 56 * 1024 * 1024
FF_CHUNKS = ((0, 512), (512, 1024), (1024, 1536), (1536, 2048), (2048, 2560), (2560, 2816))


def _params(sem):
    return pltpu.CompilerParams(dimension_semantics=sem, vmem_limit_bytes=VMEM_LIMIT)


def _const_spec(shape):
    zeros = (0,) * len(shape)
    return pl.BlockSpec(shape, lambda *_: zeros)


def _layer_norm(r, g, b):
    mu = jnp.mean(r, axis=-1, keepdims=True)
    c = r - mu
    var = jnp.mean(c * c, axis=-1, keepdims=True)
    return c * lax.rsqrt(var + LN_EPS) * g + b


def _ffn_ln_kernel(x_ref, wg_ref, wu_ref, wd_ref, g_ref, b_ref, o_ref, *maybe_ob_ref):
    x = x_ref[...]
    xb = x.astype(BF16)
    acc = None
    for c0, c1 in FF_CHUNKS:
        gate = jnp.dot(xb, wg_ref[:, c0:c1], preferred_element_type=F32)
        up = jnp.dot(xb, wu_ref[:, c0:c1], preferred_element_type=F32)
        h = (gate * jax.nn.sigmoid(gate) * up).astype(BF16)
        part = jnp.dot(h, wd_ref[c0:c1, :], preferred_element_type=F32)
        acc = part if acc is None else acc + part
    y = _layer_norm(ALPHA * x + 0.5 * acc, g_ref[...], b_ref[...])
    o_ref[...] = y
    if maybe_ob_ref:
        maybe_ob_ref[0][...] = y.astype(BF16)


def ffn_ln(x, wg, wu, wd, g, b, with_bf16):
    t = x.shape[0]
    tm = 512
    row = pl.BlockSpec((tm, D_MODEL), lambda i: (i, 0))
    out_shape = [jax.ShapeDtypeStruct((t, D_MODEL), F32)]
    out_specs = [row]
    if with_bf16:
        out_shape.append(jax.ShapeDtypeStruct((t, D_MODEL), BF16))
        out_specs.append(row)
    return pl.pallas_call(
        _ffn_ln_kernel,
        grid=(t // tm,),
        in_specs=[row, _const_spec((D_MODEL, D_FF)), _const_spec((D_MODEL, D_FF)), _const_spec((D_FF, D_MODEL)),
                  _const_spec((1, D_MODEL)), _const_spec((1, D_MODEL))],
        out_specs=out_specs,
        out_shape=out_shape,
        compiler_params=_params(("arbitrary",)),
        name="ffn_ln",
    )(x, wg, wu, wd, g, b)


def _in_proj_kernel(x_ref, w_ref, z_ref):
    z_ref[...] = jnp.dot(x_ref[...], w_ref[...], preferred_element_type=F32).astype(BF16)


def in_proj(xb, w_in):
    t = xb.shape[0]
    width = w_in.shape[1]
    tm, tn = 2048, A_QKV_W
    return pl.pallas_call(
        _in_proj_kernel,
        grid=(t // tm, width // tn),
        in_specs=[pl.BlockSpec((tm, D_MODEL), lambda i, j: (i, 0)), pl.BlockSpec((D_MODEL, tn), lambda i, j: (0, j))],
        out_specs=pl.BlockSpec((tm, tn), lambda i, j: (i, j)),
        out_shape=jax.ShapeDtypeStruct((t, width), BF16),
        compiler_params=_params(("arbitrary", "arbitrary")),
        name="in_proj",
    )(xb, w_in)


def _pair_weighted_sum(probs, v_pair, left):
    rhs = []
    for hh in range(2):
        sel = left if hh == 0 else jnp.logical_not(left)
        ones = jnp.broadcast_to(jnp.where(sel, 1.0, 0.0).astype(BF16), v_pair.shape)
        rhs.append(jnp.concatenate([jnp.where(sel, v_pair, jnp.zeros_like(v_pair)), ones], axis=1))
    out = jnp.dot(jnp.concatenate(probs, axis=1), jnp.concatenate(rhs, axis=0), preferred_element_type=F32)
    width = v_pair.shape[1]
    return out[:, :width], out[:, width:]


A_QB = 128
A_KB = A_QB + 2 * A_HALF


def _mixer_a_kernel(q_ref, kp_ref, kc_ref, kn_ref, vp_ref, vc_ref, vn_ref, bias_ref, o_ref, l_ref, kf, vf, *,
                    tl, seq_len):
    i = pl.program_id(2)
    kf[0:A_HALF, :] = kp_ref[0, 0]
    kf[A_HALF:A_HALF + tl, :] = kc_ref[0, 0]
    kf[A_HALF + tl:, :] = kn_ref[0, 0]
    vf[0:A_HALF, :] = vp_ref[0, 0]
    vf[A_HALF:A_HALF + tl, :] = vc_ref[0, 0]
    vf[A_HALF + tl:, :] = vn_ref[0, 0]

    lane = lax.broadcasted_iota(jnp.int32, (1, 2 * A_HD), 1)
    left = lane < A_HD
    lse_slot = (lane % A_HD) // LSE_LANES

    def block(j, carry):
        q0 = pl.multiple_of(j * A_QB, A_QB)
        start = i * tl + q0
        variant = (start == 0).astype(jnp.int32) + 2 * (start + A_QB == seq_len).astype(jnp.int32)
        lse_all = None
        for hp in range(A_HEADS // 2):
            cols = slice(hp * 2 * A_HD, (hp + 1) * 2 * A_HD)
            qp = q_ref[0, 0, pl.ds(q0, A_QB), cols]
            kp = kf[pl.ds(q0, A_KB), cols]
            probs, tops = [], []
            for hh in range(2):
                sel = left if hh == 0 else jnp.logical_not(left)
                qm = jnp.where(sel, qp, jnp.zeros_like(qp))
                s = lax.dot_general(qm, kp, (((1,), (1,)), ((), ())), preferred_element_type=F32)
                s = s + bias_ref[variant, hp * 2 + hh]
                m = jnp.max(s, axis=-1, keepdims=True)
                probs.append(jnp.exp2(s - m).astype(BF16))
                tops.append(m)
            num, den = _pair_weighted_sum(probs, vf[pl.ds(q0, A_KB), cols], left)
            o_ref[0, 0, pl.ds(q0, A_QB), cols] = (num / den).astype(BF16)
            lse_pair = jnp.where(left, tops[0], tops[1]) + jnp.log2(den)
            lse_all = lse_pair if hp == 0 else jnp.where(lse_slot == hp, lse_pair, lse_all)
        l_ref[0, 0, pl.ds(q0, A_QB), :] = lse_all
        return carry

    lax.fori_loop(0, tl // A_QB, block, 0, unroll=True)


def _alibi_bias(dilation):
    slopes = 2.0 ** (-8.0 * jnp.arange(1, A_HEADS + 1, dtype=F32) / A_HEADS)
    key = jnp.arange(A_KB)[None, :] - A_HALF
    rel = key - jnp.arange(A_QB)[:, None]
    dist = (jnp.abs(rel) * dilation).astype(F32)
    bias = -slopes[:, None, None] * dist[None] * LOG2E
    in_window = jnp.abs(rel) <= A_HALF
    variants = []
    for v in range(4):
        ok = in_window
        if v & 1:
            ok = ok & (key >= 0)
        if v & 2:
            ok = ok & (key < A_QB)
        variants.append(jnp.where(ok[None], bias, NEG))
    return jnp.stack(variants, 0)


def mixer_a_group(zg, dilation, q_blk, k_blk, v_blk):
    batch, _, strided_len, _ = zg.shape
    tl = min(512, strided_len)
    halo_per_tile = tl // A_HALF
    n_halo = strided_len // A_HALF

    def cur(c):
        return lambda b, r, i: (b, r, i, c)

    def before(c):
        return lambda b, r, i: (b, r, jnp.maximum(i * halo_per_tile - 1, 0), c)

    def after(c):
        return lambda b, r, i: (b, r, jnp.minimum((i + 1) * halo_per_tile, n_halo - 1), c)

    tile = (1, 1, tl, HEAD_BLOCK)
    halo = (1, 1, A_HALF, HEAD_BLOCK)
    out_spec = pl.BlockSpec(tile, cur(0))
    out_dims = (batch, dilation, strided_len, HEAD_BLOCK)
    return pl.pallas_call(
        functools.partial(_mixer_a_kernel, tl=tl, seq_len=strided_len),
        grid=(batch, dilation, strided_len // tl),
        in_specs=[pl.BlockSpec(tile, cur(q_blk)),
                  pl.BlockSpec(halo, before(k_blk)), pl.BlockSpec(tile, cur(k_blk)), pl.BlockSpec(halo, after(k_blk)),
                  pl.BlockSpec(halo, before(v_blk)), pl.BlockSpec(tile, cur(v_blk)), pl.BlockSpec(halo, after(v_blk)),
                  _const_spec((4, A_HEADS, A_QB, A_KB))],
        out_specs=[out_spec, pl.BlockSpec((1, 1, tl, 2 * A_HD), cur(0))],
        out_shape=[jax.ShapeDtypeStruct(out_dims, BF16),
                   jax.ShapeDtypeStruct((batch, dilation, strided_len, 2 * A_HD), F32)],
        scratch_shapes=[pltpu.VMEM((tl + 2 * A_HALF, HEAD_BLOCK), BF16), pltpu.VMEM((tl + 2 * A_HALF, HEAD_BLOCK), BF16)],
        compiler_params=_params(("arbitrary", "arbitrary", "arbitrary")),
        name=f"mixer_a_d{dilation}",
    )(zg, zg, zg, zg, zg, zg, zg, _alibi_bias(dilation))


def _by_residue(x, batch, seq, dilation):
    w = x.shape[-1]
    return x.reshape(batch, seq // dilation, dilation, w).transpose(0, 2, 1, 3).reshape(batch * seq, w)


def _by_token(x, batch, seq):
    return x.transpose(0, 2, 1, 3).reshape(batch * seq, x.shape[-1])


def _mixer_b_kernel(q_ref, k_ref, v_ref, g_ref, dmat_ref, qdf_ref, qdb_ref, kdf_ref, kdb_ref, cdf_ref, cdb_ref,
                    o_ref, fwd_state, bwd_state, bwd_store, *, n_blocks):
    phase = pl.program_id(1)
    n = pl.program_id(2)
    qk_w = B_HEADS * B_DK
    head_of_lane = lax.broadcasted_iota(jnp.int32, (1, qk_w), 1) // B_DK

    def rows(c):
        return slice(c * B_CHUNK, (c + 1) * B_CHUNK)

    def stacked_heads(t):
        lane_head = jnp.concatenate([head_of_lane] * (t.shape[1] // qk_w), axis=1)
        return jnp.concatenate([jnp.where(lane_head == h, t, jnp.zeros_like(t)) for h in range(B_HEADS)], axis=0)

    def kv_outer(k_decayed, v):
        full = lax.dot_general(k_decayed, v, (((0,), (0,)), ((), ())), preferred_element_type=F32)
        return jnp.concatenate([full[h * B_DK:(h + 1) * B_DK, h * B_DV:(h + 1) * B_DV] for h in range(B_HEADS)], axis=0)

    def scaled_k(c):
        return k_ref[0, rows(c), :] * (B_DK ** -0.5)

    @pl.when(phase == 0)
    def _():
        @pl.when(n == 0)
        def _():
            bwd_state[...] = jnp.zeros_like(bwd_state)

        blk = n_blocks - 1 - n
        state = bwd_state[...]
        for c in reversed(range(B_BLOCK)):
            bwd_store[blk * B_BLOCK + c] = state.astype(BF16)
            k_dec = (scaled_k(c).astype(F32) * kdb_ref[...]).astype(BF16)
            state = cdb_ref[...] * state + kv_outer(k_dec, v_ref[0, rows(c), :])
        bwd_state[...] = state

    @pl.when(phase == 1)
    def _():
        @pl.when(n == 0)
        def _():
            fwd_state[...] = jnp.zeros_like(fwd_state)

        state = fwd_state[...]
        for c in range(B_BLOCK):
            q = q_ref[0, rows(c), :]
            k = scaled_k(c)
            v = v_ref[0, rows(c), :]
            q32 = q.astype(F32)
            q_dec = jnp.concatenate([(q32 * qdf_ref[...]).astype(BF16), (q32 * qdb_ref[...]).astype(BF16)], axis=1)
            k_dec = (k.astype(F32) * kdf_ref[...]).astype(BF16)
            states = jnp.concatenate([state.astype(BF16), bwd_store[n * B_BLOCK + c]], axis=0)
            s_all = lax.dot_general(stacked_heads(q), k, (((1,), (1,)), ((), ())), preferred_element_type=F32)
            cross_all = jnp.dot(stacked_heads(q_dec), states, preferred_element_type=F32)
            for h in range(B_HEADS):
                vh = v[:, h * B_DV:(h + 1) * B_DV]
                inner = jnp.dot((s_all[rows(h), :] * dmat_ref[h]).astype(BF16), vh, preferred_element_type=F32)
                y = inner + cross_all[rows(h), :]
                mu = jnp.mean(y, axis=-1, keepdims=True)
                cen = y - mu
                var = jnp.mean(cen * cen, axis=-1, keepdims=True)
                yn = cen * lax.rsqrt(var + GN_EPS)
                gate = g_ref[0, rows(c), h * B_DV:(h + 1) * B_DV].astype(F32)
                o_ref[0, rows(c), h * B_DV:(h + 1) * B_DV] = (gate * jax.nn.sigmoid(gate) * yn).astype(BF16)
            state = cdf_ref[...] * state + kv_outer(k_dec, v)
        fwd_state[...] = state


def _retention_tables(logit_fwd, logit_bwd):
    lg_f = jax.nn.log_sigmoid(logit_fwd.astype(F32))
    lg_b = jax.nn.log_sigmoid(logit_bwd.astype(F32))
    idx = jnp.arange(B_CHUNK, dtype=F32)
    diff = idx[:, None] - idx[None, :]
    causal = diff >= 0
    dmat = jnp.where(causal[None],
                     jnp.exp(lg_f[:, None, None] * jnp.where(causal, diff, 0.0)[None]),
                     jnp.exp(lg_b[:, None, None] * jnp.where(causal, 0.0, -diff)[None]))

    def per_lane(lg, power):
        return jnp.repeat(jnp.exp(lg[None, :] * power[:, None]), B_DK, axis=1)

    def per_row(lg):
        return jnp.broadcast_to(jnp.repeat(jnp.exp(lg * B_CHUNK), B_DK)[:, None], (B_HEADS * B_DK, B_DV))

    return (dmat, per_lane(lg_f, idx + 1), per_lane(lg_b, B_CHUNK - idx), per_lane(lg_f, B_CHUNK - 1 - idx),
            per_lane(lg_b, idx), per_row(lg_f), per_row(lg_b))


def mixer_b(z, batch, seq, logit_fwd, logit_bwd):
    n_chunks = seq // B_CHUNK
    n_blocks = n_chunks // B_BLOCK
    block_rows = B_BLOCK * B_CHUNK
    zv = z.reshape(batch, seq, MAIN_W)
    qk_w = B_HEADS * B_DK
    v_w = B_HEADS * B_DV

    def scan_block(ph, n):
        return (1 - ph) * (n_blocks - 1 - n) + ph * n

    tables = _retention_tables(logit_fwd, logit_bwd)
    in_specs = [pl.BlockSpec((1, block_rows, qk_w), lambda b, ph, n: (b, ph * n, BQ_BLK256)),
                pl.BlockSpec((1, block_rows, qk_w), lambda b, ph, n: (b, scan_block(ph, n), BK_BLK256)),
                pl.BlockSpec((1, block_rows, v_w), lambda b, ph, n: (b, scan_block(ph, n), BV_BLK)),
                pl.BlockSpec((1, block_rows, v_w), lambda b, ph, n: (b, ph * n, BG_BLK))]
    in_specs += [_const_spec(t.shape) for t in tables]
    out = pl.pallas_call(
        functools.partial(_mixer_b_kernel, n_blocks=n_blocks),
        grid=(batch, 2, n_blocks),
        in_specs=in_specs,
        out_specs=pl.BlockSpec((1, block_rows, v_w), lambda b, ph, n: (b, ph * n, 0)),
        out_shape=jax.ShapeDtypeStruct((batch, seq, v_w), BF16),
        scratch_shapes=[pltpu.VMEM((qk_w, B_DV), F32), pltpu.VMEM((qk_w, B_DV), F32),
                        pltpu.VMEM((n_chunks, qk_w, B_DV), BF16)],
        compiler_params=_params(("arbitrary", "arbitrary", "arbitrary")),
        name="mixer_b",
    )(zv, zv, zv, zv, *tables)
    return out.reshape(batch * seq, v_w)


C_TQ = C_QR * GRID_W
C_KROWS = 3 * C_QR
C_TK = C_KROWS * GRID_W
C_PAIRS = C_KROWS // 2
C_NTAB = 2 * C_KH - 2


def _mixer_c_kernel(q_ref, kp_ref, kc_ref, kn_ref, vp_ref, vc_ref, vn_ref, tab_ref, o_ref, kf, vf, *, rows):
    blk = pl.program_id(1)
    kf[0:C_TQ, :] = kp_ref[0]
    kf[C_TQ:2 * C_TQ, :] = kc_ref[0]
    kf[2 * C_TQ:, :] = kn_ref[0]
    vf[0:C_TQ, :] = vp_ref[0]
    vf[C_TQ:2 * C_TQ, :] = vc_ref[0]
    vf[2 * C_TQ:, :] = vn_ref[0]

    left = lax.broadcasted_iota(jnp.int32, (1, 2 * C_HD), 1) < C_HD
    key_row = blk * C_QR - C_QR + lax.broadcasted_iota(jnp.int32, (1, C_TK), 1) // GRID_W
    pens = []
    for a in range(C_QR):
        row_start = jnp.clip(blk * C_QR + a - C_KH // 2, 0, rows - C_KH)
        pens.append(jnp.where((key_row >= row_start) & (key_row < row_start + C_KH), 0.0, NEG).astype(F32))

    for hp in range(C_HEADS // 2):
        cols = slice(hp * 2 * C_HD, (hp + 1) * 2 * C_HD)
        qp = q_ref[0, :, cols]
        kp = kf[:, cols]
        probs = []
        for hh in range(2):
            h = hp * 2 + hh
            sel = left if hh == 0 else jnp.logical_not(left)
            qm = jnp.where(sel, qp, jnp.zeros_like(qp))
            s = lax.dot_general(qm, kp, (((1,), (1,)), ((), ())), preferred_element_type=F32)
            rows_p = []
            for a in range(C_QR):
                bias = jnp.concatenate([tab_ref[h, 2 * t - C_QR - a + C_KH - 1] for t in range(C_PAIRS)], axis=1)
                sa = s[a * GRID_W:(a + 1) * GRID_W, :] + bias + pens[a]
                m = jnp.max(sa, axis=-1, keepdims=True)
                rows_p.append(jnp.exp2(sa - m).astype(BF16))
            probs.append(jnp.concatenate(rows_p, axis=0))
        num, den = _pair_weighted_sum(probs, vf[:, cols], left)
        o_ref[0, :, cols] = (num / den).astype(BF16)


def _neighbourhood_bias(rpb):
    qc = jnp.arange(GRID_W)[:, None]
    kc = jnp.arange(GRID_W)[None, :]
    col_start = jnp.clip(qc - C_KW // 2, 0, GRID_W - C_KW)
    col_ok = (kc >= col_start) & (kc < col_start + C_KW)
    onehot = ((kc - qc + (C_KW - 1))[:, :, None] == jnp.arange(2 * C_KW - 1)[None, None, :]).astype(F32)
    band = jnp.einsum("hrd,qkd->hrqk", rpb.astype(F32), onehot, precision=lax.Precision.HIGHEST)
    band = jnp.where(col_ok[None, None], band * LOG2E, NEG)
    return jnp.concatenate([band[:, :C_NTAB], band[:, 1:C_NTAB + 1]], axis=-1)


def mixer_c(z, batch, seq, rpb):
    rows = seq // GRID_W
    n_blk = rows // C_QR
    zv = z.reshape(batch, seq, MAIN_W)
    tile = (1, C_TQ, HEAD_BLOCK)
    tab = _neighbourhood_bias(rpb)

    def above(c):
        return lambda b, i: (b, jnp.maximum(i - 1, 0), c)

    def here(c):
        return lambda b, i: (b, i, c)

    def below(c):
        return lambda b, i: (b, jnp.minimum(i + 1, n_blk - 1), c)

    out = pl.pallas_call(
        functools.partial(_mixer_c_kernel, rows=rows),
        grid=(batch, n_blk),
        in_specs=[pl.BlockSpec(tile, here(CQ_BLK)),
                  pl.BlockSpec(tile, above(CK_BLK)), pl.BlockSpec(tile, here(CK_BLK)), pl.BlockSpec(tile, below(CK_BLK)),
                  pl.BlockSpec(tile, above(CV_BLK)), pl.BlockSpec(tile, here(CV_BLK)), pl.BlockSpec(tile, below(CV_BLK)),
                  _const_spec(tab.shape)],
        out_specs=pl.BlockSpec(tile, lambda b, i: (b, i, 0)),
        out_shape=jax.ShapeDtypeStruct((batch, seq, HEAD_BLOCK), BF16),
        scratch_shapes=[pltpu.VMEM((C_TK, HEAD_BLOCK), BF16), pltpu.VMEM((C_TK, HEAD_BLOCK), BF16)],
        compiler_params=_params(("arbitrary", "arbitrary")),
        name="mixer_c",
    )(zv, zv, zv, zv, zv, zv, zv, tab)
    return out.reshape(batch * seq, HEAD_BLOCK)


def _merge_kernel(x_ref, oa0_ref, oa1_ref, oa2_ref, la0_ref, la1_ref, la2_ref, yb_ref, yc_ref, gates_ref, spread_ref,
                  wa_ref, wb_ref, wc_ref, wo_ref, g_ref, b_ref, o_ref):
    lses = [la0_ref[...], la1_ref[...], la2_ref[...]]
    top = jnp.maximum(jnp.maximum(lses[0], lses[1]), lses[2])
    weights = [jnp.exp2(lse - top) for lse in lses]
    total = weights[0] + weights[1] + weights[2]
    ya = None
    for o_g_ref, w in zip((oa0_ref, oa1_ref, oa2_ref), weights):
        share = w / total
        hi = share.astype(BF16)
        lo = (share - hi.astype(F32)).astype(BF16)
        wide = jnp.dot(jnp.concatenate([hi, lo], axis=1), spread_ref[...], preferred_element_type=F32)
        term = wide * o_g_ref[...].astype(F32)
        ya = term if ya is None else ya + term
    ya = ya.astype(BF16)

    merged = None
    for br, (y, w_ref) in enumerate(((ya, wa_ref), (yb_ref[...], wb_ref), (yc_ref[...], wc_ref))):
        proj = jnp.dot(y, w_ref[...], preferred_element_type=F32)
        gate = jax.nn.sigmoid(gates_ref[:, br * D_MODEL:(br + 1) * D_MODEL].astype(F32))
        merged = gate * proj if merged is None else merged + gate * proj
    out = jnp.dot(merged.astype(BF16), wo_ref[...], preferred_element_type=F32)
    o_ref[...] = _layer_norm(ALPHA * x_ref[...] + out, g_ref[...], b_ref[...])


def _lse_spread():
    src = jnp.arange(2 * A_HD)[:, None]
    head = jnp.arange(HEAD_BLOCK)[None, :] // A_HD
    first_lane = (head % 2) * A_HD + (head // 2) * LSE_LANES
    once = (src == first_lane).astype(BF16)
    return jnp.concatenate([once, once], axis=0)


def merge_out_ln(x, oa, la, yb, yc, z, wa, wb, wc, wo, g, b):
    t = x.shape[0]
    tm = 512
    row = pl.BlockSpec((tm, D_MODEL), lambda i: (i, 0))
    br = pl.BlockSpec((tm, HEAD_BLOCK), lambda i: (i, 0))
    lse = pl.BlockSpec((tm, 2 * A_HD), lambda i: (i, 0))
    w_br = _const_spec((HEAD_BLOCK, D_MODEL))
    return pl.pallas_call(
        _merge_kernel,
        grid=(t // tm,),
        in_specs=[row] + [br] * 3 + [lse] * 3 + [br] * 2 + [
            pl.BlockSpec((tm, GATE_W), lambda i: (i, 0)), _const_spec((4 * A_HD, HEAD_BLOCK)), w_br, w_br, w_br,
            _const_spec((D_MODEL, D_MODEL)), _const_spec((1, D_MODEL)), _const_spec((1, D_MODEL))],
        out_specs=row,
        out_shape=jax.ShapeDtypeStruct((t, D_MODEL), F32),
        compiler_params=_params(("arbitrary",)),
        name="merge_out_ln",
    )(x, *oa, *la, yb, yc, z, _lse_spread(), wa, wb, wc, wo, g, b)


def _split_in_weights(w_in):
    a_w = A_GROUPS * HEAD_BLOCK
    aq, ak, av = w_in[:, :a_w] * (A_HD ** -0.5 * LOG2E), w_in[:, a_w:2 * a_w], w_in[:, 2 * a_w:3 * a_w]
    b_w = 2 * B_HEADS * (B_DK + B_DV)
    rest_b = w_in[:, 3 * a_w:3 * a_w + b_w]
    cq = w_in[:, 3 * a_w + b_w:3 * a_w + b_w + HEAD_BLOCK] * (C_HD ** -0.5 * LOG2E)
    rest = jnp.concatenate([rest_b, cq, w_in[:, 3 * a_w + b_w + HEAD_BLOCK:D_IN - GATE_W]], axis=1)

    def group(g):
        cols = slice(g * HEAD_BLOCK, (g + 1) * HEAD_BLOCK)
        return jnp.concatenate([aq[:, cols], ak[:, cols], av[:, cols]], axis=1)

    main = jnp.concatenate([w_in[:, D_IN - GATE_W:], group(0), rest], axis=1)
    return main.astype(BF16), [group(g).astype(BF16) for g in range(1, A_GROUPS)]


def _trunk(x, layers):
    batch, seq, _ = x.shape
    x = x.reshape(batch * seq, D_MODEL)
    for p in layers:
        x1, x1b = ffn_ln(x, p["wg1"], p["wu1"], p["wd1"], p["g1"], p["b1"], with_bf16=True)
        z = in_proj(x1b, p["w_main"])
        oa, la = [], []
        o, l = mixer_a_group(z.reshape(batch, 1, seq, MAIN_W), 1, AQ_BLK, AK_BLK, AV_BLK)
        oa.append(o.reshape(batch * seq, HEAD_BLOCK))
        la.append(l.reshape(batch * seq, 2 * A_HD))
        for g in range(1, A_GROUPS):
            dilation = A_PATTERNS[g][1]
            zg = in_proj(_by_residue(x1b, batch, seq, dilation), p["w_groups"][g - 1])
            o, l = mixer_a_group(zg.reshape(batch, dilation, seq // dilation, A_QKV_W), dilation, 0, 1, 2)
            oa.append(_by_token(o, batch, seq))
            la.append(_by_token(l, batch, seq))
        yb = mixer_b(z, batch, seq, p["logit_fwd"], p["logit_bwd"])
        yc = mixer_c(z, batch, seq, p["rpb"])
        x2 = merge_out_ln(x1, oa, la, yb, yc, z, p["wa"], p["wb"], p["wc"], p["wo"], p["g2"], p["b2"])
        (x,) = ffn_ln(x2, p["wg2"], p["wu2"], p["wd2"], p["g3"], p["b3"], with_bf16=False)
    return x.reshape(batch, seq, D_MODEL)


def kernel(x_prompt, x_sample, ffn1_w_gate, ffn1_w_up, ffn1_w_down, ln1_g, ln1_b, w_in, ret_logit_fwd, ret_logit_bwd, na_rpb, w_branch_a, w_branch_b, w_branch_c, w_out, ln2_g, ln2_b, ffn2_w_gate, ffn2_w_up, ffn2_w_down, ln3_g, ln3_b):
    def vec(v):
        return v.astype(F32).reshape(1, D_MODEL)

    layers = []
    for i in range(DEPTH):
        w_main, w_groups = _split_in_weights(w_in[i])
        layers.append(dict(
            wg1=ffn1_w_gate[i].astype(BF16), wu1=ffn1_w_up[i].astype(BF16), wd1=ffn1_w_down[i].astype(BF16),
            g1=vec(ln1_g[i]), b1=vec(ln1_b[i]),
            w_main=w_main, w_groups=w_groups,
            logit_fwd=ret_logit_fwd[i], logit_bwd=ret_logit_bwd[i], rpb=na_rpb[i],
            wa=w_branch_a[i].astype(BF16), wb=w_branch_b[i].astype(BF16), wc=w_branch_c[i].astype(BF16),
            wo=w_out[i].astype(BF16), g2=vec(ln2_g[i]), b2=vec(ln2_b[i]),
            wg2=ffn2_w_gate[i].astype(BF16), wu2=ffn2_w_up[i].astype(BF16), wd2=ffn2_w_down[i].astype(BF16),
            g3=vec(ln3_g[i]), b3=vec(ln3_b[i])))
    return (_trunk(x_prompt, layers), _trunk(x_sample, layers))
```

```python
import functools

import jax
import jax.numpy as jnp
from jax import lax
from jax.experimental import pallas as pl
from jax.experimental.pallas import tpu as pltpu

F32 = jnp.float32
BF16 = jnp.bfloat16

D_MODEL = 1024
DEPTH = 2
D_FF = 2816
LN_EPS = 1e-5
GN_EPS = 1e-5
ALPHA = (2 * DEPTH) ** 0.25

A_PATTERNS = ((128, 1), (512, 4), (2048, 16))
A_GROUPS = len(A_PATTERNS)
A_HEADS = 8
A_HD = 64
A_HALF = 64
LSE_LANES = A_HD // (A_HEADS // 2)
B_HEADS = 4
B_DK = 64
B_DV = 128
B_CHUNK = 128
B_BLOCK = 4
C_HEADS = 8
C_HD = 64
GRID_W = 64
C_KH = 8
C_KW = 16
C_QR = 4
D_IN = 10752
NEG = -1e30
LANES = 128
LOG2E = 1.4426950408889634

GATE_W = 3 * D_MODEL
HEAD_BLOCK = 512
A_QKV_W = 3 * HEAD_BLOCK
MAIN_W = D_IN - (A_GROUPS - 1) * A_QKV_W
AQ_BLK, AK_BLK, AV_BLK = 6, 7, 8
BQ_BLK256, BK_BLK256 = 18, 19
BV_BLK, BG_BLK = 10, 11
CQ_BLK, CK_BLK, CV_BLK = 12, 13, 14

VMEM_LIMIT = 56 * 1024 * 1024
FF_CHUNKS = ((0, 512), (512, 1024), (1024, 1536), (1536, 2048), (2048, 2560), (2560, 2816))


def _params(sem):
    return pltpu.CompilerParams(dimension_semantics=sem, vmem_limit_bytes=VMEM_LIMIT)


def _const_spec(shape):
    zeros = (0,) * len(shape)
    return pl.BlockSpec(shape, lambda *_: zeros)


def _layer_norm(r, g, b):
    mu = jnp.mean(r, axis=-1, keepdims=True)
    c = r - mu
    var = jnp.mean(c * c, axis=-1, keepdims=True)
    return c * lax.rsqrt(var + LN_EPS) * g + b


def _ffn_ln_kernel(x_ref, wg_ref, wu_ref, wd_ref, g_ref, b_ref, o_ref, *rest, dilations):
    x = x_ref[...]
    xb = x.astype(BF16)
    acc = None
    for c0, c1 in FF_CHUNKS:
        gate = jnp.dot(xb, wg_ref[:, c0:c1], preferred_element_type=F32)
        up = jnp.dot(xb, wu_ref[:, c0:c1], preferred_element_type=F32)
        h = (gate * jax.nn.sigmoid(gate) * up).astype(BF16)
        part = jnp.dot(h, wd_ref[c0:c1, :], preferred_element_type=F32)
        acc = part if acc is None else acc + part
    y = _layer_norm(ALPHA * x + 0.5 * acc, g_ref[...], b_ref[...])
    o_ref[...] = y
    if not rest:
        return
    ob_ref, *og_refs, slabs = rest
    ob_ref[...] = y.astype(BF16)
    rows = y.shape[0]
    n_slabs = D_MODEL // LANES
    for s in range(n_slabs):
        slabs[s] = y[:, s * LANES:(s + 1) * LANES]
    for d, og_ref in zip(dilations, og_refs):
        for r in range(d):
            picked = [slabs[s, pl.ds(r, rows // d, stride=d), :] for s in range(n_slabs)]
            og_ref[0, r] = jnp.concatenate(picked, axis=1).astype(BF16)


def ffn_ln(x, wg, wu, wd, g, b, batch=None, dilations=()):
    t = x.shape[0]
    tm = 512
    row = pl.BlockSpec((tm, D_MODEL), lambda i: (i, 0))
    out_shape = [jax.ShapeDtypeStruct((t, D_MODEL), F32)]
    out_specs = [row]
    scratch = []
    if dilations:
        tiles_per_seq = t // batch // tm
        out_shape.append(jax.ShapeDtypeStruct((t, D_MODEL), BF16))
        out_specs.append(row)
        for d in dilations:
            out_shape.append(jax.ShapeDtypeStruct((batch, d, t // batch // d, D_MODEL), BF16))
            out_specs.append(pl.BlockSpec((1, d, tm // d, D_MODEL),
                                          lambda i: (i // tiles_per_seq, 0, i % tiles_per_seq, 0)))
        scratch = [pltpu.VMEM((D_MODEL // LANES, tm, LANES), F32)]
    return pl.pallas_call(
        functools.partial(_ffn_ln_kernel, dilations=dilations),
        grid=(t // tm,),
        in_specs=[row, _const_spec((D_MODEL, D_FF)), _const_spec((D_MODEL, D_FF)), _const_spec((D_FF, D_MODEL)),
                  _const_spec((1, D_MODEL)), _const_spec((1, D_MODEL))],
        out_specs=out_specs,
        out_shape=out_shape,
        scratch_shapes=scratch,
        compiler_params=_params(("arbitrary",)),
        name="ffn_ln",
    )(x, wg, wu, wd, g, b)


def _in_proj_kernel(x_ref, w_ref, z_ref):
    z_ref[...] = jnp.dot(x_ref[...], w_ref[...], preferred_element_type=F32).astype(BF16)


def in_proj(xb, w_in):
    t = xb.shape[0]
    width = w_in.shape[1]
    tm, tn = 2048, A_QKV_W
    return pl.pallas_call(
        _in_proj_kernel,
        grid=(t // tm, width // tn),
        in_specs=[pl.BlockSpec((tm, D_MODEL), lambda i, j: (i, 0)), pl.BlockSpec((D_MODEL, tn), lambda i, j: (0, j))],
        out_specs=pl.BlockSpec((tm, tn), lambda i, j: (i, j)),
        out_shape=jax.ShapeDtypeStruct((t, width), BF16),
        compiler_params=_params(("arbitrary", "arbitrary")),
        name="in_proj",
    )(xb, w_in)


def _pair_scores(q_pair, k_pair, left):
    zero = jnp.zeros_like(q_pair)
    dims = (((1,), (1,)), ((), ()))
    return [lax.dot_general(jnp.where(left, q_pair, zero), k_pair, dims, preferred_element_type=F32),
            lax.dot_general(jnp.where(left, zero, q_pair), k_pair, dims, preferred_element_type=F32)]


def _pair_weighted_sum(probs, v_pair, left):
    rhs = []
    for hh in range(2):
        sel = left if hh == 0 else jnp.logical_not(left)
        ones = jnp.broadcast_to(jnp.where(sel, 1.0, 0.0).astype(BF16), v_pair.shape)
        rhs.append(jnp.concatenate([jnp.where(sel, v_pair, jnp.zeros_like(v_pair)), ones], axis=1))
    out = jnp.dot(jnp.concatenate(probs, axis=1), jnp.concatenate(rhs, axis=0), preferred_element_type=F32)
    width = v_pair.shape[1]
    return out[:, :width], out[:, width:]


A_QB = 128
A_KB = A_QB + 2 * A_HALF


def _mixer_a_kernel(q_ref, kp_ref, kc_ref, kn_ref, vp_ref, vc_ref, vn_ref, bias_ref, o_ref, l_ref, kf, vf, *,
                    tl, seq_len):
    i = pl.program_id(2)
    kf[0:A_HALF, :] = kp_ref[0, 0]
    kf[A_HALF:A_HALF + tl, :] = kc_ref[0, 0]
    kf[A_HALF + tl:, :] = kn_ref[0, 0]
    vf[0:A_HALF, :] = vp_ref[0, 0]
    vf[A_HALF:A_HALF + tl, :] = vc_ref[0, 0]
    vf[A_HALF + tl:, :] = vn_ref[0, 0]

    lane = lax.broadcasted_iota(jnp.int32, (1, 2 * A_HD), 1)
    left = lane < A_HD
    lse_slot = (lane % A_HD) // LSE_LANES

    def block(j, carry):
        q0 = pl.multiple_of(j * A_QB, A_QB)
        start = i * tl + q0
        variant = (start == 0).astype(jnp.int32) + 2 * (start + A_QB == seq_len).astype(jnp.int32)
        lse_all = None
        for hp in range(A_HEADS // 2):
            cols = slice(hp * 2 * A_HD, (hp + 1) * 2 * A_HD)
            qp = q_ref[0, 0, pl.ds(q0, A_QB), cols]
            kp = kf[pl.ds(q0, A_KB), cols]
            probs, tops = [], []
            for hh, s in enumerate(_pair_scores(qp, kp, left)):
                s = s + bias_ref[variant, hp * 2 + hh]
                m = jnp.max(s, axis=-1, keepdims=True)
                probs.append(jnp.exp2(s - m).astype(BF16))
                tops.append(m)
            num, den = _pair_weighted_sum(probs, vf[pl.ds(q0, A_KB), cols], left)
            o_ref[0, 0, pl.ds(q0, A_QB), cols] = (num / den).astype(BF16)
            lse_pair = jnp.where(left, tops[0], tops[1]) + jnp.log2(den)
            lse_all = lse_pair if hp == 0 else jnp.where(lse_slot == hp, lse_pair, lse_all)
        l_ref[0, 0, pl.ds(q0, A_QB), :] = lse_all
        return carry

    lax.fori_loop(0, tl // A_QB, block, 0, unroll=True)


def _alibi_bias(dilation):
    slopes = 2.0 ** (-8.0 * jnp.arange(1, A_HEADS + 1, dtype=F32) / A_HEADS)
    key = jnp.arange(A_KB)[None, :] - A_HALF
    rel = key - jnp.arange(A_QB)[:, None]
    dist = (jnp.abs(rel) * dilation).astype(F32)
    bias = -slopes[:, None, None] * dist[None] * LOG2E
    in_window = jnp.abs(rel) <= A_HALF
    variants = []
    for v in range(4):
        ok = in_window
        if v & 1:
            ok = ok & (key >= 0)
        if v & 2:
            ok = ok & (key < A_QB)
        variants.append(jnp.where(ok[None], bias, NEG))
    return jnp.stack(variants, 0)


def mixer_a_group(zg, dilation, q_blk, k_blk, v_blk):
    batch, _, strided_len, _ = zg.shape
    tl = min(512, strided_len)
    halo_per_tile = tl // A_HALF
    n_halo = strided_len // A_HALF

    def cur(c):
        return lambda b, r, i: (b, r, i, c)

    def before(c):
        return lambda b, r, i: (b, r, jnp.maximum(i * halo_per_tile - 1, 0), c)

    def after(c):
        return lambda b, r, i: (b, r, jnp.minimum((i + 1) * halo_per_tile, n_halo - 1), c)

    tile = (1, 1, tl, HEAD_BLOCK)
    halo = (1, 1, A_HALF, HEAD_BLOCK)
    out_spec = pl.BlockSpec(tile, cur(0))
    out_dims = (batch, dilation, strided_len, HEAD_BLOCK)
    return pl.pallas_call(
        functools.partial(_mixer_a_kernel, tl=tl, seq_len=strided_len),
        grid=(batch, dilation, strided_len // tl),
        in_specs=[pl.BlockSpec(tile, cur(q_blk)),
                  pl.BlockSpec(halo, before(k_blk)), pl.BlockSpec(tile, cur(k_blk)), pl.BlockSpec(halo, after(k_blk)),
                  pl.BlockSpec(halo, before(v_blk)), pl.BlockSpec(tile, cur(v_blk)), pl.BlockSpec(halo, after(v_blk)),
                  _const_spec((4, A_HEADS, A_QB, A_KB))],
        out_specs=[out_spec, pl.BlockSpec((1, 1, tl, 2 * A_HD), cur(0))],
        out_shape=[jax.ShapeDtypeStruct(out_dims, BF16),
                   jax.ShapeDtypeStruct((batch, dilation, strided_len, 2 * A_HD), F32)],
        scratch_shapes=[pltpu.VMEM((tl + 2 * A_HALF, HEAD_BLOCK), BF16), pltpu.VMEM((tl + 2 * A_HALF, HEAD_BLOCK), BF16)],
        compiler_params=_params(("arbitrary", "arbitrary", "arbitrary")),
        name=f"mixer_a_d{dilation}",
    )(zg, zg, zg, zg, zg, zg, zg, _alibi_bias(dilation))


def _by_token(x, batch, seq):
    return x.transpose(0, 2, 1, 3).reshape(batch * seq, x.shape[-1])


def _mixer_b_kernel(q_ref, k_ref, v_ref, g_ref, dmat_ref, qdf_ref, qdb_ref, kdf_ref, kdb_ref, cdf_ref, cdb_ref,
                    o_ref, fwd_state, bwd_state, bwd_store, *, n_blocks):
    phase = pl.program_id(1)
    n = pl.program_id(2)
    qk_w = B_HEADS * B_DK
    head_of_lane = lax.broadcasted_iota(jnp.int32, (1, qk_w), 1) // B_DK

    def rows(c):
        return slice(c * B_CHUNK, (c + 1) * B_CHUNK)

    def stacked_heads(t):
        lane_head = jnp.concatenate([head_of_lane] * (t.shape[1] // qk_w), axis=1)
        return jnp.concatenate([jnp.where(lane_head == h, t, jnp.zeros_like(t)) for h in range(B_HEADS)], axis=0)

    def kv_outer(k_decayed, v):
        full = lax.dot_general(k_decayed, v, (((0,), (0,)), ((), ())), preferred_element_type=F32)
        return jnp.concatenate([full[h * B_DK:(h + 1) * B_DK, h * B_DV:(h + 1) * B_DV] for h in range(B_HEADS)], axis=0)

    def scaled_k(c):
        return k_ref[0, rows(c), :] * (B_DK ** -0.5)

    @pl.when(phase == 0)
    def _():
        @pl.when(n == 0)
        def _():
            bwd_state[...] = jnp.zeros_like(bwd_state)

        blk = n_blocks - 1 - n
        state = bwd_state[...]
        for c in reversed(range(B_BLOCK)):
            bwd_store[blk * B_BLOCK + c] = state.astype(BF16)
            k_dec = (scaled_k(c).astype(F32) * kdb_ref[...]).astype(BF16)
            state = cdb_ref[...] * state + kv_outer(k_dec, v_ref[0, rows(c), :])
        bwd_state[...] = state

    @pl.when(phase == 1)
    def _():
        @pl.when(n == 0)
        def _():
            fwd_state[...] = jnp.zeros_like(fwd_state)

        state = fwd_state[...]
        for c in range(B_BLOCK):
            q = q_ref[0, rows(c), :]
            k = scaled_k(c)
            v = v_ref[0, rows(c), :]
            q32 = q.astype(F32)
            q_dec = jnp.concatenate([(q32 * qdf_ref[...]).astype(BF16), (q32 * qdb_ref[...]).astype(BF16)], axis=1)
            k_dec = (k.astype(F32) * kdf_ref[...]).astype(BF16)
            states = jnp.concatenate([state.astype(BF16), bwd_store[n * B_BLOCK + c]], axis=0)
            s_all = lax.dot_general(stacked_heads(q), k, (((1,), (1,)), ((), ())), preferred_element_type=F32)
            cross_all = jnp.dot(stacked_heads(q_dec), states, preferred_element_type=F32)
            for h in range(B_HEADS):
                vh = v[:, h * B_DV:(h + 1) * B_DV]
                inner = jnp.dot((s_all[rows(h), :] * dmat_ref[h]).astype(BF16), vh, preferred_element_type=F32)
                y = inner + cross_all[rows(h), :]
                mu = jnp.mean(y, axis=-1, keepdims=True)
                cen = y - mu
                var = jnp.mean(cen * cen, axis=-1, keepdims=True)
                yn = cen * lax.rsqrt(var + GN_EPS)
                gate = g_ref[0, rows(c), h * B_DV:(h + 1) * B_DV].astype(F32)
                o_ref[0, rows(c), h * B_DV:(h + 1) * B_DV] = (gate * jax.nn.sigmoid(gate) * yn).astype(BF16)
            state = cdf_ref[...] * state + kv_outer(k_dec, v)
        fwd_state[...] = state


def _retention_tables(logit_fwd, logit_bwd):
    lg_f = jax.nn.log_sigmoid(logit_fwd.astype(F32))
    lg_b = jax.nn.log_sigmoid(logit_bwd.astype(F32))
    idx = jnp.arange(B_CHUNK, dtype=F32)
    diff = idx[:, None] - idx[None, :]
    causal = diff >= 0
    dmat = jnp.where(causal[None],
                     jnp.exp(lg_f[:, None, None] * jnp.where(causal, diff, 0.0)[None]),
                     jnp.exp(lg_b[:, None, None] * jnp.where(causal, 0.0, -diff)[None]))

    def per_lane(lg, power):
        return jnp.repeat(jnp.exp(lg[None, :] * power[:, None]), B_DK, axis=1)

    def per_row(lg):
        return jnp.broadcast_to(jnp.repeat(jnp.exp(lg * B_CHUNK), B_DK)[:, None], (B_HEADS * B_DK, B_DV))

    return (dmat, per_lane(lg_f, idx + 1), per_lane(lg_b, B_CHUNK - idx), per_lane(lg_f, B_CHUNK - 1 - idx),
            per_lane(lg_b, idx), per_row(lg_f), per_row(lg_b))


def mixer_b(z, batch, seq, logit_fwd, logit_bwd):
    n_chunks = seq // B_CHUNK
    n_blocks = n_chunks // B_BLOCK
    block_rows = B_BLOCK * B_CHUNK
    zv = z.reshape(batch, seq, MAIN_W)
    qk_w = B_HEADS * B_DK
    v_w = B_HEADS * B_DV

    def scan_block(ph, n):
        return (1 - ph) * (n_blocks - 1 - n) + ph * n

    tables = _retention_tables(logit_fwd, logit_bwd)
    in_specs = [pl.BlockSpec((1, block_rows, qk_w), lambda b, ph, n: (b, ph * n, BQ_BLK256)),
                pl.BlockSpec((1, block_rows, qk_w), lambda b, ph, n: (b, scan_block(ph, n), BK_BLK256)),
                pl.BlockSpec((1, block_rows, v_w), lambda b, ph, n: (b, scan_block(ph, n), BV_BLK)),
                pl.BlockSpec((1, block_rows, v_w), lambda b, ph, n: (b, ph * n, BG_BLK))]
    in_specs += [_const_spec(t.shape) for t in tables]
    out = pl.pallas_call(
        functools.partial(_mixer_b_kernel, n_blocks=n_blocks),
        grid=(batch, 2, n_blocks),
        in_specs=in_specs,
        out_specs=pl.BlockSpec((1, block_rows, v_w), lambda b, ph, n: (b, ph * n, 0)),
        out_shape=jax.ShapeDtypeStruct((batch, seq, v_w), BF16),
        scratch_shapes=[pltpu.VMEM((qk_w, B_DV), F32), pltpu.VMEM((qk_w, B_DV), F32),
                        pltpu.VMEM((n_chunks, qk_w, B_DV), BF16)],
        compiler_params=_params(("arbitrary", "arbitrary", "arbitrary")),
        name="mixer_b",
    )(zv, zv, zv, zv, *tables)
    return out.reshape(batch * seq, v_w)


C_TQ = C_QR * GRID_W
C_KROWS = 3 * C_QR
C_TK = C_KROWS * GRID_W
C_PAIRS = C_KROWS // 2
C_NTAB = 2 * C_KH - 2


def _mixer_c_kernel(q_ref, kp_ref, kc_ref, kn_ref, vp_ref, vc_ref, vn_ref, tab_ref, o_ref, kf, vf, *, rows):
    blk = pl.program_id(1)
    kf[0:C_TQ, :] = kp_ref[0]
    kf[C_TQ:2 * C_TQ, :] = kc_ref[0]
    kf[2 * C_TQ:, :] = kn_ref[0]
    vf[0:C_TQ, :] = vp_ref[0]
    vf[C_TQ:2 * C_TQ, :] = vc_ref[0]
    vf[2 * C_TQ:, :] = vn_ref[0]

    left = lax.broadcasted_iota(jnp.int32, (1, 2 * C_HD), 1) < C_HD
    key_row = blk * C_QR - C_QR + lax.broadcasted_iota(jnp.int32, (1, C_TK), 1) // GRID_W
    pens = []
    for a in range(C_QR):
        row_start = jnp.clip(blk * C_QR + a - C_KH // 2, 0, rows - C_KH)
        pens.append(jnp.where((key_row >= row_start) & (key_row < row_start + C_KH), 0.0, NEG).astype(F32))

    for hp in range(C_HEADS // 2):
        cols = slice(hp * 2 * C_HD, (hp + 1) * 2 * C_HD)
        qp = q_ref[0, :, cols]
        kp = kf[:, cols]
        probs = []
        for hh, s in enumerate(_pair_scores(qp, kp, left)):
            h = hp * 2 + hh
            rows_p = []
            for a in range(C_QR):
                bias = jnp.concatenate([tab_ref[h, 2 * t - C_QR - a + C_KH - 1] for t in range(C_PAIRS)], axis=1)
                sa = s[a * GRID_W:(a + 1) * GRID_W, :] + bias + pens[a]
                m = jnp.max(sa, axis=-1, keepdims=True)
                rows_p.append(jnp.exp2(sa - m).astype(BF16))
            probs.append(jnp.concatenate(rows_p, axis=0))
        num, den = _pair_weighted_sum(probs, vf[:, cols], left)
        o_ref[0, :, cols] = (num / den).astype(BF16)


def _neighbourhood_bias(rpb):
    qc = jnp.arange(GRID_W)[:, None]
    kc = jnp.arange(GRID_W)[None, :]
    col_start = jnp.clip(qc - C_KW // 2, 0, GRID_W - C_KW)
    col_ok = (kc >= col_start) & (kc < col_start + C_KW)
    onehot = ((kc - qc + (C_KW - 1))[:, :, None] == jnp.arange(2 * C_KW - 1)[None, None, :]).astype(F32)
    band = jnp.einsum("hrd,qkd->hrqk", rpb.astype(F32), onehot, precision=lax.Precision.HIGHEST)
    band = jnp.where(col_ok[None, None], band * LOG2E, NEG)
    return jnp.concatenate([band[:, :C_NTAB], band[:, 1:C_NTAB + 1]], axis=-1)


def mixer_c(z, batch, seq, rpb):
    rows = seq // GRID_W
    n_blk = rows // C_QR
    zv = z.reshape(batch, seq, MAIN_W)
    tile = (1, C_TQ, HEAD_BLOCK)
    tab = _neighbourhood_bias(rpb)

    def above(c):
        return lambda b, i: (b, jnp.maximum(i - 1, 0), c)

    def here(c):
        return lambda b, i: (b, i, c)

    def below(c):
        return lambda b, i: (b, jnp.minimum(i + 1, n_blk - 1), c)

    out = pl.pallas_call(
        functools.partial(_mixer_c_kernel, rows=rows),
        grid=(batch, n_blk),
        in_specs=[pl.BlockSpec(tile, here(CQ_BLK)),
                  pl.BlockSpec(tile, above(CK_BLK)), pl.BlockSpec(tile, here(CK_BLK)), pl.BlockSpec(tile, below(CK_BLK)),
                  pl.BlockSpec(tile, above(CV_BLK)), pl.BlockSpec(tile, here(CV_BLK)), pl.BlockSpec(tile, below(CV_BLK)),
                  _const_spec(tab.shape)],
        out_specs=pl.BlockSpec(tile, lambda b, i: (b, i, 0)),
        out_shape=jax.ShapeDtypeStruct((batch, seq, HEAD_BLOCK), BF16),
        scratch_shapes=[pltpu.VMEM((C_TK, HEAD_BLOCK), BF16), pltpu.VMEM((C_TK, HEAD_BLOCK), BF16)],
        compiler_params=_params(("arbitrary", "arbitrary")),
        name="mixer_c",
    )(zv, zv, zv, zv, zv, zv, zv, tab)
    return out.reshape(batch * seq, HEAD_BLOCK)


def _merge_kernel(x_ref, oa0_ref, oa1_ref, oa2_ref, la0_ref, la1_ref, la2_ref, yb_ref, yc_ref, gates_ref, spread_ref,
                  wa_ref, wb_ref, wc_ref, wo_ref, g_ref, b_ref, o_ref):
    lses = [la0_ref[...], la1_ref[...], la2_ref[...]]
    top = jnp.maximum(jnp.maximum(lses[0], lses[1]), lses[2])
    weights = [jnp.exp2(lse - top) for lse in lses]
    total = weights[0] + weights[1] + weights[2]
    ya = None
    for o_g_ref, w in zip((oa0_ref, oa1_ref, oa2_ref), weights):
        share = w / total
        hi = share.astype(BF16)
        lo = (share - hi.astype(F32)).astype(BF16)
        wide = jnp.dot(jnp.concatenate([hi, lo], axis=1), spread_ref[...], preferred_element_type=F32)
        term = wide * o_g_ref[...].astype(F32)
        ya = term if ya is None else ya + term
    ya = ya.astype(BF16)

    merged = None
    for br, (y, w_ref) in enumerate(((ya, wa_ref), (yb_ref[...], wb_ref), (yc_ref[...], wc_ref))):
        proj = jnp.dot(y, w_ref[...], preferred_element_type=F32)
        gate = jax.nn.sigmoid(gates_ref[:, br * D_MODEL:(br + 1) * D_MODEL].astype(F32))
        merged = gate * proj if merged is None else merged + gate * proj
    out = jnp.dot(merged.astype(BF16), wo_ref[...], preferred_element_type=F32)
    o_ref[...] = _layer_norm(ALPHA * x_ref[...] + out, g_ref[...], b_ref[...])


def _lse_spread():
    src = jnp.arange(2 * A_HD)[:, None]
    head = jnp.arange(HEAD_BLOCK)[None, :] // A_HD
    first_lane = (head % 2) * A_HD + (head // 2) * LSE_LANES
    once = (src == first_lane).astype(BF16)
    return jnp.concatenate([once, once], axis=0)


def merge_out_ln(x, oa, la, yb, yc, z, wa, wb, wc, wo, g, b):
    t = x.shape[0]
    tm = 512
    row = pl.BlockSpec((tm, D_MODEL), lambda i: (i, 0))
    br = pl.BlockSpec((tm, HEAD_BLOCK), lambda i: (i, 0))
    lse = pl.BlockSpec((tm, 2 * A_HD), lambda i: (i, 0))
    w_br = _const_spec((HEAD_BLOCK, D_MODEL))
    return pl.pallas_call(
        _merge_kernel,
        grid=(t // tm,),
        in_specs=[row] + [br] * 3 + [lse] * 3 + [br] * 2 + [
            pl.BlockSpec((tm, GATE_W), lambda i: (i, 0)), _const_spec((4 * A_HD, HEAD_BLOCK)), w_br, w_br, w_br,
            _const_spec((D_MODEL, D_MODEL)), _const_spec((1, D_MODEL)), _const_spec((1, D_MODEL))],
        out_specs=row,
        out_shape=jax.ShapeDtypeStruct((t, D_MODEL), F32),
        compiler_params=_params(("arbitrary",)),
        name="merge_out_ln",
    )(x, *oa, *la, yb, yc, z, _lse_spread(), wa, wb, wc, wo, g, b)


def _split_in_weights(w_in):
    a_w = A_GROUPS * HEAD_BLOCK
    aq, ak, av = w_in[:, :a_w] * (A_HD ** -0.5 * LOG2E), w_in[:, a_w:2 * a_w], w_in[:, 2 * a_w:3 * a_w]
    b_w = 2 * B_HEADS * (B_DK + B_DV)
    rest_b = w_in[:, 3 * a_w:3 * a_w + b_w]
    cq = w_in[:, 3 * a_w + b_w:3 * a_w + b_w + HEAD_BLOCK] * (C_HD ** -0.5 * LOG2E)
    rest = jnp.concatenate([rest_b, cq, w_in[:, 3 * a_w + b_w + HEAD_BLOCK:D_IN - GATE_W]], axis=1)

    def group(g):
        cols = slice(g * HEAD_BLOCK, (g + 1) * HEAD_BLOCK)
        return jnp.concatenate([aq[:, cols], ak[:, cols], av[:, cols]], axis=1)

    main = jnp.concatenate([w_in[:, D_IN - GATE_W:], group(0), rest], axis=1)
    return main.astype(BF16), [group(g).astype(BF16) for g in range(1, A_GROUPS)]


def _trunk(x, layers):
    batch, seq, _ = x.shape
    x = x.reshape(batch * seq, D_MODEL)
    for p in layers:
        dils = tuple(d for _, d in A_PATTERNS[1:])
        x1, x1b, *x1_by_residue = ffn_ln(x, p["wg1"], p["wu1"], p["wd1"], p["g1"], p["b1"], batch, dils)
        z = in_proj(x1b, p["w_main"])
        oa, la = [], []
        o, l = mixer_a_group(z.reshape(batch, 1, seq, MAIN_W), 1, AQ_BLK, AK_BLK, AV_BLK)
        oa.append(o.reshape(batch * seq, HEAD_BLOCK))
        la.append(l.reshape(batch * seq, 2 * A_HD))
        for dilation, xg, w_g in zip(dils, x1_by_residue, p["w_groups"]):
            zg = in_proj(xg.reshape(batch * seq, D_MODEL), w_g)
            o, l = mixer_a_group(zg.reshape(batch, dilation, seq // dilation, A_QKV_W), dilation, 0, 1, 2)
            oa.append(_by_token(o, batch, seq))
            la.append(_by_token(l, batch, seq))
        yb = mixer_b(z, batch, seq, p["logit_fwd"], p["logit_bwd"])
        yc = mixer_c(z, batch, seq, p["rpb"])
        x2 = merge_out_ln(x1, oa, la, yb, yc, z, p["wa"], p["wb"], p["wc"], p["wo"], p["g2"], p["b2"])
        (x,) = ffn_ln(x2, p["wg2"], p["wu2"], p["wd2"], p["g3"], p["b3"])
    return x.reshape(batch, seq, D_MODEL)


def kernel(x_prompt, x_sample, ffn1_w_gate, ffn1_w_up, ffn1_w_down, ln1_g, ln1_b, w_in, ret_logit_fwd, ret_logit_bwd, na_rpb, w_branch_a, w_branch_b, w_branch_c, w_out, ln2_g, ln2_b, ffn2_w_gate, ffn2_w_up, ffn2_w_down, ln3_g, ln3_b):
    def vec(v):
        return v.astype(F32).reshape(1, D_MODEL)

    layers = []
    for i in range(DEPTH):
        w_main, w_groups = _split_in_weights(w_in[i])
        layers.append(dict(
            wg1=ffn1_w_gate[i].astype(BF16), wu1=ffn1_w_up[i].astype(BF16), wd1=ffn1_w_down[i].astype(BF16),
            g1=vec(ln1_g[i]), b1=vec(ln1_b[i]),
            w_main=w_main, w_groups=w_groups,
            logit_fwd=ret_logit_fwd[i], logit_bwd=ret_logit_bwd[i], rpb=na_rpb[i],
            wa=w_branch_a[i].astype(BF16), wb=w_branch_b[i].astype(BF16), wc=w_branch_c[i].astype(BF16),
            wo=w_out[i].astype(BF16), g2=vec(ln2_g[i]), b2=vec(ln2_b[i]),
            wg2=ffn2_w_gate[i].astype(BF16), wu2=ffn2_w_up[i].astype(BF16), wd2=ffn2_w_down[i].astype(BF16),
            g3=vec(ln3_g[i]), b3=vec(ln3_b[i])))
    return (_trunk(x_prompt, layers), _trunk(x_sample, layers))
```

```python
import functools

import jax
import jax.numpy as jnp
from jax import lax
from jax.experimental import pallas as pl
from jax.experimental.pallas import tpu as pltpu

F32 = jnp.float32
BF16 = jnp.bfloat16

D_MODEL = 1024
DEPTH = 2
D_FF = 2816
LN_EPS = 1e-5
GN_EPS = 1e-5
ALPHA = (2 * DEPTH) ** 0.25

A_PATTERNS = ((128, 1), (512, 4), (2048, 16))
A_GROUPS = len(A_PATTERNS)
A_HEADS = 8
A_HD = 64
A_HALF = 64
LSE_LANES = A_HD // (A_HEADS // 2)
B_HEADS = 4
B_DK = 64
B_DV = 128
B_CHUNK = 128
B_BLOCK = 4
C_HEADS = 8
C_HD = 64
GRID_W = 64
C_KH = 8
C_KW = 16
C_QR = 4
D_IN = 10752
NEG = -1e30
LANES = 128
LOG2E = 1.4426950408889634

GATE_W = 3 * D_MODEL
HEAD_BLOCK = 512
A_QKV_W = 3 * HEAD_BLOCK
MAIN_W = D_IN - (A_GROUPS - 1) * A_QKV_W
AQ_BLK, AK_BLK, AV_BLK = 6, 7, 8
BQ_BLK256, BK_BLK256 = 18, 19
BV_BLK, BG_BLK = 10, 11
CQ_BLK, CK_BLK, CV_BLK = 12, 13, 14

VMEM_LIMIT = 56 * 1024 * 1024
FF_CHUNKS = ((0, 512), (512, 1024), (1024, 1536), (1536, 2048), (2048, 2560), (2560, 2816))


def _params(sem):
    return pltpu.CompilerParams(dimension_semantics=sem, vmem_limit_bytes=VMEM_LIMIT)


def _const_spec(shape):
    zeros = (0,) * len(shape)
    return pl.BlockSpec(shape, lambda *_: zeros)


def _layer_norm(r, g, b):
    mu = jnp.mean(r, axis=-1, keepdims=True)
    c = r - mu
    var = jnp.mean(c * c, axis=-1, keepdims=True)
    return c * lax.rsqrt(var + LN_EPS) * g + b


def _ffn_ln_kernel(x_ref, wg_ref, wu_ref, wd_ref, g_ref, b_ref, o_ref, *rest, dilations):
    x = x_ref[...]
    xb = x.astype(BF16)
    acc = None
    for c0, c1 in FF_CHUNKS:
        gate = jnp.dot(xb, wg_ref[:, c0:c1], preferred_element_type=F32)
        up = jnp.dot(xb, wu_ref[:, c0:c1], preferred_element_type=F32)
        h = (gate * jax.nn.sigmoid(gate) * up).astype(BF16)
        part = jnp.dot(h, wd_ref[c0:c1, :], preferred_element_type=F32)
        acc = part if acc is None else acc + part
    y = _layer_norm(ALPHA * x + 0.5 * acc, g_ref[...], b_ref[...])
    o_ref[...] = y
    if not rest:
        return
    ob_ref, *og_refs, slabs = rest
    ob_ref[...] = y.astype(BF16)
    rows = y.shape[0]
    n_slabs = D_MODEL // LANES
    for s in range(n_slabs):
        slabs[s] = y[:, s * LANES:(s + 1) * LANES]
    for d, og_ref in zip(dilations, og_refs):
        for r in range(d):
            picked = [slabs[s, pl.ds(r, rows // d, stride=d), :] for s in range(n_slabs)]
            og_ref[0, r] = jnp.concatenate(picked, axis=1).astype(BF16)


def ffn_ln(x, wg, wu, wd, g, b, batch=None, dilations=()):
    t = x.shape[0]
    tm = 512 if dilations else 1024
    row = pl.BlockSpec((tm, D_MODEL), lambda i: (i, 0))
    out_shape = [jax.ShapeDtypeStruct((t, D_MODEL), F32)]
    out_specs = [row]
    scratch = []
    if dilations:
        tiles_per_seq = t // batch // tm
        out_shape.append(jax.ShapeDtypeStruct((t, D_MODEL), BF16))
        out_specs.append(row)
        for d in dilations:
            out_shape.append(jax.ShapeDtypeStruct((batch, d, t // batch // d, D_MODEL), BF16))
            out_specs.append(pl.BlockSpec((1, d, tm // d, D_MODEL),
                                          lambda i: (i // tiles_per_seq, 0, i % tiles_per_seq, 0)))
        scratch = [pltpu.VMEM((D_MODEL // LANES, tm, LANES), F32)]
    return pl.pallas_call(
        functools.partial(_ffn_ln_kernel, dilations=dilations),
        grid=(t // tm,),
        in_specs=[row, _const_spec((D_MODEL, D_FF)), _const_spec((D_MODEL, D_FF)), _const_spec((D_FF, D_MODEL)),
                  _const_spec((1, D_MODEL)), _const_spec((1, D_MODEL))],
        out_specs=out_specs,
        out_shape=out_shape,
        scratch_shapes=scratch,
        compiler_params=_params(("arbitrary",)),
        name="ffn_ln",
    )(x, wg, wu, wd, g, b)


def _in_proj_kernel(x_ref, w_ref, z_ref):
    z_ref[...] = jnp.dot(x_ref[...], w_ref[...], preferred_element_type=F32).astype(BF16)


def in_proj(xb, w_in):
    t = xb.shape[0]
    width = w_in.shape[1]
    tm = 2048
    tn = 2560 if width % 2560 == 0 else A_QKV_W
    return pl.pallas_call(
        _in_proj_kernel,
        grid=(t // tm, width // tn),
        in_specs=[pl.BlockSpec((tm, D_MODEL), lambda i, j: (i, 0)), pl.BlockSpec((D_MODEL, tn), lambda i, j: (0, j))],
        out_specs=pl.BlockSpec((tm, tn), lambda i, j: (i, j)),
        out_shape=jax.ShapeDtypeStruct((t, width), BF16),
        compiler_params=_params(("arbitrary", "arbitrary")),
        name="in_proj",
    )(xb, w_in)


def _pair_scores(q_pair, k_pair, left):
    zero = jnp.zeros_like(q_pair)
    dims = (((1,), (1,)), ((), ()))
    return [lax.dot_general(jnp.where(left, q_pair, zero), k_pair, dims, preferred_element_type=F32),
            lax.dot_general(jnp.where(left, zero, q_pair), k_pair, dims, preferred_element_type=F32)]


def _pair_weighted_sum(probs, v_pair, left):
    rhs = []
    for hh in range(2):
        sel = left if hh == 0 else jnp.logical_not(left)
        ones = jnp.broadcast_to(jnp.where(sel, 1.0, 0.0).astype(BF16), v_pair.shape)
        rhs.append(jnp.concatenate([jnp.where(sel, v_pair, jnp.zeros_like(v_pair)), ones], axis=1))
    out = jnp.dot(jnp.concatenate(probs, axis=1), jnp.concatenate(rhs, axis=0), preferred_element_type=F32)
    width = v_pair.shape[1]
    return out[:, :width], out[:, width:]


A_QB = 128
A_KB = A_QB + 2 * A_HALF


def _mixer_a_kernel(q_ref, kp_ref, kc_ref, kn_ref, vp_ref, vc_ref, vn_ref, bias_ref, o_ref, l_ref, kf, vf, *,
                    tl, seq_len):
    i = pl.program_id(2)
    kf[0:A_HALF, :] = kp_ref[0, 0]
    kf[A_HALF:A_HALF + tl, :] = kc_ref[0, 0]
    kf[A_HALF + tl:, :] = kn_ref[0, 0]
    vf[0:A_HALF, :] = vp_ref[0, 0]
    vf[A_HALF:A_HALF + tl, :] = vc_ref[0, 0]
    vf[A_HALF + tl:, :] = vn_ref[0, 0]

    lane = lax.broadcasted_iota(jnp.int32, (1, 2 * A_HD), 1)
    left = lane < A_HD
    lse_slot = (lane % A_HD) // LSE_LANES

    def block(j, carry):
        q0 = pl.multiple_of(j * A_QB, A_QB)
        start = i * tl + q0
        variant = (start == 0).astype(jnp.int32) + 2 * (start + A_QB == seq_len).astype(jnp.int32)
        lse_all = None
        for hp in range(A_HEADS // 2):
            cols = slice(hp * 2 * A_HD, (hp + 1) * 2 * A_HD)
            qp = q_ref[0, 0, pl.ds(q0, A_QB), cols]
            kp = kf[pl.ds(q0, A_KB), cols]
            probs, tops = [], []
            for hh, s in enumerate(_pair_scores(qp, kp, left)):
                s = s + bias_ref[variant, hp * 2 + hh]
                m = jnp.max(s, axis=-1, keepdims=True)
                probs.append(jnp.exp2(s - m).astype(BF16))
                tops.append(m)
            num, den = _pair_weighted_sum(probs, vf[pl.ds(q0, A_KB), cols], left)
            o_ref[0, 0, pl.ds(q0, A_QB), cols] = (num / den).astype(BF16)
            lse_pair = jnp.where(left, tops[0], tops[1]) + jnp.log2(den)
            lse_all = lse_pair if hp == 0 else jnp.where(lse_slot == hp, lse_pair, lse_all)
        l_ref[0, 0, pl.ds(q0, A_QB), :] = lse_all
        return carry

    lax.fori_loop(0, tl // A_QB, block, 0, unroll=True)


def _alibi_bias(dilation):
    slopes = 2.0 ** (-8.0 * jnp.arange(1, A_HEADS + 1, dtype=F32) / A_HEADS)
    key = jnp.arange(A_KB)[None, :] - A_HALF
    rel = key - jnp.arange(A_QB)[:, None]
    dist = (jnp.abs(rel) * dilation).astype(F32)
    bias = -slopes[:, None, None] * dist[None] * LOG2E
    in_window = jnp.abs(rel) <= A_HALF
    variants = []
    for v in range(4):
        ok = in_window
        if v & 1:
            ok = ok & (key >= 0)
        if v & 2:
            ok = ok & (key < A_QB)
        variants.append(jnp.where(ok[None], bias, NEG))
    return jnp.stack(variants, 0)


def mixer_a_group(zg, dilation, q_blk, k_blk, v_blk):
    batch, _, strided_len, _ = zg.shape
    tl = min(1024, strided_len)
    halo_per_tile = tl // A_HALF
    n_halo = strided_len // A_HALF

    def cur(c):
        return lambda b, r, i: (b, r, i, c)

    def before(c):
        return lambda b, r, i: (b, r, jnp.maximum(i * halo_per_tile - 1, 0), c)

    def after(c):
        return lambda b, r, i: (b, r, jnp.minimum((i + 1) * halo_per_tile, n_halo - 1), c)

    tile = (1, 1, tl, HEAD_BLOCK)
    halo = (1, 1, A_HALF, HEAD_BLOCK)
    out_spec = pl.BlockSpec(tile, cur(0))
    out_dims = (batch, dilation, strided_len, HEAD_BLOCK)
    return pl.pallas_call(
        functools.partial(_mixer_a_kernel, tl=tl, seq_len=strided_len),
        grid=(batch, dilation, strided_len // tl),
        in_specs=[pl.BlockSpec(tile, cur(q_blk)),
                  pl.BlockSpec(halo, before(k_blk)), pl.BlockSpec(tile, cur(k_blk)), pl.BlockSpec(halo, after(k_blk)),
                  pl.BlockSpec(halo, before(v_blk)), pl.BlockSpec(tile, cur(v_blk)), pl.BlockSpec(halo, after(v_blk)),
                  _const_spec((4, A_HEADS, A_QB, A_KB))],
        out_specs=[out_spec, pl.BlockSpec((1, 1, tl, 2 * A_HD), cur(0))],
        out_shape=[jax.ShapeDtypeStruct(out_dims, BF16),
                   jax.ShapeDtypeStruct((batch, dilation, strided_len, 2 * A_HD), F32)],
        scratch_shapes=[pltpu.VMEM((tl + 2 * A_HALF, HEAD_BLOCK), BF16), pltpu.VMEM((tl + 2 * A_HALF, HEAD_BLOCK), BF16)],
        compiler_params=_params(("arbitrary", "arbitrary", "arbitrary")),
        name=f"mixer_a_d{dilation}",
    )(zg, zg, zg, zg, zg, zg, zg, _alibi_bias(dilation))


def _by_token(x, batch, seq):
    return x.transpose(0, 2, 1, 3).reshape(batch * seq, x.shape[-1])


def _mixer_b_kernel(q_ref, k_ref, v_ref, g_ref, dmat_ref, qdf_ref, qdb_ref, kdf_ref, kdb_ref, cdf_ref, cdb_ref,
                    o_ref, fwd_state, bwd_state, bwd_store, *, n_blocks):
    phase = pl.program_id(1)
    n = pl.program_id(2)
    qk_w = B_HEADS * B_DK
    head_of_lane = lax.broadcasted_iota(jnp.int32, (1, qk_w), 1) // B_DK

    def rows(c):
        return slice(c * B_CHUNK, (c + 1) * B_CHUNK)

    def stacked_heads(t):
        lane_head = jnp.concatenate([head_of_lane] * (t.shape[1] // qk_w), axis=1)
        return jnp.concatenate([jnp.where(lane_head == h, t, jnp.zeros_like(t)) for h in range(B_HEADS)], axis=0)

    def kv_outer(k_decayed, v):
        full = lax.dot_general(k_decayed, v, (((0,), (0,)), ((), ())), preferred_element_type=F32)
        return jnp.concatenate([full[h * B_DK:(h + 1) * B_DK, h * B_DV:(h + 1) * B_DV] for h in range(B_HEADS)], axis=0)

    def scaled_k(c):
        return k_ref[0, rows(c), :] * (B_DK ** -0.5)

    @pl.when(phase == 0)
    def _():
        @pl.when(n == 0)
        def _():
            bwd_state[...] = jnp.zeros_like(bwd_state)

        blk = n_blocks - 1 - n
        state = bwd_state[...]
        for c in reversed(range(B_BLOCK)):
            bwd_store[blk * B_BLOCK + c] = state.astype(BF16)
            k_dec = (scaled_k(c).astype(F32) * kdb_ref[...]).astype(BF16)
            state = cdb_ref[...] * state + kv_outer(k_dec, v_ref[0, rows(c), :])
        bwd_state[...] = state

    @pl.when(phase == 1)
    def _():
        @pl.when(n == 0)
        def _():
            fwd_state[...] = jnp.zeros_like(fwd_state)

        state = fwd_state[...]
        for c in range(B_BLOCK):
            q = q_ref[0, rows(c), :]
            k = scaled_k(c)
            v = v_ref[0, rows(c), :]
            q32 = q.astype(F32)
            q_dec = jnp.concatenate([(q32 * qdf_ref[...]).astype(BF16), (q32 * qdb_ref[...]).astype(BF16)], axis=1)
            k_dec = (k.astype(F32) * kdf_ref[...]).astype(BF16)
            states = jnp.concatenate([state.astype(BF16), bwd_store[n * B_BLOCK + c]], axis=0)
            s_all = lax.dot_general(stacked_heads(q), k, (((1,), (1,)), ((), ())), preferred_element_type=F32)
            cross_all = jnp.dot(stacked_heads(q_dec), states, preferred_element_type=F32)
            for h in range(B_HEADS):
                vh = v[:, h * B_DV:(h + 1) * B_DV]
                inner = jnp.dot((s_all[rows(h), :] * dmat_ref[h]).astype(BF16), vh, preferred_element_type=F32)
                y = inner + cross_all[rows(h), :]
                mu = jnp.mean(y, axis=-1, keepdims=True)
                cen = y - mu
                var = jnp.mean(cen * cen, axis=-1, keepdims=True)
                yn = cen * lax.rsqrt(var + GN_EPS)
                gate = g_ref[0, rows(c), h * B_DV:(h + 1) * B_DV].astype(F32)
                o_ref[0, rows(c), h * B_DV:(h + 1) * B_DV] = (gate * jax.nn.sigmoid(gate) * yn).astype(BF16)
            state = cdf_ref[...] * state + kv_outer(k_dec, v)
        fwd_state[...] = state


def _retention_tables(logit_fwd, logit_bwd):
    lg_f = jax.nn.log_sigmoid(logit_fwd.astype(F32))
    lg_b = jax.nn.log_sigmoid(logit_bwd.astype(F32))
    idx = jnp.arange(B_CHUNK, dtype=F32)
    diff = idx[:, None] - idx[None, :]
    causal = diff >= 0
    dmat = jnp.where(causal[None],
                     jnp.exp(lg_f[:, None, None] * jnp.where(causal, diff, 0.0)[None]),
                     jnp.exp(lg_b[:, None, None] * jnp.where(causal, 0.0, -diff)[None]))

    def per_lane(lg, power):
        return jnp.repeat(jnp.exp(lg[None, :] * power[:, None]), B_DK, axis=1)

    def per_row(lg):
        return jnp.broadcast_to(jnp.repeat(jnp.exp(lg * B_CHUNK), B_DK)[:, None], (B_HEADS * B_DK, B_DV))

    return (dmat, per_lane(lg_f, idx + 1), per_lane(lg_b, B_CHUNK - idx), per_lane(lg_f, B_CHUNK - 1 - idx),
            per_lane(lg_b, idx), per_row(lg_f), per_row(lg_b))


def mixer_b(z, batch, seq, logit_fwd, logit_bwd):
    n_chunks = seq // B_CHUNK
    n_blocks = n_chunks // B_BLOCK
    block_rows = B_BLOCK * B_CHUNK
    zv = z.reshape(batch, seq, MAIN_W)
    qk_w = B_HEADS * B_DK
    v_w = B_HEADS * B_DV

    def scan_block(ph, n):
        return (1 - ph) * (n_blocks - 1 - n) + ph * n

    tables = _retention_tables(logit_fwd, logit_bwd)
    in_specs = [pl.BlockSpec((1, block_rows, qk_w), lambda b, ph, n: (b, ph * n, BQ_BLK256)),
                pl.BlockSpec((1, block_rows, qk_w), lambda b, ph, n: (b, scan_block(ph, n), BK_BLK256)),
                pl.BlockSpec((1, block_rows, v_w), lambda b, ph, n: (b, scan_block(ph, n), BV_BLK)),
                pl.BlockSpec((1, block_rows, v_w), lambda b, ph, n: (b, ph * n, BG_BLK))]
    in_specs += [_const_spec(t.shape) for t in tables]
    out = pl.pallas_call(
        functools.partial(_mixer_b_kernel, n_blocks=n_blocks),
        grid=(batch, 2, n_blocks),
        in_specs=in_specs,
        out_specs=pl.BlockSpec((1, block_rows, v_w), lambda b, ph, n: (b, ph * n, 0)),
        out_shape=jax.ShapeDtypeStruct((batch, seq, v_w), BF16),
        scratch_shapes=[pltpu.VMEM((qk_w, B_DV), F32), pltpu.VMEM((qk_w, B_DV), F32),
                        pltpu.VMEM((n_chunks, qk_w, B_DV), BF16)],
        compiler_params=_params(("arbitrary", "arbitrary", "arbitrary")),
        name="mixer_b",
    )(zv, zv, zv, zv, *tables)
    return out.reshape(batch * seq, v_w)


C_TQ = C_QR * GRID_W
C_KROWS = 3 * C_QR
C_TK = C_KROWS * GRID_W
C_PAIRS = C_KROWS // 2
C_NTAB = 2 * C_KH - 2


def _mixer_c_kernel(q_ref, kp_ref, kc_ref, kn_ref, vp_ref, vc_ref, vn_ref, tab_ref, o_ref, kf, vf, *, rows):
    blk = pl.program_id(1)
    kf[0:C_TQ, :] = kp_ref[0]
    kf[C_TQ:2 * C_TQ, :] = kc_ref[0]
    kf[2 * C_TQ:, :] = kn_ref[0]
    vf[0:C_TQ, :] = vp_ref[0]
    vf[C_TQ:2 * C_TQ, :] = vc_ref[0]
    vf[2 * C_TQ:, :] = vn_ref[0]

    left = lax.broadcasted_iota(jnp.int32, (1, 2 * C_HD), 1) < C_HD
    key_row = blk * C_QR - C_QR + lax.broadcasted_iota(jnp.int32, (1, C_TK), 1) // GRID_W
    pens = []
    for a in range(C_QR):
        row_start = jnp.clip(blk * C_QR + a - C_KH // 2, 0, rows - C_KH)
        pens.append(jnp.where((key_row >= row_start) & (key_row < row_start + C_KH), 0.0, NEG).astype(F32))

    for hp in range(C_HEADS // 2):
        cols = slice(hp * 2 * C_HD, (hp + 1) * 2 * C_HD)
        qp = q_ref[0, :, cols]
        kp = kf[:, cols]
        probs = []
        for hh, s in enumerate(_pair_scores(qp, kp, left)):
            h = hp * 2 + hh
            rows_p = []
            for a in range(C_QR):
                bias = jnp.concatenate([tab_ref[h, 2 * t - C_QR - a + C_KH - 1] for t in range(C_PAIRS)], axis=1)
                sa = s[a * GRID_W:(a + 1) * GRID_W, :] + bias + pens[a]
                m = jnp.max(sa, axis=-1, keepdims=True)
                rows_p.append(jnp.exp2(sa - m).astype(BF16))
            probs.append(jnp.concatenate(rows_p, axis=0))
        num, den = _pair_weighted_sum(probs, vf[:, cols], left)
        o_ref[0, :, cols] = (num / den).astype(BF16)


def _neighbourhood_bias(rpb):
    qc = jnp.arange(GRID_W)[:, None]
    kc = jnp.arange(GRID_W)[None, :]
    col_start = jnp.clip(qc - C_KW // 2, 0, GRID_W - C_KW)
    col_ok = (kc >= col_start) & (kc < col_start + C_KW)
    onehot = ((kc - qc + (C_KW - 1))[:, :, None] == jnp.arange(2 * C_KW - 1)[None, None, :]).astype(F32)
    band = jnp.einsum("hrd,qkd->hrqk", rpb.astype(F32), onehot, precision=lax.Precision.HIGHEST)
    band = jnp.where(col_ok[None, None], band * LOG2E, NEG)
    return jnp.concatenate([band[:, :C_NTAB], band[:, 1:C_NTAB + 1]], axis=-1)


def mixer_c(z, batch, seq, rpb):
    rows = seq // GRID_W
    n_blk = rows // C_QR
    zv = z.reshape(batch, seq, MAIN_W)
    tile = (1, C_TQ, HEAD_BLOCK)
    tab = _neighbourhood_bias(rpb)

    def above(c):
        return lambda b, i: (b, jnp.maximum(i - 1, 0), c)

    def here(c):
        return lambda b, i: (b, i, c)

    def below(c):
        return lambda b, i: (b, jnp.minimum(i + 1, n_blk - 1), c)

    out = pl.pallas_call(
        functools.partial(_mixer_c_kernel, rows=rows),
        grid=(batch, n_blk),
        in_specs=[pl.BlockSpec(tile, here(CQ_BLK)),
                  pl.BlockSpec(tile, above(CK_BLK)), pl.BlockSpec(tile, here(CK_BLK)), pl.BlockSpec(tile, below(CK_BLK)),
                  pl.BlockSpec(tile, above(CV_BLK)), pl.BlockSpec(tile, here(CV_BLK)), pl.BlockSpec(tile, below(CV_BLK)),
                  _const_spec(tab.shape)],
        out_specs=pl.BlockSpec(tile, lambda b, i: (b, i, 0)),
        out_shape=jax.ShapeDtypeStruct((batch, seq, HEAD_BLOCK), BF16),
        scratch_shapes=[pltpu.VMEM((C_TK, HEAD_BLOCK), BF16), pltpu.VMEM((C_TK, HEAD_BLOCK), BF16)],
        compiler_params=_params(("arbitrary", "arbitrary")),
        name="mixer_c",
    )(zv, zv, zv, zv, zv, zv, zv, tab)
    return out.reshape(batch * seq, HEAD_BLOCK)


def _merge_kernel(x_ref, oa0_ref, oa1_ref, oa2_ref, la0_ref, la1_ref, la2_ref, yb_ref, yc_ref, gates_ref, spread_ref,
                  wa_ref, wb_ref, wc_ref, wo_ref, g_ref, b_ref, o_ref):
    lses = [la0_ref[...], la1_ref[...], la2_ref[...]]
    top = jnp.maximum(jnp.maximum(lses[0], lses[1]), lses[2])
    weights = [jnp.exp2(lse - top) for lse in lses]
    total = weights[0] + weights[1] + weights[2]
    ya = None
    for o_g_ref, w in zip((oa0_ref, oa1_ref, oa2_ref), weights):
        share = w / total
        hi = share.astype(BF16)
        lo = (share - hi.astype(F32)).astype(BF16)
        wide = jnp.dot(jnp.concatenate([hi, lo], axis=1), spread_ref[...], preferred_element_type=F32)
        term = wide * o_g_ref[...].astype(F32)
        ya = term if ya is None else ya + term
    ya = ya.astype(BF16)

    merged = None
    for br, (y, w_ref) in enumerate(((ya, wa_ref), (yb_ref[...], wb_ref), (yc_ref[...], wc_ref))):
        proj = jnp.dot(y, w_ref[...], preferred_element_type=F32)
        gate = jax.nn.sigmoid(gates_ref[:, br * D_MODEL:(br + 1) * D_MODEL].astype(F32))
        merged = gate * proj if merged is None else merged + gate * proj
    out = jnp.dot(merged.astype(BF16), wo_ref[...], preferred_element_type=F32)
    o_ref[...] = _layer_norm(ALPHA * x_ref[...] + out, g_ref[...], b_ref[...])


def _lse_spread():
    src = jnp.arange(2 * A_HD)[:, None]
    head = jnp.arange(HEAD_BLOCK)[None, :] // A_HD
    first_lane = (head % 2) * A_HD + (head // 2) * LSE_LANES
    once = (src == first_lane).astype(BF16)
    return jnp.concatenate([once, once], axis=0)


def merge_out_ln(x, oa, la, yb, yc, z, wa, wb, wc, wo, g, b):
    t = x.shape[0]
    tm = 1024
    row = pl.BlockSpec((tm, D_MODEL), lambda i: (i, 0))
    br = pl.BlockSpec((tm, HEAD_BLOCK), lambda i: (i, 0))
    lse = pl.BlockSpec((tm, 2 * A_HD), lambda i: (i, 0))
    w_br = _const_spec((HEAD_BLOCK, D_MODEL))
    return pl.pallas_call(
        _merge_kernel,
        grid=(t // tm,),
        in_specs=[row] + [br] * 3 + [lse] * 3 + [br] * 2 + [
            pl.BlockSpec((tm, GATE_W), lambda i: (i, 0)), _const_spec((4 * A_HD, HEAD_BLOCK)), w_br, w_br, w_br,
            _const_spec((D_MODEL, D_MODEL)), _const_spec((1, D_MODEL)), _const_spec((1, D_MODEL))],
        out_specs=row,
        out_shape=jax.ShapeDtypeStruct((t, D_MODEL), F32),
        compiler_params=_params(("arbitrary",)),
        name="merge_out_ln",
    )(x, *oa, *la, yb, yc, z, _lse_spread(), wa, wb, wc, wo, g, b)


def _split_in_weights(w_in):
    a_w = A_GROUPS * HEAD_BLOCK
    aq, ak, av = w_in[:, :a_w] * (A_HD ** -0.5 * LOG2E), w_in[:, a_w:2 * a_w], w_in[:, 2 * a_w:3 * a_w]
    b_w = 2 * B_HEADS * (B_DK + B_DV)
    rest_b = w_in[:, 3 * a_w:3 * a_w + b_w]
    cq = w_in[:, 3 * a_w + b_w:3 * a_w + b_w + HEAD_BLOCK] * (C_HD ** -0.5 * LOG2E)
    rest = jnp.concatenate([rest_b, cq, w_in[:, 3 * a_w + b_w + HEAD_BLOCK:D_IN - GATE_W]], axis=1)

    def group(g):
        cols = slice(g * HEAD_BLOCK, (g + 1) * HEAD_BLOCK)
        return jnp.concatenate([aq[:, cols], ak[:, cols], av[:, cols]], axis=1)

    main = jnp.concatenate([w_in[:, D_IN - GATE_W:], group(0), rest], axis=1)
    return main.astype(BF16), [group(g).astype(BF16) for g in range(1, A_GROUPS)]


def _trunk(x, layers):
    batch, seq, _ = x.shape
    x = x.reshape(batch * seq, D_MODEL)
    for p in layers:
        dils = tuple(d for _, d in A_PATTERNS[1:])
        x1, x1b, *x1_by_residue = ffn_ln(x, p["wg1"], p["wu1"], p["wd1"], p["g1"], p["b1"], batch, dils)
        z = in_proj(x1b, p["w_main"])
        oa, la = [], []
        o, l = mixer_a_group(z.reshape(batch, 1, seq, MAIN_W), 1, AQ_BLK, AK_BLK, AV_BLK)
        oa.append(o.reshape(batch * seq, HEAD_BLOCK))
        la.append(l.reshape(batch * seq, 2 * A_HD))
        for dilation, xg, w_g in zip(dils, x1_by_residue, p["w_groups"]):
            zg = in_proj(xg.reshape(batch * seq, D_MODEL), w_g)
            o, l = mixer_a_group(zg.reshape(batch, dilation, seq // dilation, A_QKV_W), dilation, 0, 1, 2)
            oa.append(_by_token(o, batch, seq))
            la.append(_by_token(l, batch, seq))
        yb = mixer_b(z, batch, seq, p["logit_fwd"], p["logit_bwd"])
        yc = mixer_c(z, batch, seq, p["rpb"])
        x2 = merge_out_ln(x1, oa, la, yb, yc, z, p["wa"], p["wb"], p["wc"], p["wo"], p["g2"], p["b2"])
        (x,) = ffn_ln(x2, p["wg2"], p["wu2"], p["wd2"], p["g3"], p["b3"])
    return x.reshape(batch, seq, D_MODEL)


def kernel(x_prompt, x_sample, ffn1_w_gate, ffn1_w_up, ffn1_w_down, ln1_g, ln1_b, w_in, ret_logit_fwd, ret_logit_bwd, na_rpb, w_branch_a, w_branch_b, w_branch_c, w_out, ln2_g, ln2_b, ffn2_w_gate, ffn2_w_up, ffn2_w_down, ln3_g, ln3_b):
    def vec(v):
        return v.astype(F32).reshape(1, D_MODEL)

    layers = []
    for i in range(DEPTH):
        w_main, w_groups = _split_in_weights(w_in[i])
        layers.append(dict(
            wg1=ffn1_w_gate[i].astype(BF16), wu1=ffn1_w_up[i].astype(BF16), wd1=ffn1_w_down[i].astype(BF16),
            g1=vec(ln1_g[i]), b1=vec(ln1_b[i]),
            w_main=w_main, w_groups=w_groups,
            logit_fwd=ret_logit_fwd[i], logit_bwd=ret_logit_bwd[i], rpb=na_rpb[i],
            wa=w_branch_a[i].astype(BF16), wb=w_branch_b[i].astype(BF16), wc=w_branch_c[i].astype(BF16),
            wo=w_out[i].astype(BF16), g2=vec(ln2_g[i]), b2=vec(ln2_b[i]),
            wg2=ffn2_w_gate[i].astype(BF16), wu2=ffn2_w_up[i].astype(BF16), wd2=ffn2_w_down[i].astype(BF16),
            g3=vec(ln3_g[i]), b3=vec(ln3_b[i])))
    return (_trunk(x_prompt, layers), _trunk(x_sample, layers))
```

```python
import functools

import jax
import jax.numpy as jnp
from jax import lax
from jax.experimental import pallas as pl
from jax.experimental.pallas import tpu as pltpu

F32 = jnp.float32
BF16 = jnp.bfloat16

D_MODEL = 1024
DEPTH = 2
D_FF = 2816
LN_EPS = 1e-5
GN_EPS = 1e-5
ALPHA = (2 * DEPTH) ** 0.25

A_PATTERNS = ((128, 1), (512, 4), (2048, 16))
A_GROUPS = len(A_PATTERNS)
A_HEADS = 8
A_HD = 64
A_HALF = 64
LSE_LANES = A_HD // (A_HEADS // 2)
B_HEADS = 4
B_DK = 64
B_DV = 128
B_CHUNK = 128
B_BLOCK = 8
C_HEADS = 8
C_HD = 64
GRID_W = 64
C_KH = 8
C_KW = 16
C_QR = 4
C_STEP = 2
D_IN = 10752
NEG = -1e30
LANES = 128
LOG2E = 1.4426950408889634

GATE_W = 3 * D_MODEL
HEAD_BLOCK = 512
A_QKV_W = 3 * HEAD_BLOCK
MAIN_W = D_IN - (A_GROUPS - 1) * A_QKV_W
AQ_BLK, AK_BLK, AV_BLK = 6, 7, 8
BQ_BLK256, BK_BLK256 = 18, 19
BV_BLK, BG_BLK = 10, 11
CQ_BLK, CK_BLK, CV_BLK = 12, 13, 14

VMEM_LIMIT = 56 * 1024 * 1024
FF_CHUNKS = ((0, 512), (512, 1024), (1024, 1536), (1536, 2048), (2048, 2560), (2560, 2816))


def _params(sem):
    return pltpu.CompilerParams(dimension_semantics=sem, vmem_limit_bytes=VMEM_LIMIT)


def _const_spec(shape):
    zeros = (0,) * len(shape)
    return pl.BlockSpec(shape, lambda *_: zeros)


def _layer_norm(r, g, b):
    mu = jnp.mean(r, axis=-1, keepdims=True)
    c = r - mu
    var = jnp.mean(c * c, axis=-1, keepdims=True)
    return c * lax.rsqrt(var + LN_EPS) * g + b


def _ffn_ln_kernel(x_ref, wg_ref, wu_ref, wd_ref, g_ref, b_ref, o_ref, *rest, dilations):
    x = x_ref[...]
    xb = x.astype(BF16)
    acc = None
    for c0, c1 in FF_CHUNKS:
        gate = jnp.dot(xb, wg_ref[:, c0:c1], preferred_element_type=F32)
        up = jnp.dot(xb, wu_ref[:, c0:c1], preferred_element_type=F32)
        h = (gate * jax.nn.sigmoid(gate) * up).astype(BF16)
        part = jnp.dot(h, wd_ref[c0:c1, :], preferred_element_type=F32)
        acc = part if acc is None else acc + part
    y = _layer_norm(ALPHA * x + 0.5 * acc, g_ref[...], b_ref[...])
    o_ref[...] = y
    if not rest:
        return
    ob_ref, *og_refs, slabs = rest
    ob_ref[...] = y.astype(BF16)
    rows = y.shape[0]
    n_slabs = D_MODEL // LANES
    done = 1
    ordered = y
    for d, og_ref in zip(dilations, og_refs):
        for s in range(n_slabs):
            slabs[s] = ordered[:, s * LANES:(s + 1) * LANES]
        step = d // done
        group = rows // done
        pieces = []
        for r in range(d):
            first = (r % done) * group + r // done
            picked = [slabs[s, pl.ds(first, rows // d, stride=step), :] for s in range(n_slabs)]
            pieces.append(jnp.concatenate(picked, axis=1))
            og_ref[0, r] = pieces[-1].astype(BF16)
        ordered = jnp.concatenate(pieces, axis=0)
        done = d


def ffn_ln(x, wg, wu, wd, g, b, batch=None, dilations=()):
    t = x.shape[0]
    tm = 512 if dilations else 1024
    row = pl.BlockSpec((tm, D_MODEL), lambda i: (i, 0))
    out_shape = [jax.ShapeDtypeStruct((t, D_MODEL), F32)]
    out_specs = [row]
    scratch = []
    if dilations:
        assert all(b % a == 0 for a, b in zip((1,) + dilations, dilations)), dilations
        tiles_per_seq = t // batch // tm
        out_shape.append(jax.ShapeDtypeStruct((t, D_MODEL), BF16))
        out_specs.append(row)
        for d in dilations:
            out_shape.append(jax.ShapeDtypeStruct((batch, d, t // batch // d, D_MODEL), BF16))
            out_specs.append(pl.BlockSpec((1, d, tm // d, D_MODEL),
                                          lambda i: (i // tiles_per_seq, 0, i % tiles_per_seq, 0)))
        scratch = [pltpu.VMEM((D_MODEL // LANES, tm, LANES), F32)]
    return pl.pallas_call(
        functools.partial(_ffn_ln_kernel, dilations=dilations),
        grid=(t // tm,),
        in_specs=[row, _const_spec((D_MODEL, D_FF)), _const_spec((D_MODEL, D_FF)), _const_spec((D_FF, D_MODEL)),
                  _const_spec((1, D_MODEL)), _const_spec((1, D_MODEL))],
        out_specs=out_specs,
        out_shape=out_shape,
        scratch_shapes=scratch,
        compiler_params=_params(("arbitrary",)),
        name="ffn_ln",
    )(x, wg, wu, wd, g, b)


def _in_proj_kernel(x_ref, w_ref, z_ref):
    z_ref[...] = jnp.dot(x_ref[...], w_ref[...], preferred_element_type=F32).astype(BF16)


def in_proj(xb, w_in):
    t = xb.shape[0]
    width = w_in.shape[1]
    tm = 2048
    tn = 2560 if width % 2560 == 0 else A_QKV_W
    return pl.pallas_call(
        _in_proj_kernel,
        grid=(t // tm, width // tn),
        in_specs=[pl.BlockSpec((tm, D_MODEL), lambda i, j: (i, 0)), pl.BlockSpec((D_MODEL, tn), lambda i, j: (0, j))],
        out_specs=pl.BlockSpec((tm, tn), lambda i, j: (i, j)),
        out_shape=jax.ShapeDtypeStruct((t, width), BF16),
        compiler_params=_params(("arbitrary", "arbitrary")),
        name="in_proj",
    )(xb, w_in)


def _pair_scores(q_pair, k_pair, left):
    zero = jnp.zeros_like(q_pair)
    dims = (((1,), (1,)), ((), ()))
    return [lax.dot_general(jnp.where(left, q_pair, zero), k_pair, dims, preferred_element_type=F32),
            lax.dot_general(jnp.where(left, zero, q_pair), k_pair, dims, preferred_element_type=F32)]


def _pair_weighted_sum(probs, v_pair, left):
    rhs = []
    for hh in range(2):
        sel = left if hh == 0 else jnp.logical_not(left)
        ones = jnp.broadcast_to(jnp.where(sel, 1.0, 0.0).astype(BF16), v_pair.shape)
        rhs.append(jnp.concatenate([jnp.where(sel, v_pair, jnp.zeros_like(v_pair)), ones], axis=1))
    out = jnp.dot(jnp.concatenate(probs, axis=1), jnp.concatenate(rhs, axis=0), preferred_element_type=F32)
    width = v_pair.shape[1]
    return out[:, :width], out[:, width:]


A_QB = 128
A_KB = A_QB + 2 * A_HALF


def _mixer_a_kernel(q_ref, kp_ref, kc_ref, kn_ref, vp_ref, vc_ref, vn_ref, bias_ref, o_ref, l_ref, kf, vf, *,
                    tl, seq_len):
    i = pl.program_id(2)
    kf[0:A_HALF, :] = kp_ref[0, 0]
    kf[A_HALF:A_HALF + tl, :] = kc_ref[0, 0]
    kf[A_HALF + tl:, :] = kn_ref[0, 0]
    vf[0:A_HALF, :] = vp_ref[0, 0]
    vf[A_HALF:A_HALF + tl, :] = vc_ref[0, 0]
    vf[A_HALF + tl:, :] = vn_ref[0, 0]

    lane = lax.broadcasted_iota(jnp.int32, (1, 2 * A_HD), 1)
    left = lane < A_HD
    lse_slot = (lane % A_HD) // LSE_LANES

    def block(j, carry):
        q0 = pl.multiple_of(j * A_QB, A_QB)
        start = i * tl + q0
        variant = (start == 0).astype(jnp.int32) + 2 * (start + A_QB == seq_len).astype(jnp.int32)
        lse_all = None
        for hp in range(A_HEADS // 2):
            cols = slice(hp * 2 * A_HD, (hp + 1) * 2 * A_HD)
            qp = q_ref[0, 0, pl.ds(q0, A_QB), cols]
            kp = kf[pl.ds(q0, A_KB), cols]
            probs, tops = [], []
            for hh, s in enumerate(_pair_scores(qp, kp, left)):
                s = s + bias_ref[variant, hp * 2 + hh]
                m = jnp.max(s, axis=-1, keepdims=True)
                probs.append(jnp.exp2(s - m).astype(BF16))
                tops.append(m)
            num, den = _pair_weighted_sum(probs, vf[pl.ds(q0, A_KB), cols], left)
            o_ref[0, 0, pl.ds(q0, A_QB), cols] = (num / den).astype(BF16)
            lse_pair = jnp.where(left, tops[0], tops[1]) + jnp.log2(den)
            lse_all = lse_pair if hp == 0 else jnp.where(lse_slot == hp, lse_pair, lse_all)
        l_ref[0, 0, pl.ds(q0, A_QB), :] = lse_all
        return carry

    lax.fori_loop(0, tl // A_QB, block, 0, unroll=True)


def _alibi_bias(dilation):
    slopes = 2.0 ** (-8.0 * jnp.arange(1, A_HEADS + 1, dtype=F32) / A_HEADS)
    key = jnp.arange(A_KB)[None, :] - A_HALF
    rel = key - jnp.arange(A_QB)[:, None]
    dist = (jnp.abs(rel) * dilation).astype(F32)
    bias = -slopes[:, None, None] * dist[None] * LOG2E
    in_window = jnp.abs(rel) <= A_HALF
    variants = []
    for v in range(4):
        ok = in_window
        if v & 1:
            ok = ok & (key >= 0)
        if v & 2:
            ok = ok & (key < A_QB)
        variants.append(jnp.where(ok[None], bias, NEG))
    return jnp.stack(variants, 0)


def mixer_a_group(zg, dilation, q_blk, k_blk, v_blk):
    batch, _, strided_len, _ = zg.shape
    tl = min(1024, strided_len)
    halo_per_tile = tl // A_HALF
    n_halo = strided_len // A_HALF

    def cur(c):
        return lambda b, r, i: (b, r, i, c)

    def before(c):
        return lambda b, r, i: (b, r, jnp.maximum(i * halo_per_tile - 1, 0), c)

    def after(c):
        return lambda b, r, i: (b, r, jnp.minimum((i + 1) * halo_per_tile, n_halo - 1), c)

    tile = (1, 1, tl, HEAD_BLOCK)
    halo = (1, 1, A_HALF, HEAD_BLOCK)
    out_spec = pl.BlockSpec(tile, cur(0))
    out_dims = (batch, dilation, strided_len, HEAD_BLOCK)
    return pl.pallas_call(
        functools.partial(_mixer_a_kernel, tl=tl, seq_len=strided_len),
        grid=(batch, dilation, strided_len // tl),
        in_specs=[pl.BlockSpec(tile, cur(q_blk)),
                  pl.BlockSpec(halo, before(k_blk)), pl.BlockSpec(tile, cur(k_blk)), pl.BlockSpec(halo, after(k_blk)),
                  pl.BlockSpec(halo, before(v_blk)), pl.BlockSpec(tile, cur(v_blk)), pl.BlockSpec(halo, after(v_blk)),
                  _const_spec((4, A_HEADS, A_QB, A_KB))],
        out_specs=[out_spec, pl.BlockSpec((1, 1, tl, 2 * A_HD), cur(0))],
        out_shape=[jax.ShapeDtypeStruct(out_dims, BF16),
                   jax.ShapeDtypeStruct((batch, dilation, strided_len, 2 * A_HD), F32)],
        scratch_shapes=[pltpu.VMEM((tl + 2 * A_HALF, HEAD_BLOCK), BF16), pltpu.VMEM((tl + 2 * A_HALF, HEAD_BLOCK), BF16)],
        compiler_params=_params(("arbitrary", "arbitrary", "arbitrary")),
        name=f"mixer_a_d{dilation}",
    )(zg, zg, zg, zg, zg, zg, zg, _alibi_bias(dilation))


def _by_token(x, batch, seq):
    return x.transpose(0, 2, 1, 3).reshape(batch * seq, x.shape[-1])


def _mixer_b_kernel(q_ref, k_ref, v_ref, g_ref, dmat_ref, qdf_ref, qdb_ref, kdf_ref, kdb_ref, cdf_ref, cdb_ref,
                    o_ref, fwd_state, bwd_state, bwd_store, *, n_blocks):
    phase = pl.program_id(1)
    n = pl.program_id(2)
    qk_w = B_HEADS * B_DK
    head_of_lane = lax.broadcasted_iota(jnp.int32, (1, qk_w), 1) // B_DK

    def rows(c):
        return slice(c * B_CHUNK, (c + 1) * B_CHUNK)

    def stacked_heads(t):
        lane_head = jnp.concatenate([head_of_lane] * (t.shape[1] // qk_w), axis=1)
        return jnp.concatenate([jnp.where(lane_head == h, t, jnp.zeros_like(t)) for h in range(B_HEADS)], axis=0)

    def head_block_diagonal(states):
        row_head = (lax.broadcasted_iota(jnp.int32, (states.shape[0], 1), 0) % qk_w) // B_DK
        return jnp.concatenate([jnp.where(row_head == h, states, jnp.zeros_like(states)) for h in range(B_HEADS)],
                               axis=1)

    def kv_outer(k_decayed, v):
        full = lax.dot_general(k_decayed, v, (((0,), (0,)), ((), ())), preferred_element_type=F32)
        return jnp.concatenate([full[h * B_DK:(h + 1) * B_DK, h * B_DV:(h + 1) * B_DV] for h in range(B_HEADS)], axis=0)

    def scaled_k(c):
        return k_ref[0, rows(c), :] * (B_DK ** -0.5)

    @pl.when(phase == 0)
    def _():
        @pl.when(n == 0)
        def _():
            bwd_state[...] = jnp.zeros_like(bwd_state)

        blk = n_blocks - 1 - n
        state = bwd_state[...]
        for c in reversed(range(B_BLOCK)):
            bwd_store[blk * B_BLOCK + c] = state.astype(BF16)
            k_dec = (scaled_k(c).astype(F32) * kdb_ref[...]).astype(BF16)
            state = cdb_ref[...] * state + kv_outer(k_dec, v_ref[0, rows(c), :])
        bwd_state[...] = state

    @pl.when(phase == 1)
    def _():
        @pl.when(n == 0)
        def _():
            fwd_state[...] = jnp.zeros_like(fwd_state)

        state = fwd_state[...]
        for c in range(B_BLOCK):
            q = q_ref[0, rows(c), :]
            k = scaled_k(c)
            v = v_ref[0, rows(c), :]
            q32 = q.astype(F32)
            q_dec = jnp.concatenate([(q32 * qdf_ref[...]).astype(BF16), (q32 * qdb_ref[...]).astype(BF16)], axis=1)
            k_dec = (k.astype(F32) * kdf_ref[...]).astype(BF16)
            states = jnp.concatenate([state.astype(BF16), bwd_store[n * B_BLOCK + c]], axis=0)
            s_all = lax.dot_general(q, stacked_heads(k), (((1,), (1,)), ((), ())), preferred_element_type=F32)
            cross_all = jnp.dot(q_dec, head_block_diagonal(states), preferred_element_type=F32)
            for h in range(B_HEADS):
                vh = v[:, h * B_DV:(h + 1) * B_DV]
                inner = jnp.dot((s_all[:, rows(h)] * dmat_ref[h]).astype(BF16), vh, preferred_element_type=F32)
                y = inner + cross_all[:, h * B_DV:(h + 1) * B_DV]
                mu = jnp.mean(y, axis=-1, keepdims=True)
                cen = y - mu
                var = jnp.mean(cen * cen, axis=-1, keepdims=True)
                yn = cen * lax.rsqrt(var + GN_EPS)
                gate = g_ref[0, rows(c), h * B_DV:(h + 1) * B_DV].astype(F32)
                o_ref[0, rows(c), h * B_DV:(h + 1) * B_DV] = (gate * jax.nn.sigmoid(gate) * yn).astype(BF16)
            state = cdf_ref[...] * state + kv_outer(k_dec, v)
        fwd_state[...] = state


def _retention_tables(logit_fwd, logit_bwd):
    lg_f = jax.nn.log_sigmoid(logit_fwd.astype(F32))
    lg_b = jax.nn.log_sigmoid(logit_bwd.astype(F32))
    idx = jnp.arange(B_CHUNK, dtype=F32)
    diff = idx[:, None] - idx[None, :]
    causal = diff >= 0
    dmat = jnp.where(causal[None],
                     jnp.exp(lg_f[:, None, None] * jnp.where(causal, diff, 0.0)[None]),
                     jnp.exp(lg_b[:, None, None] * jnp.where(causal, 0.0, -diff)[None]))

    def per_lane(lg, power):
        return jnp.repeat(jnp.exp(lg[None, :] * power[:, None]), B_DK, axis=1)

    def per_row(lg):
        return jnp.broadcast_to(jnp.repeat(jnp.exp(lg * B_CHUNK), B_DK)[:, None], (B_HEADS * B_DK, B_DV))

    return (dmat, per_lane(lg_f, idx + 1), per_lane(lg_b, B_CHUNK - idx), per_lane(lg_f, B_CHUNK - 1 - idx),
            per_lane(lg_b, idx), per_row(lg_f), per_row(lg_b))


def mixer_b(z, batch, seq, logit_fwd, logit_bwd):
    n_chunks = seq // B_CHUNK
    n_blocks = n_chunks // B_BLOCK
    block_rows = B_BLOCK * B_CHUNK
    zv = z.reshape(batch, seq, MAIN_W)
    qk_w = B_HEADS * B_DK
    v_w = B_HEADS * B_DV

    def scan_block(ph, n):
        return (1 - ph) * (n_blocks - 1 - n) + ph * n

    tables = _retention_tables(logit_fwd, logit_bwd)
    in_specs = [pl.BlockSpec((1, block_rows, qk_w), lambda b, ph, n: (b, ph * n, BQ_BLK256)),
                pl.BlockSpec((1, block_rows, qk_w), lambda b, ph, n: (b, scan_block(ph, n), BK_BLK256)),
                pl.BlockSpec((1, block_rows, v_w), lambda b, ph, n: (b, scan_block(ph, n), BV_BLK)),
                pl.BlockSpec((1, block_rows, v_w), lambda b, ph, n: (b, ph * n, BG_BLK))]
    in_specs += [_const_spec(t.shape) for t in tables]
    out = pl.pallas_call(
        functools.partial(_mixer_b_kernel, n_blocks=n_blocks),
        grid=(batch, 2, n_blocks),
        in_specs=in_specs,
        out_specs=pl.BlockSpec((1, block_rows, v_w), lambda b, ph, n: (b, ph * n, 0)),
        out_shape=jax.ShapeDtypeStruct((batch, seq, v_w), BF16),
        scratch_shapes=[pltpu.VMEM((qk_w, B_DV), F32), pltpu.VMEM((qk_w, B_DV), F32),
                        pltpu.VMEM((n_chunks, qk_w, B_DV), BF16)],
        compiler_params=_params(("arbitrary", "arbitrary", "arbitrary")),
        name="mixer_b",
    )(zv, zv, zv, zv, *tables)
    return out.reshape(batch * seq, v_w)


C_TQ = C_QR * GRID_W
C_KROWS = 3 * C_QR
C_TK = C_KROWS * GRID_W
C_PAIRS = C_KROWS // 2
C_NTAB = 2 * C_KH - 2


def _mixer_c_kernel(q_ref, kp_ref, kc_ref, kn_ref, vp_ref, vc_ref, vn_ref, tab_ref, o_ref, kf, vf, *, rows):
    cur = C_STEP * C_TQ
    kf[0:C_TQ, :] = kp_ref[0]
    kf[C_TQ:C_TQ + cur, :] = kc_ref[0]
    kf[C_TQ + cur:, :] = kn_ref[0]
    vf[0:C_TQ, :] = vp_ref[0]
    vf[C_TQ:C_TQ + cur, :] = vc_ref[0]
    vf[C_TQ + cur:, :] = vn_ref[0]

    left = lax.broadcasted_iota(jnp.int32, (1, 2 * C_HD), 1) < C_HD
    for sb in range(C_STEP):
        blk = pl.program_id(1) * C_STEP + sb
        q_rows = slice(sb * C_TQ, (sb + 1) * C_TQ)
        k_rows = slice(sb * C_TQ, sb * C_TQ + C_TK)
        key_row = blk * C_QR - C_QR + lax.broadcasted_iota(jnp.int32, (1, C_TK), 1) // GRID_W
        pens = []
        for a in range(C_QR):
            row_start = jnp.clip(blk * C_QR + a - C_KH // 2, 0, rows - C_KH)
            pens.append(jnp.where((key_row >= row_start) & (key_row < row_start + C_KH), 0.0, NEG).astype(F32))

        for hp in range(C_HEADS // 2):
            cols = slice(hp * 2 * C_HD, (hp + 1) * 2 * C_HD)
            qp = q_ref[0, q_rows, cols]
            kp = kf[k_rows, cols]
            probs = []
            for hh, s in enumerate(_pair_scores(qp, kp, left)):
                h = hp * 2 + hh
                rows_p = []
                for a in range(C_QR):
                    bias = jnp.concatenate([tab_ref[h, 2 * t - C_QR - a + C_KH - 1] for t in range(C_PAIRS)], axis=1)
                    sa = s[a * GRID_W:(a + 1) * GRID_W, :] + bias + pens[a]
                    m = jnp.max(sa, axis=-1, keepdims=True)
                    rows_p.append(jnp.exp2(sa - m).astype(BF16))
                probs.append(jnp.concatenate(rows_p, axis=0))
            num, den = _pair_weighted_sum(probs, vf[k_rows, cols], left)
            o_ref[0, q_rows, cols] = (num / den).astype(BF16)


def _neighbourhood_bias(rpb):
    qc = jnp.arange(GRID_W)[:, None]
    kc = jnp.arange(GRID_W)[None, :]
    col_start = jnp.clip(qc - C_KW // 2, 0, GRID_W - C_KW)
    col_ok = (kc >= col_start) & (kc < col_start + C_KW)
    onehot = ((kc - qc + (C_KW - 1))[:, :, None] == jnp.arange(2 * C_KW - 1)[None, None, :]).astype(F32)
    band = jnp.einsum("hrd,qkd->hrqk", rpb.astype(F32), onehot, precision=lax.Precision.HIGHEST)
    band = jnp.where(col_ok[None, None], band * LOG2E, NEG)
    return jnp.concatenate([band[:, :C_NTAB], band[:, 1:C_NTAB + 1]], axis=-1)


def mixer_c(z, batch, seq, rpb):
    rows = seq // GRID_W
    n_blk = rows // C_QR
    zv = z.reshape(batch, seq, MAIN_W)
    halo = (1, C_TQ, HEAD_BLOCK)
    tile = (1, C_STEP * C_TQ, HEAD_BLOCK)
    tab = _neighbourhood_bias(rpb)

    def above(c):
        return lambda b, i: (b, jnp.maximum(i * C_STEP - 1, 0), c)

    def here(c):
        return lambda b, i: (b, i, c)

    def below(c):
        return lambda b, i: (b, jnp.minimum((i + 1) * C_STEP, n_blk - 1), c)

    out = pl.pallas_call(
        functools.partial(_mixer_c_kernel, rows=rows),
        grid=(batch, n_blk // C_STEP),
        in_specs=[pl.BlockSpec(tile, here(CQ_BLK)),
                  pl.BlockSpec(halo, above(CK_BLK)), pl.BlockSpec(tile, here(CK_BLK)), pl.BlockSpec(halo, below(CK_BLK)),
                  pl.BlockSpec(halo, above(CV_BLK)), pl.BlockSpec(tile, here(CV_BLK)), pl.BlockSpec(halo, below(CV_BLK)),
                  _const_spec(tab.shape)],
        out_specs=pl.BlockSpec(tile, lambda b, i: (b, i, 0)),
        out_shape=jax.ShapeDtypeStruct((batch, seq, HEAD_BLOCK), BF16),
        scratch_shapes=[pltpu.VMEM(((C_STEP + 2) * C_TQ, HEAD_BLOCK), BF16),
                        pltpu.VMEM(((C_STEP + 2) * C_TQ, HEAD_BLOCK), BF16)],
        compiler_params=_params(("arbitrary", "arbitrary")),
        name="mixer_c",
    )(zv, zv, zv, zv, zv, zv, zv, tab)
    return out.reshape(batch * seq, HEAD_BLOCK)


def _merge_kernel(x_ref, oa0_ref, oa1_ref, oa2_ref, la0_ref, la1_ref, la2_ref, yb_ref, yc_ref, gates_ref, spread_ref,
                  wa_ref, wb_ref, wc_ref, wo_ref, g_ref, b_ref, o_ref):
    lses = [la0_ref[...], la1_ref[...], la2_ref[...]]
    top = jnp.maximum(jnp.maximum(lses[0], lses[1]), lses[2])
    weights = [jnp.exp2(lse - top) for lse in lses]
    total = weights[0] + weights[1] + weights[2]
    ya = None
    for o_g_ref, w in zip((oa0_ref, oa1_ref, oa2_ref), weights):
        share = w / total
        hi = share.astype(BF16)
        lo = (share - hi.astype(F32)).astype(BF16)
        wide = jnp.dot(jnp.concatenate([hi, lo], axis=1), spread_ref[...], preferred_element_type=F32)
        term = wide * o_g_ref[...].astype(F32)
        ya = term if ya is None else ya + term
    ya = ya.astype(BF16)

    merged = None
    for br, (y, w_ref) in enumerate(((ya, wa_ref), (yb_ref[...], wb_ref), (yc_ref[...], wc_ref))):
        proj = jnp.dot(y, w_ref[...], preferred_element_type=F32)
        gate = jax.nn.sigmoid(gates_ref[:, br * D_MODEL:(br + 1) * D_MODEL].astype(F32))
        merged = gate * proj if merged is None else merged + gate * proj
    out = jnp.dot(merged.astype(BF16), wo_ref[...], preferred_element_type=F32)
    o_ref[...] = _layer_norm(ALPHA * x_ref[...] + out, g_ref[...], b_ref[...])


def _lse_spread():
    src = jnp.arange(2 * A_HD)[:, None]
    head = jnp.arange(HEAD_BLOCK)[None, :] // A_HD
    first_lane = (head % 2) * A_HD + (head // 2) * LSE_LANES
    once = (src == first_lane).astype(BF16)
    return jnp.concatenate([once, once], axis=0)


def merge_out_ln(x, oa, la, yb, yc, z, wa, wb, wc, wo, g, b):
    t = x.shape[0]
    tm = 1024
    row = pl.BlockSpec((tm, D_MODEL), lambda i: (i, 0))
    br = pl.BlockSpec((tm, HEAD_BLOCK), lambda i: (i, 0))
    lse = pl.BlockSpec((tm, 2 * A_HD), lambda i: (i, 0))
    w_br = _const_spec((HEAD_BLOCK, D_MODEL))
    return pl.pallas_call(
        _merge_kernel,
        grid=(t // tm,),
        in_specs=[row] + [br] * 3 + [lse] * 3 + [br] * 2 + [
            pl.BlockSpec((tm, GATE_W), lambda i: (i, 0)), _const_spec((4 * A_HD, HEAD_BLOCK)), w_br, w_br, w_br,
            _const_spec((D_MODEL, D_MODEL)), _const_spec((1, D_MODEL)), _const_spec((1, D_MODEL))],
        out_specs=row,
        out_shape=jax.ShapeDtypeStruct((t, D_MODEL), F32),
        compiler_params=_params(("arbitrary",)),
        name="merge_out_ln",
    )(x, *oa, *la, yb, yc, z, _lse_spread(), wa, wb, wc, wo, g, b)


def _split_in_weights(w_in):
    a_w = A_GROUPS * HEAD_BLOCK
    aq, ak, av = w_in[:, :a_w] * (A_HD ** -0.5 * LOG2E), w_in[:, a_w:2 * a_w], w_in[:, 2 * a_w:3 * a_w]
    b_w = 2 * B_HEADS * (B_DK + B_DV)
    rest_b = w_in[:, 3 * a_w:3 * a_w + b_w]
    cq = w_in[:, 3 * a_w + b_w:3 * a_w + b_w + HEAD_BLOCK] * (C_HD ** -0.5 * LOG2E)
    rest = jnp.concatenate([rest_b, cq, w_in[:, 3 * a_w + b_w + HEAD_BLOCK:D_IN - GATE_W]], axis=1)

    def group(g):
        cols = slice(g * HEAD_BLOCK, (g + 1) * HEAD_BLOCK)
        return jnp.concatenate([aq[:, cols], ak[:, cols], av[:, cols]], axis=1)

    main = jnp.concatenate([w_in[:, D_IN - GATE_W:], group(0), rest], axis=1)
    return main.astype(BF16), [group(g).astype(BF16) for g in range(1, A_GROUPS)]


def _trunk(x, layers):
    batch, seq, _ = x.shape
    x = x.reshape(batch * seq, D_MODEL)
    for p in layers:
        dils = tuple(d for _, d in A_PATTERNS[1:])
        x1, x1b, *x1_by_residue = ffn_ln(x, p["wg1"], p["wu1"], p["wd1"], p["g1"], p["b1"], batch, dils)
        z = in_proj(x1b, p["w_main"])
        oa, la = [], []
        o, l = mixer_a_group(z.reshape(batch, 1, seq, MAIN_W), 1, AQ_BLK, AK_BLK, AV_BLK)
        oa.append(o.reshape(batch * seq, HEAD_BLOCK))
        la.append(l.reshape(batch * seq, 2 * A_HD))
        for dilation, xg, w_g in zip(dils, x1_by_residue, p["w_groups"]):
            zg = in_proj(xg.reshape(batch * seq, D_MODEL), w_g)
            o, l = mixer_a_group(zg.reshape(batch, dilation, seq // dilation, A_QKV_W), dilation, 0, 1, 2)
            oa.append(_by_token(o, batch, seq))
            la.append(_by_token(l, batch, seq))
        yb = mixer_b(z, batch, seq, p["logit_fwd"], p["logit_bwd"])
        yc = mixer_c(z, batch, seq, p["rpb"])
        x2 = merge_out_ln(x1, oa, la, yb, yc, z, p["wa"], p["wb"], p["wc"], p["wo"], p["g2"], p["b2"])
        (x,) = ffn_ln(x2, p["wg2"], p["wu2"], p["wd2"], p["g3"], p["b3"])
    return x.reshape(batch, seq, D_MODEL)


def kernel(x_prompt, x_sample, ffn1_w_gate, ffn1_w_up, ffn1_w_down, ln1_g, ln1_b, w_in, ret_logit_fwd, ret_logit_bwd, na_rpb, w_branch_a, w_branch_b, w_branch_c, w_out, ln2_g, ln2_b, ffn2_w_gate, ffn2_w_up, ffn2_w_down, ln3_g, ln3_b):
    def vec(v):
        return v.astype(F32).reshape(1, D_MODEL)

    layers = []
    for i in range(DEPTH):
        w_main, w_groups = _split_in_weights(w_in[i])
        layers.append(dict(
            wg1=ffn1_w_gate[i].astype(BF16), wu1=ffn1_w_up[i].astype(BF16), wd1=ffn1_w_down[i].astype(BF16),
            g1=vec(ln1_g[i]), b1=vec(ln1_b[i]),
            w_main=w_main, w_groups=w_groups,
            logit_fwd=ret_logit_fwd[i], logit_bwd=ret_logit_bwd[i], rpb=na_rpb[i],
            wa=w_branch_a[i].astype(BF16), wb=w_branch_b[i].astype(BF16), wc=w_branch_c[i].astype(BF16),
            wo=w_out[i].astype(BF16), g2=vec(ln2_g[i]), b2=vec(ln2_b[i]),
            wg2=ffn2_w_gate[i].astype(BF16), wu2=ffn2_w_up[i].astype(BF16), wd2=ffn2_w_down[i].astype(BF16),
            g3=vec(ln3_g[i]), b3=vec(ln3_b[i])))
    return (_trunk(x_prompt, layers), _trunk(x_sample, layers))
```

```python
import functools

import jax
import jax.numpy as jnp
from jax import lax
from jax.experimental import pallas as pl
from jax.experimental.pallas import tpu as pltpu

F32 = jnp.float32
BF16 = jnp.bfloat16

D_MODEL = 1024
DEPTH = 2
D_FF = 2816
LN_EPS = 1e-5
GN_EPS = 1e-5
ALPHA = (2 * DEPTH) ** 0.25

A_PATTERNS = ((128, 1), (512, 4), (2048, 16))
A_GROUPS = len(A_PATTERNS)
A_HEADS = 8
A_HD = 64
A_HALF = 64
LSE_LANES = A_HD // (A_HEADS // 2)
B_HEADS = 4
B_DK = 64
B_DV = 128
B_CHUNK = 128
B_BLOCK = 8
C_HEADS = 8
C_HD = 64
GRID_W = 64
C_KH = 8
C_KW = 16
C_QR = 4
C_STEP = 2
D_IN = 10752
NEG = -1e30
LANES = 128
LOG2E = 1.4426950408889634

GATE_W = 3 * D_MODEL
HEAD_BLOCK = 512
A_QKV_W = 3 * HEAD_BLOCK
MAIN_W = D_IN - (A_GROUPS - 1) * A_QKV_W
AQ_BLK, AK_BLK, AV_BLK = 6, 7, 8
BQ_BLK256, BK_BLK256 = 18, 19
BV_BLK, BG_BLK = 10, 11
CQ_BLK, CK_BLK, CV_BLK = 12, 13, 14

VMEM_LIMIT = 56 * 1024 * 1024
FF_CHUNKS = ((0, 512), (512, 1024), (1024, 1536), (1536, 2048), (2048, 2560), (2560, 2816))
FF_NORM_ROWS = 256
MERGE_ROWS = 256


def _params(sem):
    return pltpu.CompilerParams(dimension_semantics=sem, vmem_limit_bytes=VMEM_LIMIT)


def _const_spec(shape):
    zeros = (0,) * len(shape)
    return pl.BlockSpec(shape, lambda *_: zeros)


def _layer_norm(r, g, b):
    mu = jnp.mean(r, axis=-1, keepdims=True)
    c = r - mu
    var = jnp.mean(c * c, axis=-1, keepdims=True)
    return c * lax.rsqrt(var + LN_EPS) * g + b


def _ffn_ln_kernel(x_ref, wg_ref, wu_ref, wd_ref, g_ref, b_ref, o_ref, *rest, dilations):
    x = x_ref[...]
    xb = x.astype(BF16)
    hidden = []
    for c0, c1 in FF_CHUNKS:
        gate = jnp.dot(xb, wg_ref[:, c0:c1], preferred_element_type=F32)
        up = jnp.dot(xb, wu_ref[:, c0:c1], preferred_element_type=F32)
        hidden.append((gate * jax.nn.sigmoid(gate) * up).astype(BF16))
    hidden = jnp.concatenate(hidden, axis=1)
    if rest:
        ob_ref, *og_refs, slabs = rest
    n_slabs = D_MODEL // LANES
    rows = FF_NORM_ROWS
    for r0 in range(0, x.shape[0], rows):
        acc = jnp.dot(hidden[r0:r0 + rows], wd_ref[...], preferred_element_type=F32)
        y = _layer_norm(ALPHA * x[r0:r0 + rows] + 0.5 * acc, g_ref[...], b_ref[...])
        o_ref[r0:r0 + rows, :] = y
        if not rest:
            continue
        ob_ref[r0:r0 + rows, :] = y.astype(BF16)
        done = 1
        ordered = y
        for d, og_ref in zip(dilations, og_refs):
            for s in range(n_slabs):
                slabs[s] = ordered[:, s * LANES:(s + 1) * LANES]
            step = d // done
            group = rows // done
            pieces = []
            for r in range(d):
                first = (r % done) * group + r // done
                picked = [slabs[s, pl.ds(first, rows // d, stride=step), :] for s in range(n_slabs)]
                pieces.append(jnp.concatenate(picked, axis=1))
                og_ref[0, r, r0 // d:(r0 + rows) // d, :] = pieces[-1].astype(BF16)
            ordered = jnp.concatenate(pieces, axis=0)
            done = d


def ffn_ln(x, wg, wu, wd, g, b, batch=None, dilations=()):
    t = x.shape[0]
    tm = 1024
    row = pl.BlockSpec((tm, D_MODEL), lambda i: (i, 0))
    out_shape = [jax.ShapeDtypeStruct((t, D_MODEL), F32)]
    out_specs = [row]
    scratch = []
    if dilations:
        assert all(b % a == 0 for a, b in zip((1,) + dilations, dilations)), dilations
        tiles_per_seq = t // batch // tm
        out_shape.append(jax.ShapeDtypeStruct((t, D_MODEL), BF16))
        out_specs.append(row)
        for d in dilations:
            out_shape.append(jax.ShapeDtypeStruct((batch, d, t // batch // d, D_MODEL), BF16))
            out_specs.append(pl.BlockSpec((1, d, tm // d, D_MODEL),
                                          lambda i: (i // tiles_per_seq, 0, i % tiles_per_seq, 0)))
        scratch = [pltpu.VMEM((D_MODEL // LANES, FF_NORM_ROWS, LANES), F32)]
    return pl.pallas_call(
        functools.partial(_ffn_ln_kernel, dilations=dilations),
        grid=(t // tm,),
        in_specs=[row, _const_spec((D_MODEL, D_FF)), _const_spec((D_MODEL, D_FF)), _const_spec((D_FF, D_MODEL)),
                  _const_spec((1, D_MODEL)), _const_spec((1, D_MODEL))],
        out_specs=out_specs,
        out_shape=out_shape,
        scratch_shapes=scratch,
        compiler_params=_params(("arbitrary",)),
        name="ffn_ln",
    )(x, wg, wu, wd, g, b)


def _in_proj_kernel(x_ref, w_ref, z_ref):
    z_ref[...] = jnp.dot(x_ref[...], w_ref[...], preferred_element_type=F32).astype(BF16)


def in_proj(xb, w_in):
    t = xb.shape[0]
    width = w_in.shape[1]
    tm = 2048
    tn = 2560 if width % 2560 == 0 else A_QKV_W
    return pl.pallas_call(
        _in_proj_kernel,
        grid=(t // tm, width // tn),
        in_specs=[pl.BlockSpec((tm, D_MODEL), lambda i, j: (i, 0)), pl.BlockSpec((D_MODEL, tn), lambda i, j: (0, j))],
        out_specs=pl.BlockSpec((tm, tn), lambda i, j: (i, j)),
        out_shape=jax.ShapeDtypeStruct((t, width), BF16),
        compiler_params=_params(("arbitrary", "arbitrary")),
        name="in_proj",
    )(xb, w_in)


def _pair_scores(q_pair, k_pair, left):
    zero = jnp.zeros_like(q_pair)
    dims = (((1,), (1,)), ((), ()))
    return [lax.dot_general(jnp.where(left, q_pair, zero), k_pair, dims, preferred_element_type=F32),
            lax.dot_general(jnp.where(left, zero, q_pair), k_pair, dims, preferred_element_type=F32)]


def _pair_weighted_sum(probs, v_pair, left):
    rhs = []
    for hh in range(2):
        sel = left if hh == 0 else jnp.logical_not(left)
        ones = jnp.broadcast_to(jnp.where(sel, 1.0, 0.0).astype(BF16), v_pair.shape)
        rhs.append(jnp.concatenate([jnp.where(sel, v_pair, jnp.zeros_like(v_pair)), ones], axis=1))
    out = jnp.dot(jnp.concatenate(probs, axis=1), jnp.concatenate(rhs, axis=0), preferred_element_type=F32)
    width = v_pair.shape[1]
    return out[:, :width], out[:, width:]


A_QB = 128
A_KB = A_QB + 2 * A_HALF


def _mixer_a_kernel(q_ref, kp_ref, kc_ref, kn_ref, vp_ref, vc_ref, vn_ref, bias_ref, o_ref, l_ref, kf, vf, *,
                    tl, seq_len):
    i = pl.program_id(2)
    kf[0:A_HALF, :] = kp_ref[0, 0]
    kf[A_HALF:A_HALF + tl, :] = kc_ref[0, 0]
    kf[A_HALF + tl:, :] = kn_ref[0, 0]
    vf[0:A_HALF, :] = vp_ref[0, 0]
    vf[A_HALF:A_HALF + tl, :] = vc_ref[0, 0]
    vf[A_HALF + tl:, :] = vn_ref[0, 0]

    lane = lax.broadcasted_iota(jnp.int32, (1, 2 * A_HD), 1)
    left = lane < A_HD
    lse_slot = (lane % A_HD) // LSE_LANES

    def block(j, carry):
        q0 = pl.multiple_of(j * A_QB, A_QB)
        start = i * tl + q0
        variant = (start == 0).astype(jnp.int32) + 2 * (start + A_QB == seq_len).astype(jnp.int32)
        lse_all = None
        for hp in range(A_HEADS // 2):
            cols = slice(hp * 2 * A_HD, (hp + 1) * 2 * A_HD)
            qp = q_ref[0, 0, pl.ds(q0, A_QB), cols]
            kp = kf[pl.ds(q0, A_KB), cols]
            probs, tops = [], []
            for hh, s in enumerate(_pair_scores(qp, kp, left)):
                s = s + bias_ref[variant, hp * 2 + hh]
                m = jnp.max(s, axis=-1, keepdims=True)
                probs.append(jnp.exp2(s - m).astype(BF16))
                tops.append(m)
            num, den = _pair_weighted_sum(probs, vf[pl.ds(q0, A_KB), cols], left)
            o_ref[0, 0, pl.ds(q0, A_QB), cols] = (num / den).astype(BF16)
            lse_pair = jnp.where(left, tops[0], tops[1]) + jnp.log2(den)
            lse_all = lse_pair if hp == 0 else jnp.where(lse_slot == hp, lse_pair, lse_all)
        l_ref[0, 0, pl.ds(q0, A_QB), :] = lse_all
        return carry

    lax.fori_loop(0, tl // A_QB, block, 0, unroll=True)


def _alibi_bias(dilation):
    slopes = 2.0 ** (-8.0 * jnp.arange(1, A_HEADS + 1, dtype=F32) / A_HEADS)
    key = jnp.arange(A_KB)[None, :] - A_HALF
    rel = key - jnp.arange(A_QB)[:, None]
    dist = (jnp.abs(rel) * dilation).astype(F32)
    bias = -slopes[:, None, None] * dist[None] * LOG2E
    in_window = jnp.abs(rel) <= A_HALF
    variants = []
    for v in range(4):
        ok = in_window
        if v & 1:
            ok = ok & (key >= 0)
        if v & 2:
            ok = ok & (key < A_QB)
        variants.append(jnp.where(ok[None], bias, NEG))
    return jnp.stack(variants, 0)


def mixer_a_group(zg, dilation, q_blk, k_blk, v_blk):
    batch, _, strided_len, _ = zg.shape
    tl = min(1024, strided_len)
    halo_per_tile = tl // A_HALF
    n_halo = strided_len // A_HALF

    def cur(c):
        return lambda b, r, i: (b, r, i, c)

    def before(c):
        return lambda b, r, i: (b, r, jnp.maximum(i * halo_per_tile - 1, 0), c)

    def after(c):
        return lambda b, r, i: (b, r, jnp.minimum((i + 1) * halo_per_tile, n_halo - 1), c)

    tile = (1, 1, tl, HEAD_BLOCK)
    halo = (1, 1, A_HALF, HEAD_BLOCK)
    out_spec = pl.BlockSpec(tile, cur(0))
    out_dims = (batch, dilation, strided_len, HEAD_BLOCK)
    return pl.pallas_call(
        functools.partial(_mixer_a_kernel, tl=tl, seq_len=strided_len),
        grid=(batch, dilation, strided_len // tl),
        in_specs=[pl.BlockSpec(tile, cur(q_blk)),
                  pl.BlockSpec(halo, before(k_blk)), pl.BlockSpec(tile, cur(k_blk)), pl.BlockSpec(halo, after(k_blk)),
                  pl.BlockSpec(halo, before(v_blk)), pl.BlockSpec(tile, cur(v_blk)), pl.BlockSpec(halo, after(v_blk)),
                  _const_spec((4, A_HEADS, A_QB, A_KB))],
        out_specs=[out_spec, pl.BlockSpec((1, 1, tl, 2 * A_HD), cur(0))],
        out_shape=[jax.ShapeDtypeStruct(out_dims, BF16),
                   jax.ShapeDtypeStruct((batch, dilation, strided_len, 2 * A_HD), F32)],
        scratch_shapes=[pltpu.VMEM((tl + 2 * A_HALF, HEAD_BLOCK), BF16), pltpu.VMEM((tl + 2 * A_HALF, HEAD_BLOCK), BF16)],
        compiler_params=_params(("arbitrary", "arbitrary", "arbitrary")),
        name=f"mixer_a_d{dilation}",
    )(zg, zg, zg, zg, zg, zg, zg, _alibi_bias(dilation))


def _by_token(x, batch, seq):
    return x.transpose(0, 2, 1, 3).reshape(batch * seq, x.shape[-1])


def _mixer_b_kernel(q_ref, k_ref, v_ref, g_ref, dmat_ref, qdf_ref, qdb_ref, kdf_ref, kdb_ref, cdf_ref, cdb_ref,
                    o_ref, fwd_state, bwd_state, bwd_store, *, n_blocks):
    phase = pl.program_id(1)
    n = pl.program_id(2)
    qk_w = B_HEADS * B_DK
    head_of_lane = lax.broadcasted_iota(jnp.int32, (1, qk_w), 1) // B_DK

    def rows(c):
        return slice(c * B_CHUNK, (c + 1) * B_CHUNK)

    def stacked_heads(t):
        lane_head = jnp.concatenate([head_of_lane] * (t.shape[1] // qk_w), axis=1)
        return jnp.concatenate([jnp.where(lane_head == h, t, jnp.zeros_like(t)) for h in range(B_HEADS)], axis=0)

    def head_block_diagonal(states):
        row_head = (lax.broadcasted_iota(jnp.int32, (states.shape[0], 1), 0) % qk_w) // B_DK
        return jnp.concatenate([jnp.where(row_head == h, states, jnp.zeros_like(states)) for h in range(B_HEADS)],
                               axis=1)

    def kv_outer(k_decayed, v):
        full = lax.dot_general(k_decayed, v, (((0,), (0,)), ((), ())), preferred_element_type=F32)
        return jnp.concatenate([full[h * B_DK:(h + 1) * B_DK, h * B_DV:(h + 1) * B_DV] for h in range(B_HEADS)], axis=0)

    def scaled_k(c):
        return k_ref[0, rows(c), :] * (B_DK ** -0.5)

    @pl.when(phase == 0)
    def _():
        @pl.when(n == 0)
        def _():
            bwd_state[...] = jnp.zeros_like(bwd_state)

        blk = n_blocks - 1 - n
        state = bwd_state[...]
        for c in reversed(range(B_BLOCK)):
            bwd_store[blk * B_BLOCK + c] = state.astype(BF16)
            k_dec = (scaled_k(c).astype(F32) * kdb_ref[...]).astype(BF16)
            state = cdb_ref[...] * state + kv_outer(k_dec, v_ref[0, rows(c), :])
        bwd_state[...] = state

    @pl.when(phase == 1)
    def _():
        @pl.when(n == 0)
        def _():
            fwd_state[...] = jnp.zeros_like(fwd_state)

        state = fwd_state[...]
        for c in range(B_BLOCK):
            q = q_ref[0, rows(c), :]
            k = scaled_k(c)
            v = v_ref[0, rows(c), :]
            q32 = q.astype(F32)
            q_dec = jnp.concatenate([(q32 * qdf_ref[...]).astype(BF16), (q32 * qdb_ref[...]).astype(BF16)], axis=1)
            k_dec = (k.astype(F32) * kdf_ref[...]).astype(BF16)
            states = jnp.concatenate([state.astype(BF16), bwd_store[n * B_BLOCK + c]], axis=0)
            s_all = lax.dot_general(q, stacked_heads(k), (((1,), (1,)), ((), ())), preferred_element_type=F32)
            cross_all = jnp.dot(q_dec, head_block_diagonal(states), preferred_element_type=F32)
            for h in range(B_HEADS):
                vh = v[:, h * B_DV:(h + 1) * B_DV]
                inner = jnp.dot((s_all[:, rows(h)] * dmat_ref[h]).astype(BF16), vh, preferred_element_type=F32)
                y = inner + cross_all[:, h * B_DV:(h + 1) * B_DV]
                mu = jnp.mean(y, axis=-1, keepdims=True)
                cen = y - mu
                var = jnp.mean(cen * cen, axis=-1, keepdims=True)
                yn = cen * lax.rsqrt(var + GN_EPS)
                gate = g_ref[0, rows(c), h * B_DV:(h + 1) * B_DV].astype(F32)
                o_ref[0, rows(c), h * B_DV:(h + 1) * B_DV] = (gate * jax.nn.sigmoid(gate) * yn).astype(BF16)
            state = cdf_ref[...] * state + kv_outer(k_dec, v)
        fwd_state[...] = state


def _retention_tables(logit_fwd, logit_bwd):
    lg_f = jax.nn.log_sigmoid(logit_fwd.astype(F32))
    lg_b = jax.nn.log_sigmoid(logit_bwd.astype(F32))
    idx = jnp.arange(B_CHUNK, dtype=F32)
    diff = idx[:, None] - idx[None, :]
    causal = diff >= 0
    dmat = jnp.where(causal[None],
                     jnp.exp(lg_f[:, None, None] * jnp.where(causal, diff, 0.0)[None]),
                     jnp.exp(lg_b[:, None, None] * jnp.where(causal, 0.0, -diff)[None]))

    def per_lane(lg, power):
        return jnp.repeat(jnp.exp(lg[None, :] * power[:, None]), B_DK, axis=1)

    def per_row(lg):
        return jnp.broadcast_to(jnp.repeat(jnp.exp(lg * B_CHUNK), B_DK)[:, None], (B_HEADS * B_DK, B_DV))

    return (dmat, per_lane(lg_f, idx + 1), per_lane(lg_b, B_CHUNK - idx), per_lane(lg_f, B_CHUNK - 1 - idx),
            per_lane(lg_b, idx), per_row(lg_f), per_row(lg_b))


def mixer_b(z, batch, seq, logit_fwd, logit_bwd):
    n_chunks = seq // B_CHUNK
    n_blocks = n_chunks // B_BLOCK
    block_rows = B_BLOCK * B_CHUNK
    zv = z.reshape(batch, seq, MAIN_W)
    qk_w = B_HEADS * B_DK
    v_w = B_HEADS * B_DV

    def scan_block(ph, n):
        return (1 - ph) * (n_blocks - 1 - n) + ph * n

    tables = _retention_tables(logit_fwd, logit_bwd)
    in_specs = [pl.BlockSpec((1, block_rows, qk_w), lambda b, ph, n: (b, ph * n, BQ_BLK256)),
                pl.BlockSpec((1, block_rows, qk_w), lambda b, ph, n: (b, scan_block(ph, n), BK_BLK256)),
                pl.BlockSpec((1, block_rows, v_w), lambda b, ph, n: (b, scan_block(ph, n), BV_BLK)),
                pl.BlockSpec((1, block_rows, v_w), lambda b, ph, n: (b, ph * n, BG_BLK))]
    in_specs += [_const_spec(t.shape) for t in tables]
    out = pl.pallas_call(
        functools.partial(_mixer_b_kernel, n_blocks=n_blocks),
        grid=(batch, 2, n_blocks),
        in_specs=in_specs,
        out_specs=pl.BlockSpec((1, block_rows, v_w), lambda b, ph, n: (b, ph * n, 0)),
        out_shape=jax.ShapeDtypeStruct((batch, seq, v_w), BF16),
        scratch_shapes=[pltpu.VMEM((qk_w, B_DV), F32), pltpu.VMEM((qk_w, B_DV), F32),
                        pltpu.VMEM((n_chunks, qk_w, B_DV), BF16)],
        compiler_params=_params(("arbitrary", "arbitrary", "arbitrary")),
        name="mixer_b",
    )(zv, zv, zv, zv, *tables)
    return out.reshape(batch * seq, v_w)


C_TQ = C_QR * GRID_W
C_KROWS = 3 * C_QR
C_TK = C_KROWS * GRID_W
C_PAIRS = C_KROWS // 2
C_NTAB = 2 * C_KH - 2


def _mixer_c_kernel(q_ref, kp_ref, kc_ref, kn_ref, vp_ref, vc_ref, vn_ref, tab_ref, o_ref, kf, vf, *, rows):
    cur = C_STEP * C_TQ
    kf[0:C_TQ, :] = kp_ref[0]
    kf[C_TQ:C_TQ + cur, :] = kc_ref[0]
    kf[C_TQ + cur:, :] = kn_ref[0]
    vf[0:C_TQ, :] = vp_ref[0]
    vf[C_TQ:C_TQ + cur, :] = vc_ref[0]
    vf[C_TQ + cur:, :] = vn_ref[0]

    left = lax.broadcasted_iota(jnp.int32, (1, 2 * C_HD), 1) < C_HD
    for sb in range(C_STEP):
        blk = pl.program_id(1) * C_STEP + sb
        q_rows = slice(sb * C_TQ, (sb + 1) * C_TQ)
        k_rows = slice(sb * C_TQ, sb * C_TQ + C_TK)
        key_row = blk * C_QR - C_QR + lax.broadcasted_iota(jnp.int32, (1, C_TK), 1) // GRID_W
        pens = []
        for a in range(C_QR):
            row_start = jnp.clip(blk * C_QR + a - C_KH // 2, 0, rows - C_KH)
            pens.append(jnp.where((key_row >= row_start) & (key_row < row_start + C_KH), 0.0, NEG).astype(F32))

        for hp in range(C_HEADS // 2):
            cols = slice(hp * 2 * C_HD, (hp + 1) * 2 * C_HD)
            qp = q_ref[0, q_rows, cols]
            kp = kf[k_rows, cols]
            probs = []
            for hh, s in enumerate(_pair_scores(qp, kp, left)):
                h = hp * 2 + hh
                rows_p = []
                for a in range(C_QR):
                    bias = jnp.concatenate([tab_ref[h, 2 * t - C_QR - a + C_KH - 1] for t in range(C_PAIRS)], axis=1)
                    sa = s[a * GRID_W:(a + 1) * GRID_W, :] + bias + pens[a]
                    m = jnp.max(sa, axis=-1, keepdims=True)
                    rows_p.append(jnp.exp2(sa - m).astype(BF16))
                probs.append(jnp.concatenate(rows_p, axis=0))
            num, den = _pair_weighted_sum(probs, vf[k_rows, cols], left)
            o_ref[0, q_rows, cols] = (num / den).astype(BF16)


def _neighbourhood_bias(rpb):
    qc = jnp.arange(GRID_W)[:, None]
    kc = jnp.arange(GRID_W)[None, :]
    col_start = jnp.clip(qc - C_KW // 2, 0, GRID_W - C_KW)
    col_ok = (kc >= col_start) & (kc < col_start + C_KW)
    onehot = ((kc - qc + (C_KW - 1))[:, :, None] == jnp.arange(2 * C_KW - 1)[None, None, :]).astype(F32)
    band = jnp.einsum("hrd,qkd->hrqk", rpb.astype(F32), onehot, precision=lax.Precision.HIGHEST)
    band = jnp.where(col_ok[None, None], band * LOG2E, NEG)
    return jnp.concatenate([band[:, :C_NTAB], band[:, 1:C_NTAB + 1]], axis=-1)


def mixer_c(z, batch, seq, rpb):
    rows = seq // GRID_W
    n_blk = rows // C_QR
    zv = z.reshape(batch, seq, MAIN_W)
    halo = (1, C_TQ, HEAD_BLOCK)
    tile = (1, C_STEP * C_TQ, HEAD_BLOCK)
    tab = _neighbourhood_bias(rpb)

    def above(c):
        return lambda b, i: (b, jnp.maximum(i * C_STEP - 1, 0), c)

    def here(c):
        return lambda b, i: (b, i, c)

    def below(c):
        return lambda b, i: (b, jnp.minimum((i + 1) * C_STEP, n_blk - 1), c)

    out = pl.pallas_call(
        functools.partial(_mixer_c_kernel, rows=rows),
        grid=(batch, n_blk // C_STEP),
        in_specs=[pl.BlockSpec(tile, here(CQ_BLK)),
                  pl.BlockSpec(halo, above(CK_BLK)), pl.BlockSpec(tile, here(CK_BLK)), pl.BlockSpec(halo, below(CK_BLK)),
                  pl.BlockSpec(halo, above(CV_BLK)), pl.BlockSpec(tile, here(CV_BLK)), pl.BlockSpec(halo, below(CV_BLK)),
                  _const_spec(tab.shape)],
        out_specs=pl.BlockSpec(tile, lambda b, i: (b, i, 0)),
        out_shape=jax.ShapeDtypeStruct((batch, seq, HEAD_BLOCK), BF16),
        scratch_shapes=[pltpu.VMEM(((C_STEP + 2) * C_TQ, HEAD_BLOCK), BF16),
                        pltpu.VMEM(((C_STEP + 2) * C_TQ, HEAD_BLOCK), BF16)],
        compiler_params=_params(("arbitrary", "arbitrary")),
        name="mixer_c",
    )(zv, zv, zv, zv, zv, zv, zv, tab)
    return out.reshape(batch * seq, HEAD_BLOCK)


def _merge_kernel(x_ref, oa0_ref, oa1_ref, oa2_ref, la0_ref, la1_ref, la2_ref, yb_ref, yc_ref, gates_ref, spread_ref,
                  wa_ref, wb_ref, wc_ref, wo_ref, g_ref, b_ref, o_ref):
    lses = [la0_ref[...], la1_ref[...], la2_ref[...]]
    top = jnp.maximum(jnp.maximum(lses[0], lses[1]), lses[2])
    weights = [jnp.exp2(lse - top) for lse in lses]
    total = weights[0] + weights[1] + weights[2]
    ya = None
    for o_g_ref, w in zip((oa0_ref, oa1_ref, oa2_ref), weights):
        share = w / total
        hi = share.astype(BF16)
        lo = (share - hi.astype(F32)).astype(BF16)
        wide = jnp.dot(jnp.concatenate([hi, lo], axis=1), spread_ref[...], preferred_element_type=F32)
        term = wide * o_g_ref[...].astype(F32)
        ya = term if ya is None else ya + term
    ya = ya.astype(BF16)

    merged = None
    for br, (y, w_ref) in enumerate(((ya, wa_ref), (yb_ref[...], wb_ref), (yc_ref[...], wc_ref))):
        proj = jnp.dot(y, w_ref[...], preferred_element_type=F32)
        gate = jax.nn.sigmoid(gates_ref[:, br * D_MODEL:(br + 1) * D_MODEL].astype(F32))
        merged = gate * proj if merged is None else merged + gate * proj
    merged = merged.astype(BF16)
    for r0 in range(0, x_ref.shape[0], MERGE_ROWS):
        rows = slice(r0, r0 + MERGE_ROWS)
        out = jnp.dot(merged[rows], wo_ref[...], preferred_element_type=F32)
        o_ref[rows, :] = _layer_norm(ALPHA * x_ref[rows, :] + out, g_ref[...], b_ref[...])


def _lse_spread():
    src = jnp.arange(2 * A_HD)[:, None]
    head = jnp.arange(HEAD_BLOCK)[None, :] // A_HD
    first_lane = (head % 2) * A_HD + (head // 2) * LSE_LANES
    once = (src == first_lane).astype(BF16)
    return jnp.concatenate([once, once], axis=0)


def merge_out_ln(x, oa, la, yb, yc, z, wa, wb, wc, wo, g, b):
    t = x.shape[0]
    tm = 1024
    row = pl.BlockSpec((tm, D_MODEL), lambda i: (i, 0))
    br = pl.BlockSpec((tm, HEAD_BLOCK), lambda i: (i, 0))
    lse = pl.BlockSpec((tm, 2 * A_HD), lambda i: (i, 0))
    w_br = _const_spec((HEAD_BLOCK, D_MODEL))
    return pl.pallas_call(
        _merge_kernel,
        grid=(t // tm,),
        in_specs=[row] + [br] * 3 + [lse] * 3 + [br] * 2 + [
            pl.BlockSpec((tm, GATE_W), lambda i: (i, 0)), _const_spec((4 * A_HD, HEAD_BLOCK)), w_br, w_br, w_br,
            _const_spec((D_MODEL, D_MODEL)), _const_spec((1, D_MODEL)), _const_spec((1, D_MODEL))],
        out_specs=row,
        out_shape=jax.ShapeDtypeStruct((t, D_MODEL), F32),
        compiler_params=_params(("arbitrary",)),
        name="merge_out_ln",
    )(x, *oa, *la, yb, yc, z, _lse_spread(), wa, wb, wc, wo, g, b)


def _split_in_weights(w_in):
    a_w = A_GROUPS * HEAD_BLOCK
    aq, ak, av = w_in[:, :a_w] * (A_HD ** -0.5 * LOG2E), w_in[:, a_w:2 * a_w], w_in[:, 2 * a_w:3 * a_w]
    b_w = 2 * B_HEADS * (B_DK + B_DV)
    rest_b = w_in[:, 3 * a_w:3 * a_w + b_w]
    cq = w_in[:, 3 * a_w + b_w:3 * a_w + b_w + HEAD_BLOCK] * (C_HD ** -0.5 * LOG2E)
    rest = jnp.concatenate([rest_b, cq, w_in[:, 3 * a_w + b_w + HEAD_BLOCK:D_IN - GATE_W]], axis=1)

    def group(g):
        cols = slice(g * HEAD_BLOCK, (g + 1) * HEAD_BLOCK)
        return jnp.concatenate([aq[:, cols], ak[:, cols], av[:, cols]], axis=1)

    main = jnp.concatenate([w_in[:, D_IN - GATE_W:], group(0), rest], axis=1)
    return main.astype(BF16), [group(g).astype(BF16) for g in range(1, A_GROUPS)]


def _trunk(x, layers):
    batch, seq, _ = x.shape
    x = x.reshape(batch * seq, D_MODEL)
    for p in layers:
        dils = tuple(d for _, d in A_PATTERNS[1:])
        x1, x1b, *x1_by_residue = ffn_ln(x, p["wg1"], p["wu1"], p["wd1"], p["g1"], p["b1"], batch, dils)
        z = in_proj(x1b, p["w_main"])
        oa, la = [], []
        o, l = mixer_a_group(z.reshape(batch, 1, seq, MAIN_W), 1, AQ_BLK, AK_BLK, AV_BLK)
        oa.append(o.reshape(batch * seq, HEAD_BLOCK))
        la.append(l.reshape(batch * seq, 2 * A_HD))
        for dilation, xg, w_g in zip(dils, x1_by_residue, p["w_groups"]):
            zg = in_proj(xg.reshape(batch * seq, D_MODEL), w_g)
            o, l = mixer_a_group(zg.reshape(batch, dilation, seq // dilation, A_QKV_W), dilation, 0, 1, 2)
            oa.append(_by_token(o, batch, seq))
            la.append(_by_token(l, batch, seq))
        yb = mixer_b(z, batch, seq, p["logit_fwd"], p["logit_bwd"])
        yc = mixer_c(z, batch, seq, p["rpb"])
        x2 = merge_out_ln(x1, oa, la, yb, yc, z, p["wa"], p["wb"], p["wc"], p["wo"], p["g2"], p["b2"])
        (x,) = ffn_ln(x2, p["wg2"], p["wu2"], p["wd2"], p["g3"], p["b3"])
    return x.reshape(batch, seq, D_MODEL)


def kernel(x_prompt, x_sample, ffn1_w_gate, ffn1_w_up, ffn1_w_down, ln1_g, ln1_b, w_in, ret_logit_fwd, ret_logit_bwd, na_rpb, w_branch_a, w_branch_b, w_branch_c, w_out, ln2_g, ln2_b, ffn2_w_gate, ffn2_w_up, ffn2_w_down, ln3_g, ln3_b):
    def vec(v):
        return v.astype(F32).reshape(1, D_MODEL)

    layers = []
    for i in range(DEPTH):
        w_main, w_groups = _split_in_weights(w_in[i])
        layers.append(dict(
            wg1=ffn1_w_gate[i].astype(BF16), wu1=ffn1_w_up[i].astype(BF16), wd1=ffn1_w_down[i].astype(BF16),
            g1=vec(ln1_g[i]), b1=vec(ln1_b[i]),
            w_main=w_main, w_groups=w_groups,
            logit_fwd=ret_logit_fwd[i], logit_bwd=ret_logit_bwd[i], rpb=na_rpb[i],
            wa=w_branch_a[i].astype(BF16), wb=w_branch_b[i].astype(BF16), wc=w_branch_c[i].astype(BF16),
            wo=w_out[i].astype(BF16), g2=vec(ln2_g[i]), b2=vec(ln2_b[i]),
            wg2=ffn2_w_gate[i].astype(BF16), wu2=ffn2_w_up[i].astype(BF16), wd2=ffn2_w_down[i].astype(BF16),
            g3=vec(ln3_g[i]), b3=vec(ln3_b[i])))
    return (_trunk(x_prompt, layers), _trunk(x_sample, layers))
```

```python
import functools

import jax
import jax.numpy as jnp
from jax import lax
from jax.experimental import pallas as pl
from jax.experimental.pallas import tpu as pltpu

F32 = jnp.float32
BF16 = jnp.bfloat16

D_MODEL = 1024
DEPTH = 2
D_FF = 2816
LN_EPS = 1e-5
GN_EPS = 1e-5
ALPHA = (2 * DEPTH) ** 0.25

A_PATTERNS = ((128, 1), (512, 4), (2048, 16))
A_GROUPS = len(A_PATTERNS)
A_HEADS = 8
A_HD = 64
A_HALF = 64
LSE_LANES = A_HD // (A_HEADS // 2)
B_HEADS = 4
B_DK = 64
B_DV = 128
B_CHUNK = 128
B_BLOCK = 8
C_HEADS = 8
C_HD = 64
GRID_W = 64
C_KH = 8
C_KW = 16
C_QR = 4
C_STEP = 2
D_IN = 10752
NEG = -1e30
LANES = 128
LOG2E = 1.4426950408889634

GATE_W = 3 * D_MODEL
HEAD_BLOCK = 512
A_QKV_W = 3 * HEAD_BLOCK
MAIN_W = D_IN - (A_GROUPS - 1) * A_QKV_W
A_QKV_BLK = 2
B_QKVG_BLK = 3
C_QKV_BLK = 4

VMEM_LIMIT = 56 * 1024 * 1024
FF_CHUNKS = ((0, 512), (512, 1024), (1024, 1536), (1536, 2048), (2048, 2560), (2560, 2816))
FF_NORM_ROWS = 256
MERGE_ROWS = 256


def _params(sem):
    return pltpu.CompilerParams(dimension_semantics=sem, vmem_limit_bytes=VMEM_LIMIT)


def _const_spec(shape):
    zeros = (0,) * len(shape)
    return pl.BlockSpec(shape, lambda *_: zeros)


def _layer_norm(r, g, b):
    mu = jnp.mean(r, axis=-1, keepdims=True)
    c = r - mu
    var = jnp.mean(c * c, axis=-1, keepdims=True)
    return c * lax.rsqrt(var + LN_EPS) * g + b


def _ffn_ln_kernel(x_ref, wg_ref, wu_ref, wd_ref, g_ref, b_ref, o_ref, *rest, dilations):
    x = x_ref[...]
    xb = x.astype(BF16)
    hidden = []
    for c0, c1 in FF_CHUNKS:
        gate = jnp.dot(xb, wg_ref[:, c0:c1], preferred_element_type=F32)
        up = jnp.dot(xb, wu_ref[:, c0:c1], preferred_element_type=F32)
        hidden.append((gate * jax.nn.sigmoid(gate) * up).astype(BF16))
    hidden = jnp.concatenate(hidden, axis=1)
    if rest:
        ob_ref, *og_refs, slabs = rest
    n_slabs = D_MODEL // LANES
    rows = FF_NORM_ROWS
    for r0 in range(0, x.shape[0], rows):
        acc = jnp.dot(hidden[r0:r0 + rows], wd_ref[...], preferred_element_type=F32)
        y = _layer_norm(ALPHA * x[r0:r0 + rows] + 0.5 * acc, g_ref[...], b_ref[...])
        o_ref[r0:r0 + rows, :] = y
        if not rest:
            continue
        ob_ref[r0:r0 + rows, :] = y.astype(BF16)
        done = 1
        ordered = y
        for d, og_ref in zip(dilations, og_refs):
            for s in range(n_slabs):
                slabs[s] = ordered[:, s * LANES:(s + 1) * LANES]
            step = d // done
            group = rows // done
            pieces = []
            for r in range(d):
                first = (r % done) * group + r // done
                picked = [slabs[s, pl.ds(first, rows // d, stride=step), :] for s in range(n_slabs)]
                pieces.append(jnp.concatenate(picked, axis=1))
                og_ref[0, r, r0 // d:(r0 + rows) // d, :] = pieces[-1].astype(BF16)
            ordered = jnp.concatenate(pieces, axis=0)
            done = d


def ffn_ln(x, wg, wu, wd, g, b, batch=None, dilations=()):
    t = x.shape[0]
    tm = 1024
    row = pl.BlockSpec((tm, D_MODEL), lambda i: (i, 0))
    out_shape = [jax.ShapeDtypeStruct((t, D_MODEL), F32)]
    out_specs = [row]
    scratch = []
    if dilations:
        assert all(b % a == 0 for a, b in zip((1,) + dilations, dilations)), dilations
        tiles_per_seq = t // batch // tm
        out_shape.append(jax.ShapeDtypeStruct((t, D_MODEL), BF16))
        out_specs.append(row)
        for d in dilations:
            out_shape.append(jax.ShapeDtypeStruct((batch, d, t // batch // d, D_MODEL), BF16))
            out_specs.append(pl.BlockSpec((1, d, tm // d, D_MODEL),
                                          lambda i: (i // tiles_per_seq, 0, i % tiles_per_seq, 0)))
        scratch = [pltpu.VMEM((D_MODEL // LANES, FF_NORM_ROWS, LANES), F32)]
    return pl.pallas_call(
        functools.partial(_ffn_ln_kernel, dilations=dilations),
        grid=(t // tm,),
        in_specs=[row, _const_spec((D_MODEL, D_FF)), _const_spec((D_MODEL, D_FF)), _const_spec((D_FF, D_MODEL)),
                  _const_spec((1, D_MODEL)), _const_spec((1, D_MODEL))],
        out_specs=out_specs,
        out_shape=out_shape,
        scratch_shapes=scratch,
        compiler_params=_params(("arbitrary",)),
        name="ffn_ln",
    )(x, wg, wu, wd, g, b)


def _in_proj_kernel(x_ref, w_ref, z_ref):
    z_ref[...] = jnp.dot(x_ref[...], w_ref[...], preferred_element_type=F32).astype(BF16)


def in_proj(xb, w_in):
    t = xb.shape[0]
    width = w_in.shape[1]
    tm = 2048
    tn = 2560 if width % 2560 == 0 else A_QKV_W
    return pl.pallas_call(
        _in_proj_kernel,
        grid=(t // tm, width // tn),
        in_specs=[pl.BlockSpec((tm, D_MODEL), lambda i, j: (i, 0)), pl.BlockSpec((D_MODEL, tn), lambda i, j: (0, j))],
        out_specs=pl.BlockSpec((tm, tn), lambda i, j: (i, j)),
        out_shape=jax.ShapeDtypeStruct((t, width), BF16),
        compiler_params=_params(("arbitrary", "arbitrary")),
        name="in_proj",
    )(xb, w_in)


def _pair_scores(q_pair, k_pair, left):
    zero = jnp.zeros_like(q_pair)
    dims = (((1,), (1,)), ((), ()))
    return [lax.dot_general(jnp.where(left, q_pair, zero), k_pair, dims, preferred_element_type=F32),
            lax.dot_general(jnp.where(left, zero, q_pair), k_pair, dims, preferred_element_type=F32)]


def _pair_weighted_sum(probs, v_pair, left):
    rhs = []
    for hh in range(2):
        sel = left if hh == 0 else jnp.logical_not(left)
        ones = jnp.broadcast_to(jnp.where(sel, 1.0, 0.0).astype(BF16), v_pair.shape)
        rhs.append(jnp.concatenate([jnp.where(sel, v_pair, jnp.zeros_like(v_pair)), ones], axis=1))
    out = jnp.dot(jnp.concatenate(probs, axis=1), jnp.concatenate(rhs, axis=0), preferred_element_type=F32)
    width = v_pair.shape[1]
    return out[:, :width], out[:, width:]


A_QB = 128
A_KB = A_QB + 2 * A_HALF


def _mixer_a_kernel(before_ref, cur_ref, after_ref, bias_ref, o_ref, l_ref, kf, vf, *, tl, seq_len):
    i = pl.program_id(2)
    k_cols = slice(HEAD_BLOCK, 2 * HEAD_BLOCK)
    v_cols = slice(2 * HEAD_BLOCK, 3 * HEAD_BLOCK)
    kf[0:A_HALF, :] = before_ref[0, 0, :, k_cols]
    kf[A_HALF:A_HALF + tl, :] = cur_ref[0, 0, :, k_cols]
    kf[A_HALF + tl:, :] = after_ref[0, 0, :, k_cols]
    vf[0:A_HALF, :] = before_ref[0, 0, :, v_cols]
    vf[A_HALF:A_HALF + tl, :] = cur_ref[0, 0, :, v_cols]
    vf[A_HALF + tl:, :] = after_ref[0, 0, :, v_cols]

    lane = lax.broadcasted_iota(jnp.int32, (1, 2 * A_HD), 1)
    left = lane < A_HD
    lse_slot = (lane % A_HD) // LSE_LANES

    def block(j, carry):
        q0 = pl.multiple_of(j * A_QB, A_QB)
        start = i * tl + q0
        variant = (start == 0).astype(jnp.int32) + 2 * (start + A_QB == seq_len).astype(jnp.int32)
        lse_all = None
        for hp in range(A_HEADS // 2):
            cols = slice(hp * 2 * A_HD, (hp + 1) * 2 * A_HD)
            qp = cur_ref[0, 0, pl.ds(q0, A_QB), cols]
            kp = kf[pl.ds(q0, A_KB), cols]
            probs, tops = [], []
            for hh, s in enumerate(_pair_scores(qp, kp, left)):
                s = s + bias_ref[variant, hp * 2 + hh]
                m = jnp.max(s, axis=-1, keepdims=True)
                probs.append(jnp.exp2(s - m).astype(BF16))
                tops.append(m)
            num, den = _pair_weighted_sum(probs, vf[pl.ds(q0, A_KB), cols], left)
            o_ref[0, 0, pl.ds(q0, A_QB), cols] = (num / den).astype(BF16)
            lse_pair = jnp.where(left, tops[0], tops[1]) + jnp.log2(den)
            lse_all = lse_pair if hp == 0 else jnp.where(lse_slot == hp, lse_pair, lse_all)
        l_ref[0, 0, pl.ds(q0, A_QB), :] = lse_all
        return carry

    lax.fori_loop(0, tl // A_QB, block, 0, unroll=True)


def _alibi_bias(dilation):
    slopes = 2.0 ** (-8.0 * jnp.arange(1, A_HEADS + 1, dtype=F32) / A_HEADS)
    key = jnp.arange(A_KB)[None, :] - A_HALF
    rel = key - jnp.arange(A_QB)[:, None]
    dist = (jnp.abs(rel) * dilation).astype(F32)
    bias = -slopes[:, None, None] * dist[None] * LOG2E
    in_window = jnp.abs(rel) <= A_HALF
    variants = []
    for v in range(4):
        ok = in_window
        if v & 1:
            ok = ok & (key >= 0)
        if v & 2:
            ok = ok & (key < A_QB)
        variants.append(jnp.where(ok[None], bias, NEG))
    return jnp.stack(variants, 0)


def mixer_a_group(zg, dilation, qkv_blk):
    batch, _, strided_len, _ = zg.shape
    tl = min(1024, strided_len)
    halo_per_tile = tl // A_HALF
    n_halo = strided_len // A_HALF
    tile = (1, 1, tl, HEAD_BLOCK)
    out_dims = (batch, dilation, strided_len, HEAD_BLOCK)
    return pl.pallas_call(
        functools.partial(_mixer_a_kernel, tl=tl, seq_len=strided_len),
        grid=(batch, dilation, strided_len // tl),
        in_specs=[pl.BlockSpec((1, 1, A_HALF, A_QKV_W),
                               lambda b, r, i: (b, r, jnp.maximum(i * halo_per_tile - 1, 0), qkv_blk)),
                  pl.BlockSpec((1, 1, tl, A_QKV_W), lambda b, r, i: (b, r, i, qkv_blk)),
                  pl.BlockSpec((1, 1, A_HALF, A_QKV_W),
                               lambda b, r, i: (b, r, jnp.minimum((i + 1) * halo_per_tile, n_halo - 1), qkv_blk)),
                  _const_spec((4, A_HEADS, A_QB, A_KB))],
        out_specs=[pl.BlockSpec(tile, lambda b, r, i: (b, r, i, 0)),
                   pl.BlockSpec((1, 1, tl, 2 * A_HD), lambda b, r, i: (b, r, i, 0))],
        out_shape=[jax.ShapeDtypeStruct(out_dims, BF16),
                   jax.ShapeDtypeStruct((batch, dilation, strided_len, 2 * A_HD), F32)],
        scratch_shapes=[pltpu.VMEM((tl + 2 * A_HALF, HEAD_BLOCK), BF16), pltpu.VMEM((tl + 2 * A_HALF, HEAD_BLOCK), BF16)],
        compiler_params=_params(("arbitrary", "arbitrary", "arbitrary")),
        name=f"mixer_a_d{dilation}",
    )(zg, zg, zg, _alibi_bias(dilation))


def _by_token(x, batch, seq):
    return x.transpose(0, 2, 1, 3).reshape(batch * seq, x.shape[-1])


def _mixer_b_kernel(qkvg_ref, dmat_ref, qdf_ref, qdb_ref, kdf_ref, kdb_ref, cdf_ref, cdb_ref,
                    o_ref, fwd_state, bwd_state, bwd_store, *, n_blocks):
    phase = pl.program_id(1)
    n = pl.program_id(2)
    qk_w = B_HEADS * B_DK
    v_w = B_HEADS * B_DV
    head_of_lane = lax.broadcasted_iota(jnp.int32, (1, qk_w), 1) // B_DK

    def rows(c):
        return slice(c * B_CHUNK, (c + 1) * B_CHUNK)

    def stacked_heads(t):
        lane_head = jnp.concatenate([head_of_lane] * (t.shape[1] // qk_w), axis=1)
        return jnp.concatenate([jnp.where(lane_head == h, t, jnp.zeros_like(t)) for h in range(B_HEADS)], axis=0)

    def head_block_diagonal(states):
        row_head = (lax.broadcasted_iota(jnp.int32, (states.shape[0], 1), 0) % qk_w) // B_DK
        return jnp.concatenate([jnp.where(row_head == h, states, jnp.zeros_like(states)) for h in range(B_HEADS)],
                               axis=1)

    def kv_outer(k_decayed, v):
        full = lax.dot_general(k_decayed, v, (((0,), (0,)), ((), ())), preferred_element_type=F32)
        return jnp.concatenate([full[h * B_DK:(h + 1) * B_DK, h * B_DV:(h + 1) * B_DV] for h in range(B_HEADS)], axis=0)

    def scaled_k(c):
        return qkvg_ref[0, rows(c), qk_w:2 * qk_w] * (B_DK ** -0.5)

    @pl.when(phase == 0)
    def _():
        @pl.when(n == 0)
        def _():
            bwd_state[...] = jnp.zeros_like(bwd_state)

        blk = n_blocks - 1 - n
        state = bwd_state[...]
        for c in reversed(range(B_BLOCK)):
            bwd_store[blk * B_BLOCK + c] = state.astype(BF16)
            k_dec = (scaled_k(c).astype(F32) * kdb_ref[...]).astype(BF16)
            state = cdb_ref[...] * state + kv_outer(k_dec, qkvg_ref[0, rows(c), 2 * qk_w:2 * qk_w + v_w])
        bwd_state[...] = state

    @pl.when(phase == 1)
    def _():
        @pl.when(n == 0)
        def _():
            fwd_state[...] = jnp.zeros_like(fwd_state)

        state = fwd_state[...]
        for c in range(B_BLOCK):
            q = qkvg_ref[0, rows(c), 0:qk_w]
            k = scaled_k(c)
            v = qkvg_ref[0, rows(c), 2 * qk_w:2 * qk_w + v_w]
            q32 = q.astype(F32)
            q_dec = jnp.concatenate([(q32 * qdf_ref[...]).astype(BF16), (q32 * qdb_ref[...]).astype(BF16)], axis=1)
            k_dec = (k.astype(F32) * kdf_ref[...]).astype(BF16)
            states = jnp.concatenate([state.astype(BF16), bwd_store[n * B_BLOCK + c]], axis=0)
            s_all = lax.dot_general(q, stacked_heads(k), (((1,), (1,)), ((), ())), preferred_element_type=F32)
            cross_all = jnp.dot(q_dec, head_block_diagonal(states), preferred_element_type=F32)
            for h in range(B_HEADS):
                vh = v[:, h * B_DV:(h + 1) * B_DV]
                inner = jnp.dot((s_all[:, rows(h)] * dmat_ref[h]).astype(BF16), vh, preferred_element_type=F32)
                y = inner + cross_all[:, h * B_DV:(h + 1) * B_DV]
                mu = jnp.mean(y, axis=-1, keepdims=True)
                cen = y - mu
                var = jnp.mean(cen * cen, axis=-1, keepdims=True)
                yn = cen * lax.rsqrt(var + GN_EPS)
                g0 = 2 * qk_w + v_w + h * B_DV
                gate = qkvg_ref[0, rows(c), g0:g0 + B_DV].astype(F32)
                o_ref[0, rows(c), h * B_DV:(h + 1) * B_DV] = (gate * jax.nn.sigmoid(gate) * yn).astype(BF16)
            state = cdf_ref[...] * state + kv_outer(k_dec, v)
        fwd_state[...] = state


def _retention_tables(logit_fwd, logit_bwd):
    lg_f = jax.nn.log_sigmoid(logit_fwd.astype(F32))
    lg_b = jax.nn.log_sigmoid(logit_bwd.astype(F32))
    idx = jnp.arange(B_CHUNK, dtype=F32)
    diff = idx[:, None] - idx[None, :]
    causal = diff >= 0
    dmat = jnp.where(causal[None],
                     jnp.exp(lg_f[:, None, None] * jnp.where(causal, diff, 0.0)[None]),
                     jnp.exp(lg_b[:, None, None] * jnp.where(causal, 0.0, -diff)[None]))

    def per_lane(lg, power):
        return jnp.repeat(jnp.exp(lg[None, :] * power[:, None]), B_DK, axis=1)

    def per_row(lg):
        return jnp.broadcast_to(jnp.repeat(jnp.exp(lg * B_CHUNK), B_DK)[:, None], (B_HEADS * B_DK, B_DV))

    return (dmat, per_lane(lg_f, idx + 1), per_lane(lg_b, B_CHUNK - idx), per_lane(lg_f, B_CHUNK - 1 - idx),
            per_lane(lg_b, idx), per_row(lg_f), per_row(lg_b))


def mixer_b(z, batch, seq, logit_fwd, logit_bwd):
    n_chunks = seq // B_CHUNK
    n_blocks = n_chunks // B_BLOCK
    block_rows = B_BLOCK * B_CHUNK
    zv = z.reshape(batch, seq, MAIN_W)
    qk_w = B_HEADS * B_DK
    v_w = B_HEADS * B_DV

    def scan_block(ph, n):
        return (1 - ph) * (n_blocks - 1 - n) + ph * n

    tables = _retention_tables(logit_fwd, logit_bwd)
    in_specs = [pl.BlockSpec((1, block_rows, 2 * (qk_w + v_w)), lambda b, ph, n: (b, scan_block(ph, n), B_QKVG_BLK))]
    in_specs += [_const_spec(t.shape) for t in tables]
    out = pl.pallas_call(
        functools.partial(_mixer_b_kernel, n_blocks=n_blocks),
        grid=(batch, 2, n_blocks),
        in_specs=in_specs,
        out_specs=pl.BlockSpec((1, block_rows, v_w), lambda b, ph, n: (b, ph * n, 0)),
        out_shape=jax.ShapeDtypeStruct((batch, seq, v_w), BF16),
        scratch_shapes=[pltpu.VMEM((qk_w, B_DV), F32), pltpu.VMEM((qk_w, B_DV), F32),
                        pltpu.VMEM((n_chunks, qk_w, B_DV), BF16)],
        compiler_params=_params(("arbitrary", "arbitrary", "arbitrary")),
        name="mixer_b",
    )(zv, *tables)
    return out.reshape(batch * seq, v_w)


C_TQ = C_QR * GRID_W
C_KROWS = 3 * C_QR
C_TK = C_KROWS * GRID_W
C_PAIRS = C_KROWS // 2
C_NTAB = 2 * C_KH - 2


def _mixer_c_kernel(above_ref, cur_ref, below_ref, tab_ref, o_ref, kf, vf, *, rows):
    cur = C_STEP * C_TQ
    k_cols = slice(HEAD_BLOCK, 2 * HEAD_BLOCK)
    v_cols = slice(2 * HEAD_BLOCK, 3 * HEAD_BLOCK)
    kf[0:C_TQ, :] = above_ref[0, :, k_cols]
    kf[C_TQ:C_TQ + cur, :] = cur_ref[0, :, k_cols]
    kf[C_TQ + cur:, :] = below_ref[0, :, k_cols]
    vf[0:C_TQ, :] = above_ref[0, :, v_cols]
    vf[C_TQ:C_TQ + cur, :] = cur_ref[0, :, v_cols]
    vf[C_TQ + cur:, :] = below_ref[0, :, v_cols]

    left = lax.broadcasted_iota(jnp.int32, (1, 2 * C_HD), 1) < C_HD
    for sb in range(C_STEP):
        blk = pl.program_id(1) * C_STEP + sb
        q_rows = slice(sb * C_TQ, (sb + 1) * C_TQ)
        k_rows = slice(sb * C_TQ, sb * C_TQ + C_TK)
        key_row = blk * C_QR - C_QR + lax.broadcasted_iota(jnp.int32, (1, C_TK), 1) // GRID_W
        pens = []
        for a in range(C_QR):
            row_start = jnp.clip(blk * C_QR + a - C_KH // 2, 0, rows - C_KH)
            pens.append(jnp.where((key_row >= row_start) & (key_row < row_start + C_KH), 0.0, NEG).astype(F32))

        for hp in range(C_HEADS // 2):
            cols = slice(hp * 2 * C_HD, (hp + 1) * 2 * C_HD)
            qp = cur_ref[0, q_rows, cols]
            kp = kf[k_rows, cols]
            probs = []
            for hh, s in enumerate(_pair_scores(qp, kp, left)):
                h = hp * 2 + hh
                rows_p = []
                for a in range(C_QR):
                    bias = jnp.concatenate([tab_ref[h, 2 * t - C_QR - a + C_KH - 1] for t in range(C_PAIRS)], axis=1)
                    sa = s[a * GRID_W:(a + 1) * GRID_W, :] + bias + pens[a]
                    m = jnp.max(sa, axis=-1, keepdims=True)
                    rows_p.append(jnp.exp2(sa - m).astype(BF16))
                probs.append(jnp.concatenate(rows_p, axis=0))
            num, den = _pair_weighted_sum(probs, vf[k_rows, cols], left)
            o_ref[0, q_rows, cols] = (num / den).astype(BF16)


def _neighbourhood_bias(rpb):
    qc = jnp.arange(GRID_W)[:, None]
    kc = jnp.arange(GRID_W)[None, :]
    col_start = jnp.clip(qc - C_KW // 2, 0, GRID_W - C_KW)
    col_ok = (kc >= col_start) & (kc < col_start + C_KW)
    onehot = ((kc - qc + (C_KW - 1))[:, :, None] == jnp.arange(2 * C_KW - 1)[None, None, :]).astype(F32)
    band = jnp.einsum("hrd,qkd->hrqk", rpb.astype(F32), onehot, precision=lax.Precision.HIGHEST)
    band = jnp.where(col_ok[None, None], band * LOG2E, NEG)
    return jnp.concatenate([band[:, :C_NTAB], band[:, 1:C_NTAB + 1]], axis=-1)


def mixer_c(z, batch, seq, rpb):
    rows = seq // GRID_W
    n_blk = rows // C_QR
    zv = z.reshape(batch, seq, MAIN_W)
    tab = _neighbourhood_bias(rpb)
    halo = (1, C_TQ, A_QKV_W)
    tile = (1, C_STEP * C_TQ, A_QKV_W)
    out = pl.pallas_call(
        functools.partial(_mixer_c_kernel, rows=rows),
        grid=(batch, n_blk // C_STEP),
        in_specs=[pl.BlockSpec(halo, lambda b, i: (b, jnp.maximum(i * C_STEP - 1, 0), C_QKV_BLK)),
                  pl.BlockSpec(tile, lambda b, i: (b, i, C_QKV_BLK)),
                  pl.BlockSpec(halo, lambda b, i: (b, jnp.minimum((i + 1) * C_STEP, n_blk - 1), C_QKV_BLK)),
                  _const_spec(tab.shape)],
        out_specs=pl.BlockSpec((1, C_STEP * C_TQ, HEAD_BLOCK), lambda b, i: (b, i, 0)),
        out_shape=jax.ShapeDtypeStruct((batch, seq, HEAD_BLOCK), BF16),
        scratch_shapes=[pltpu.VMEM(((C_STEP + 2) * C_TQ, HEAD_BLOCK), BF16),
                        pltpu.VMEM(((C_STEP + 2) * C_TQ, HEAD_BLOCK), BF16)],
        compiler_params=_params(("arbitrary", "arbitrary")),
        name="mixer_c",
    )(zv, zv, zv, tab)
    return out.reshape(batch * seq, HEAD_BLOCK)


def _merge_kernel(x_ref, oa0_ref, oa1_ref, oa2_ref, la0_ref, la1_ref, la2_ref, yb_ref, yc_ref, gates_ref, spread_ref,
                  wa_ref, wb_ref, wc_ref, wo_ref, g_ref, b_ref, o_ref):
    lses = [la0_ref[...], la1_ref[...], la2_ref[...]]
    top = jnp.maximum(jnp.maximum(lses[0], lses[1]), lses[2])
    weights = [jnp.exp2(lse - top) for lse in lses]
    total = weights[0] + weights[1] + weights[2]
    ya = None
    for o_g_ref, w in zip((oa0_ref, oa1_ref, oa2_ref), weights):
        share = w / total
        hi = share.astype(BF16)
        lo = (share - hi.astype(F32)).astype(BF16)
        wide = jnp.dot(jnp.concatenate([hi, lo], axis=1), spread_ref[...], preferred_element_type=F32)
        term = wide * o_g_ref[...].astype(F32)
        ya = term if ya is None else ya + term
    ya = ya.astype(BF16)

    merged = None
    for br, (y, w_ref) in enumerate(((ya, wa_ref), (yb_ref[...], wb_ref), (yc_ref[...], wc_ref))):
        proj = jnp.dot(y, w_ref[...], preferred_element_type=F32)
        gate = jax.nn.sigmoid(gates_ref[:, br * D_MODEL:(br + 1) * D_MODEL].astype(F32))
        merged = gate * proj if merged is None else merged + gate * proj
    merged = merged.astype(BF16)
    for r0 in range(0, x_ref.shape[0], MERGE_ROWS):
        rows = slice(r0, r0 + MERGE_ROWS)
        out = jnp.dot(merged[rows], wo_ref[...], preferred_element_type=F32)
        o_ref[rows, :] = _layer_norm(ALPHA * x_ref[rows, :] + out, g_ref[...], b_ref[...])


def _lse_spread():
    src = jnp.arange(2 * A_HD)[:, None]
    head = jnp.arange(HEAD_BLOCK)[None, :] // A_HD
    first_lane = (head % 2) * A_HD + (head // 2) * LSE_LANES
    once = (src == first_lane).astype(BF16)
    return jnp.concatenate([once, once], axis=0)


def merge_out_ln(x, oa, la, yb, yc, z, wa, wb, wc, wo, g, b):
    t = x.shape[0]
    tm = 1024
    row = pl.BlockSpec((tm, D_MODEL), lambda i: (i, 0))
    br = pl.BlockSpec((tm, HEAD_BLOCK), lambda i: (i, 0))
    lse = pl.BlockSpec((tm, 2 * A_HD), lambda i: (i, 0))
    w_br = _const_spec((HEAD_BLOCK, D_MODEL))
    return pl.pallas_call(
        _merge_kernel,
        grid=(t // tm,),
        in_specs=[row] + [br] * 3 + [lse] * 3 + [br] * 2 + [
            pl.BlockSpec((tm, GATE_W), lambda i: (i, 0)), _const_spec((4 * A_HD, HEAD_BLOCK)), w_br, w_br, w_br,
            _const_spec((D_MODEL, D_MODEL)), _const_spec((1, D_MODEL)), _const_spec((1, D_MODEL))],
        out_specs=row,
        out_shape=jax.ShapeDtypeStruct((t, D_MODEL), F32),
        compiler_params=_params(("arbitrary",)),
        name="merge_out_ln",
    )(x, *oa, *la, yb, yc, z, _lse_spread(), wa, wb, wc, wo, g, b)


def _split_in_weights(w_in):
    a_w = A_GROUPS * HEAD_BLOCK
    aq, ak, av = w_in[:, :a_w] * (A_HD ** -0.5 * LOG2E), w_in[:, a_w:2 * a_w], w_in[:, 2 * a_w:3 * a_w]
    b_w = 2 * B_HEADS * (B_DK + B_DV)
    rest_b = w_in[:, 3 * a_w:3 * a_w + b_w]
    cq = w_in[:, 3 * a_w + b_w:3 * a_w + b_w + HEAD_BLOCK] * (C_HD ** -0.5 * LOG2E)
    rest = jnp.concatenate([rest_b, cq, w_in[:, 3 * a_w + b_w + HEAD_BLOCK:D_IN - GATE_W]], axis=1)

    def group(g):
        cols = slice(g * HEAD_BLOCK, (g + 1) * HEAD_BLOCK)
        return jnp.concatenate([aq[:, cols], ak[:, cols], av[:, cols]], axis=1)

    main = jnp.concatenate([w_in[:, D_IN - GATE_W:], group(0), rest], axis=1)
    return main.astype(BF16), [group(g).astype(BF16) for g in range(1, A_GROUPS)]


def _trunk(x, layers):
    batch, seq, _ = x.shape
    x = x.reshape(batch * seq, D_MODEL)
    for p in layers:
        dils = tuple(d for _, d in A_PATTERNS[1:])
        x1, x1b, *x1_by_residue = ffn_ln(x, p["wg1"], p["wu1"], p["wd1"], p["g1"], p["b1"], batch, dils)
        z = in_proj(x1b, p["w_main"])
        oa, la = [], []
        o, l = mixer_a_group(z.reshape(batch, 1, seq, MAIN_W), 1, A_QKV_BLK)
        oa.append(o.reshape(batch * seq, HEAD_BLOCK))
        la.append(l.reshape(batch * seq, 2 * A_HD))
        for dilation, xg, w_g in zip(dils, x1_by_residue, p["w_groups"]):
            zg = in_proj(xg.reshape(batch * seq, D_MODEL), w_g)
            o, l = mixer_a_group(zg.reshape(batch, dilation, seq // dilation, A_QKV_W), dilation, 0)
            oa.append(_by_token(o, batch, seq))
            la.append(_by_token(l, batch, seq))
        yb = mixer_b(z, batch, seq, p["logit_fwd"], p["logit_bwd"])
        yc = mixer_c(z, batch, seq, p["rpb"])
        x2 = merge_out_ln(x1, oa, la, yb, yc, z, p["wa"], p["wb"], p["wc"], p["wo"], p["g2"], p["b2"])
        (x,) = ffn_ln(x2, p["wg2"], p["wu2"], p["wd2"], p["g3"], p["b3"])
    return x.reshape(batch, seq, D_MODEL)


def kernel(x_prompt, x_sample, ffn1_w_gate, ffn1_w_up, ffn1_w_down, ln1_g, ln1_b, w_in, ret_logit_fwd, ret_logit_bwd, na_rpb, w_branch_a, w_branch_b, w_branch_c, w_out, ln2_g, ln2_b, ffn2_w_gate, ffn2_w_up, ffn2_w_down, ln3_g, ln3_b):
    def vec(v):
        return v.astype(F32).reshape(1, D_MODEL)

    layers = []
    for i in range(DEPTH):
        w_main, w_groups = _split_in_weights(w_in[i])
        layers.append(dict(
            wg1=ffn1_w_gate[i].astype(BF16), wu1=ffn1_w_up[i].astype(BF16), wd1=ffn1_w_down[i].astype(BF16),
            g1=vec(ln1_g[i]), b1=vec(ln1_b[i]),
            w_main=w_main, w_groups=w_groups,
            logit_fwd=ret_logit_fwd[i], logit_bwd=ret_logit_bwd[i], rpb=na_rpb[i],
            wa=w_branch_a[i].astype(BF16), wb=w_branch_b[i].astype(BF16), wc=w_branch_c[i].astype(BF16),
            wo=w_out[i].astype(BF16), g2=vec(ln2_g[i]), b2=vec(ln2_b[i]),
            wg2=ffn2_w_gate[i].astype(BF16), wu2=ffn2_w_up[i].astype(BF16), wd2=ffn2_w_down[i].astype(BF16),
            g3=vec(ln3_g[i]), b3=vec(ln3_b[i])))
    return (_trunk(x_prompt, layers), _trunk(x_sample, layers))
```

```python
import functools

import jax
import jax.numpy as jnp
from jax import lax
from jax.experimental import pallas as pl
from jax.experimental.pallas import tpu as pltpu

F32 = jnp.float32
BF16 = jnp.bfloat16

D_MODEL = 1024
DEPTH = 2
D_FF = 2816
LN_EPS = 1e-5
GN_EPS = 1e-5
ALPHA = (2 * DEPTH) ** 0.25

A_PATTERNS = ((128, 1), (512, 4), (2048, 16))
A_GROUPS = len(A_PATTERNS)
A_HEADS = 8
A_HD = 64
A_HALF = 64
LSE_LANES = A_HD // (A_HEADS // 2)
B_HEADS = 4
B_DK = 64
B_DV = 128
B_CHUNK = 128
B_BLOCK = 8
C_HEADS = 8
C_HD = 64
GRID_W = 64
C_KH = 8
C_KW = 16
C_QR = 4
C_STEP = 4
D_IN = 10752
NEG = -1e30
LANES = 128
LOG2E = 1.4426950408889634

GATE_W = 3 * D_MODEL
HEAD_BLOCK = 512
A_QKV_W = 3 * HEAD_BLOCK
MAIN_W = D_IN - (A_GROUPS - 1) * A_QKV_W
A_QKV_BLK = 2
B_QKVG_BLK = 3
C_QKV_BLK = 4

VMEM_LIMIT = 56 * 1024 * 1024
FF_CHUNKS = ((0, 512), (512, 1024), (1024, 1536), (1536, 2048), (2048, 2560), (2560, 2816))
FF_NORM_ROWS = 256
MERGE_ROWS = 256


def _params(sem):
    return pltpu.CompilerParams(dimension_semantics=sem, vmem_limit_bytes=VMEM_LIMIT)


def _const_spec(shape):
    zeros = (0,) * len(shape)
    return pl.BlockSpec(shape, lambda *_: zeros)


def _layer_norm(r, g, b):
    mu = jnp.mean(r, axis=-1, keepdims=True)
    c = r - mu
    var = jnp.mean(c * c, axis=-1, keepdims=True)
    return c * lax.rsqrt(var + LN_EPS) * g + b


def _ffn_ln_kernel(x_ref, wg_ref, wu_ref, wd_ref, g_ref, b_ref, o_ref, *rest, dilations):
    x = x_ref[...]
    xb = x.astype(BF16)
    hidden = []
    for c0, c1 in FF_CHUNKS:
        gate = jnp.dot(xb, wg_ref[:, c0:c1], preferred_element_type=F32)
        up = jnp.dot(xb, wu_ref[:, c0:c1], preferred_element_type=F32)
        hidden.append((gate * jax.nn.sigmoid(gate) * up).astype(BF16))
    hidden = jnp.concatenate(hidden, axis=1)
    if rest:
        ob_ref, *og_refs, slabs = rest
    n_slabs = D_MODEL // LANES
    rows = FF_NORM_ROWS
    for r0 in range(0, x.shape[0], rows):
        acc = jnp.dot(hidden[r0:r0 + rows], wd_ref[...], preferred_element_type=F32)
        y = _layer_norm(ALPHA * x[r0:r0 + rows] + 0.5 * acc, g_ref[...], b_ref[...])
        o_ref[r0:r0 + rows, :] = y
        if not rest:
            continue
        ob_ref[r0:r0 + rows, :] = y.astype(BF16)
        done = 1
        ordered = y
        for d, og_ref in zip(dilations, og_refs):
            for s in range(n_slabs):
                slabs[s] = ordered[:, s * LANES:(s + 1) * LANES]
            step = d // done
            group = rows // done
            pieces = []
            for r in range(d):
                first = (r % done) * group + r // done
                picked = [slabs[s, pl.ds(first, rows // d, stride=step), :] for s in range(n_slabs)]
                pieces.append(jnp.concatenate(picked, axis=1))
                og_ref[0, r, r0 // d:(r0 + rows) // d, :] = pieces[-1].astype(BF16)
            ordered = jnp.concatenate(pieces, axis=0)
            done = d


def ffn_ln(x, wg, wu, wd, g, b, batch=None, dilations=()):
    t = x.shape[0]
    tm = 1024
    row = pl.BlockSpec((tm, D_MODEL), lambda i: (i, 0))
    out_shape = [jax.ShapeDtypeStruct((t, D_MODEL), F32)]
    out_specs = [row]
    scratch = []
    if dilations:
        assert all(b % a == 0 for a, b in zip((1,) + dilations, dilations)), dilations
        tiles_per_seq = t // batch // tm
        out_shape.append(jax.ShapeDtypeStruct((t, D_MODEL), BF16))
        out_specs.append(row)
        for d in dilations:
            out_shape.append(jax.ShapeDtypeStruct((batch, d, t // batch // d, D_MODEL), BF16))
            out_specs.append(pl.BlockSpec((1, d, tm // d, D_MODEL),
                                          lambda i: (i // tiles_per_seq, 0, i % tiles_per_seq, 0)))
        scratch = [pltpu.VMEM((D_MODEL // LANES, FF_NORM_ROWS, LANES), F32)]
    return pl.pallas_call(
        functools.partial(_ffn_ln_kernel, dilations=dilations),
        grid=(t // tm,),
        in_specs=[row, _const_spec((D_MODEL, D_FF)), _const_spec((D_MODEL, D_FF)), _const_spec((D_FF, D_MODEL)),
                  _const_spec((1, D_MODEL)), _const_spec((1, D_MODEL))],
        out_specs=out_specs,
        out_shape=out_shape,
        scratch_shapes=scratch,
        compiler_params=_params(("arbitrary",)),
        name="ffn_ln",
    )(x, wg, wu, wd, g, b)


def _in_proj_kernel(x_ref, w_ref, z_ref):
    z_ref[...] = jnp.dot(x_ref[...], w_ref[...], preferred_element_type=F32).astype(BF16)


def in_proj(xb, w_in):
    t = xb.shape[0]
    width = w_in.shape[1]
    tm = 2048
    tn = 2560 if width % 2560 == 0 else A_QKV_W
    return pl.pallas_call(
        _in_proj_kernel,
        grid=(t // tm, width // tn),
        in_specs=[pl.BlockSpec((tm, D_MODEL), lambda i, j: (i, 0)), pl.BlockSpec((D_MODEL, tn), lambda i, j: (0, j))],
        out_specs=pl.BlockSpec((tm, tn), lambda i, j: (i, j)),
        out_shape=jax.ShapeDtypeStruct((t, width), BF16),
        compiler_params=_params(("arbitrary", "arbitrary")),
        name="in_proj",
    )(xb, w_in)


def _pair_scores(q_pair, k_pair, left):
    zero = jnp.zeros_like(q_pair)
    dims = (((1,), (1,)), ((), ()))
    return [lax.dot_general(jnp.where(left, q_pair, zero), k_pair, dims, preferred_element_type=F32),
            lax.dot_general(jnp.where(left, zero, q_pair), k_pair, dims, preferred_element_type=F32)]


def _pair_weighted_sum(probs, v_pair, left):
    rhs = []
    for hh in range(2):
        sel = left if hh == 0 else jnp.logical_not(left)
        ones = jnp.broadcast_to(jnp.where(sel, 1.0, 0.0).astype(BF16), v_pair.shape)
        rhs.append(jnp.concatenate([jnp.where(sel, v_pair, jnp.zeros_like(v_pair)), ones], axis=1))
    out = jnp.dot(jnp.concatenate(probs, axis=1), jnp.concatenate(rhs, axis=0), preferred_element_type=F32)
    width = v_pair.shape[1]
    return out[:, :width], out[:, width:]


A_QB = 128
A_KB = A_QB + 2 * A_HALF


def _mixer_a_kernel(before_ref, cur_ref, after_ref, bias_ref, o_ref, l_ref, kf, vf, *, tl, seq_len):
    i = pl.program_id(2)
    k_cols = slice(HEAD_BLOCK, 2 * HEAD_BLOCK)
    v_cols = slice(2 * HEAD_BLOCK, 3 * HEAD_BLOCK)
    kf[0:A_HALF, :] = before_ref[0, 0, :, k_cols]
    kf[A_HALF:A_HALF + tl, :] = cur_ref[0, 0, :, k_cols]
    kf[A_HALF + tl:, :] = after_ref[0, 0, :, k_cols]
    vf[0:A_HALF, :] = before_ref[0, 0, :, v_cols]
    vf[A_HALF:A_HALF + tl, :] = cur_ref[0, 0, :, v_cols]
    vf[A_HALF + tl:, :] = after_ref[0, 0, :, v_cols]

    lane = lax.broadcasted_iota(jnp.int32, (1, 2 * A_HD), 1)
    left = lane < A_HD
    lse_slot = (lane % A_HD) // LSE_LANES

    def block(j, carry):
        q0 = pl.multiple_of(j * A_QB, A_QB)
        start = i * tl + q0
        variant = (start == 0).astype(jnp.int32) + 2 * (start + A_QB == seq_len).astype(jnp.int32)
        lse_all = None
        for hp in range(A_HEADS // 2):
            cols = slice(hp * 2 * A_HD, (hp + 1) * 2 * A_HD)
            qp = cur_ref[0, 0, pl.ds(q0, A_QB), cols]
            kp = kf[pl.ds(q0, A_KB), cols]
            probs, tops = [], []
            for hh, s in enumerate(_pair_scores(qp, kp, left)):
                s = s + bias_ref[variant, hp * 2 + hh]
                m = jnp.max(s, axis=-1, keepdims=True)
                probs.append(jnp.exp2(s - m).astype(BF16))
                tops.append(m)
            num, den = _pair_weighted_sum(probs, vf[pl.ds(q0, A_KB), cols], left)
            o_ref[0, 0, pl.ds(q0, A_QB), cols] = (num / den).astype(BF16)
            lse_pair = jnp.where(left, tops[0], tops[1]) + jnp.log2(den)
            lse_all = lse_pair if hp == 0 else jnp.where(lse_slot == hp, lse_pair, lse_all)
        l_ref[0, 0, pl.ds(q0, A_QB), :] = lse_all
        return carry

    lax.fori_loop(0, tl // A_QB, block, 0, unroll=True)


def _alibi_bias(dilation):
    slopes = 2.0 ** (-8.0 * jnp.arange(1, A_HEADS + 1, dtype=F32) / A_HEADS)
    key = jnp.arange(A_KB)[None, :] - A_HALF
    rel = key - jnp.arange(A_QB)[:, None]
    dist = (jnp.abs(rel) * dilation).astype(F32)
    bias = -slopes[:, None, None] * dist[None] * LOG2E
    in_window = jnp.abs(rel) <= A_HALF
    variants = []
    for v in range(4):
        ok = in_window
        if v & 1:
            ok = ok & (key >= 0)
        if v & 2:
            ok = ok & (key < A_QB)
        variants.append(jnp.where(ok[None], bias, NEG))
    return jnp.stack(variants, 0)


def mixer_a_group(zg, dilation, qkv_blk):
    batch, _, strided_len, _ = zg.shape
    tl = min(1024, strided_len)
    halo_per_tile = tl // A_HALF
    n_halo = strided_len // A_HALF
    tile = (1, 1, tl, HEAD_BLOCK)
    out_dims = (batch, dilation, strided_len, HEAD_BLOCK)
    return pl.pallas_call(
        functools.partial(_mixer_a_kernel, tl=tl, seq_len=strided_len),
        grid=(batch, dilation, strided_len // tl),
        in_specs=[pl.BlockSpec((1, 1, A_HALF, A_QKV_W),
                               lambda b, r, i: (b, r, jnp.maximum(i * halo_per_tile - 1, 0), qkv_blk)),
                  pl.BlockSpec((1, 1, tl, A_QKV_W), lambda b, r, i: (b, r, i, qkv_blk)),
                  pl.BlockSpec((1, 1, A_HALF, A_QKV_W),
                               lambda b, r, i: (b, r, jnp.minimum((i + 1) * halo_per_tile, n_halo - 1), qkv_blk)),
                  _const_spec((4, A_HEADS, A_QB, A_KB))],
        out_specs=[pl.BlockSpec(tile, lambda b, r, i: (b, r, i, 0)),
                   pl.BlockSpec((1, 1, tl, 2 * A_HD), lambda b, r, i: (b, r, i, 0))],
        out_shape=[jax.ShapeDtypeStruct(out_dims, BF16),
                   jax.ShapeDtypeStruct((batch, dilation, strided_len, 2 * A_HD), F32)],
        scratch_shapes=[pltpu.VMEM((tl + 2 * A_HALF, HEAD_BLOCK), BF16), pltpu.VMEM((tl + 2 * A_HALF, HEAD_BLOCK), BF16)],
        compiler_params=_params(("arbitrary", "arbitrary", "arbitrary")),
        name=f"mixer_a_d{dilation}",
    )(zg, zg, zg, _alibi_bias(dilation))


def _by_token(x, batch, seq):
    return x.transpose(0, 2, 1, 3).reshape(batch * seq, x.shape[-1])


def _mixer_b_kernel(qkvg_ref, dmat_ref, qdf_ref, qdb_ref, kdf_ref, kdb_ref, cdf_ref, cdb_ref,
                    o_ref, fwd_state, bwd_state, bwd_store, *, n_blocks):
    phase = pl.program_id(1)
    n = pl.program_id(2)
    qk_w = B_HEADS * B_DK
    v_w = B_HEADS * B_DV
    head_of_lane = lax.broadcasted_iota(jnp.int32, (1, qk_w), 1) // B_DK

    def rows(c):
        return slice(c * B_CHUNK, (c + 1) * B_CHUNK)

    def stacked_heads(t):
        lane_head = jnp.concatenate([head_of_lane] * (t.shape[1] // qk_w), axis=1)
        return jnp.concatenate([jnp.where(lane_head == h, t, jnp.zeros_like(t)) for h in range(B_HEADS)], axis=0)

    def head_block_diagonal(states):
        row_head = (lax.broadcasted_iota(jnp.int32, (states.shape[0], 1), 0) % qk_w) // B_DK
        return jnp.concatenate([jnp.where(row_head == h, states, jnp.zeros_like(states)) for h in range(B_HEADS)],
                               axis=1)

    def kv_outer(k_decayed, v):
        full = lax.dot_general(k_decayed, v, (((0,), (0,)), ((), ())), preferred_element_type=F32)
        return jnp.concatenate([full[h * B_DK:(h + 1) * B_DK, h * B_DV:(h + 1) * B_DV] for h in range(B_HEADS)], axis=0)

    def scaled_k(c):
        return qkvg_ref[0, rows(c), qk_w:2 * qk_w] * (B_DK ** -0.5)

    @pl.when(phase == 0)
    def _():
        @pl.when(n == 0)
        def _():
            bwd_state[...] = jnp.zeros_like(bwd_state)

        blk = n_blocks - 1 - n
        state = bwd_state[...]
        for c in reversed(range(B_BLOCK)):
            bwd_store[blk * B_BLOCK + c] = state.astype(BF16)
            k_dec = (scaled_k(c).astype(F32) * kdb_ref[...]).astype(BF16)
            state = cdb_ref[...] * state + kv_outer(k_dec, qkvg_ref[0, rows(c), 2 * qk_w:2 * qk_w + v_w])
        bwd_state[...] = state

    @pl.when(phase == 1)
    def _():
        @pl.when(n == 0)
        def _():
            fwd_state[...] = jnp.zeros_like(fwd_state)

        state = fwd_state[...]
        for c in range(B_BLOCK):
            q = qkvg_ref[0, rows(c), 0:qk_w]
            k = scaled_k(c)
            v = qkvg_ref[0, rows(c), 2 * qk_w:2 * qk_w + v_w]
            q32 = q.astype(F32)
            q_dec = jnp.concatenate([(q32 * qdf_ref[...]).astype(BF16), (q32 * qdb_ref[...]).astype(BF16)], axis=1)
            k_dec = (k.astype(F32) * kdf_ref[...]).astype(BF16)
            states = jnp.concatenate([state.astype(BF16), bwd_store[n * B_BLOCK + c]], axis=0)
            s_all = lax.dot_general(q, stacked_heads(k), (((1,), (1,)), ((), ())), preferred_element_type=F32)
            cross_all = jnp.dot(q_dec, head_block_diagonal(states), preferred_element_type=F32)
            for h in range(B_HEADS):
                vh = v[:, h * B_DV:(h + 1) * B_DV]
                inner = jnp.dot((s_all[:, rows(h)] * dmat_ref[h]).astype(BF16), vh, preferred_element_type=F32)
                y = inner + cross_all[:, h * B_DV:(h + 1) * B_DV]
                mu = jnp.mean(y, axis=-1, keepdims=True)
                cen = y - mu
                var = jnp.mean(cen * cen, axis=-1, keepdims=True)
                yn = cen * lax.rsqrt(var + GN_EPS)
                g0 = 2 * qk_w + v_w + h * B_DV
                gate = qkvg_ref[0, rows(c), g0:g0 + B_DV].astype(F32)
                o_ref[0, rows(c), h * B_DV:(h + 1) * B_DV] = (gate * jax.nn.sigmoid(gate) * yn).astype(BF16)
            state = cdf_ref[...] * state + kv_outer(k_dec, v)
        fwd_state[...] = state


def _retention_tables(logit_fwd, logit_bwd):
    lg_f = jax.nn.log_sigmoid(logit_fwd.astype(F32))
    lg_b = jax.nn.log_sigmoid(logit_bwd.astype(F32))
    idx = jnp.arange(B_CHUNK, dtype=F32)
    diff = idx[:, None] - idx[None, :]
    causal = diff >= 0
    dmat = jnp.where(causal[None],
                     jnp.exp(lg_f[:, None, None] * jnp.where(causal, diff, 0.0)[None]),
                     jnp.exp(lg_b[:, None, None] * jnp.where(causal, 0.0, -diff)[None]))

    def per_lane(lg, power):
        return jnp.repeat(jnp.exp(lg[None, :] * power[:, None]), B_DK, axis=1)

    def per_row(lg):
        return jnp.broadcast_to(jnp.repeat(jnp.exp(lg * B_CHUNK), B_DK)[:, None], (B_HEADS * B_DK, B_DV))

    return (dmat, per_lane(lg_f, idx + 1), per_lane(lg_b, B_CHUNK - idx), per_lane(lg_f, B_CHUNK - 1 - idx),
            per_lane(lg_b, idx), per_row(lg_f), per_row(lg_b))


def mixer_b(z, batch, seq, logit_fwd, logit_bwd):
    n_chunks = seq // B_CHUNK
    n_blocks = n_chunks // B_BLOCK
    block_rows = B_BLOCK * B_CHUNK
    zv = z.reshape(batch, seq, MAIN_W)
    qk_w = B_HEADS * B_DK
    v_w = B_HEADS * B_DV

    def scan_block(ph, n):
        return (1 - ph) * (n_blocks - 1 - n) + ph * n

    tables = _retention_tables(logit_fwd, logit_bwd)
    in_specs = [pl.BlockSpec((1, block_rows, 2 * (qk_w + v_w)), lambda b, ph, n: (b, scan_block(ph, n), B_QKVG_BLK))]
    in_specs += [_const_spec(t.shape) for t in tables]
    out = pl.pallas_call(
        functools.partial(_mixer_b_kernel, n_blocks=n_blocks),
        grid=(batch, 2, n_blocks),
        in_specs=in_specs,
        out_specs=pl.BlockSpec((1, block_rows, v_w), lambda b, ph, n: (b, ph * n, 0)),
        out_shape=jax.ShapeDtypeStruct((batch, seq, v_w), BF16),
        scratch_shapes=[pltpu.VMEM((qk_w, B_DV), F32), pltpu.VMEM((qk_w, B_DV), F32),
                        pltpu.VMEM((n_chunks, qk_w, B_DV), BF16)],
        compiler_params=_params(("arbitrary", "arbitrary", "arbitrary")),
        name="mixer_b",
    )(zv, *tables)
    return out.reshape(batch * seq, v_w)


C_TQ = C_QR * GRID_W
C_KROWS = 3 * C_QR
C_TK = C_KROWS * GRID_W
C_PAIRS = C_KROWS // 2
C_NTAB = 2 * C_KH - 2


def _mixer_c_kernel(above_ref, cur_ref, below_ref, tab_ref, o_ref, kf, vf, *, rows):
    cur = C_STEP * C_TQ
    k_cols = slice(HEAD_BLOCK, 2 * HEAD_BLOCK)
    v_cols = slice(2 * HEAD_BLOCK, 3 * HEAD_BLOCK)
    kf[0:C_TQ, :] = above_ref[0, :, k_cols]
    kf[C_TQ:C_TQ + cur, :] = cur_ref[0, :, k_cols]
    kf[C_TQ + cur:, :] = below_ref[0, :, k_cols]
    vf[0:C_TQ, :] = above_ref[0, :, v_cols]
    vf[C_TQ:C_TQ + cur, :] = cur_ref[0, :, v_cols]
    vf[C_TQ + cur:, :] = below_ref[0, :, v_cols]

    left = lax.broadcasted_iota(jnp.int32, (1, 2 * C_HD), 1) < C_HD
    for sb in range(C_STEP):
        blk = pl.program_id(1) * C_STEP + sb
        q_rows = slice(sb * C_TQ, (sb + 1) * C_TQ)
        k_rows = slice(sb * C_TQ, sb * C_TQ + C_TK)
        key_row = blk * C_QR - C_QR + lax.broadcasted_iota(jnp.int32, (1, C_TK), 1) // GRID_W
        pens = []
        for a in range(C_QR):
            row_start = jnp.clip(blk * C_QR + a - C_KH // 2, 0, rows - C_KH)
            pens.append(jnp.where((key_row >= row_start) & (key_row < row_start + C_KH), 0.0, NEG).astype(F32))

        for hp in range(C_HEADS // 2):
            cols = slice(hp * 2 * C_HD, (hp + 1) * 2 * C_HD)
            qp = cur_ref[0, q_rows, cols]
            kp = kf[k_rows, cols]
            probs = []
            for hh, s in enumerate(_pair_scores(qp, kp, left)):
                h = hp * 2 + hh
                rows_p = []
                for a in range(C_QR):
                    bias = jnp.concatenate([tab_ref[h, 2 * t - C_QR - a + C_KH - 1] for t in range(C_PAIRS)], axis=1)
                    sa = s[a * GRID_W:(a + 1) * GRID_W, :] + bias + pens[a]
                    m = jnp.max(sa, axis=-1, keepdims=True)
                    rows_p.append(jnp.exp2(sa - m).astype(BF16))
                probs.append(jnp.concatenate(rows_p, axis=0))
            num, den = _pair_weighted_sum(probs, vf[k_rows, cols], left)
            o_ref[0, q_rows, cols] = (num / den).astype(BF16)


def _neighbourhood_bias(rpb):
    qc = jnp.arange(GRID_W)[:, None]
    kc = jnp.arange(GRID_W)[None, :]
    col_start = jnp.clip(qc - C_KW // 2, 0, GRID_W - C_KW)
    col_ok = (kc >= col_start) & (kc < col_start + C_KW)
    onehot = ((kc - qc + (C_KW - 1))[:, :, None] == jnp.arange(2 * C_KW - 1)[None, None, :]).astype(F32)
    band = jnp.einsum("hrd,qkd->hrqk", rpb.astype(F32), onehot, precision=lax.Precision.HIGHEST)
    band = jnp.where(col_ok[None, None], band * LOG2E, NEG)
    return jnp.concatenate([band[:, :C_NTAB], band[:, 1:C_NTAB + 1]], axis=-1)


def mixer_c(z, batch, seq, rpb):
    rows = seq // GRID_W
    n_blk = rows // C_QR
    zv = z.reshape(batch, seq, MAIN_W)
    tab = _neighbourhood_bias(rpb)
    halo = (1, C_TQ, A_QKV_W)
    tile = (1, C_STEP * C_TQ, A_QKV_W)
    out = pl.pallas_call(
        functools.partial(_mixer_c_kernel, rows=rows),
        grid=(batch, n_blk // C_STEP),
        in_specs=[pl.BlockSpec(halo, lambda b, i: (b, jnp.maximum(i * C_STEP - 1, 0), C_QKV_BLK)),
                  pl.BlockSpec(tile, lambda b, i: (b, i, C_QKV_BLK)),
                  pl.BlockSpec(halo, lambda b, i: (b, jnp.minimum((i + 1) * C_STEP, n_blk - 1), C_QKV_BLK)),
                  _const_spec(tab.shape)],
        out_specs=pl.BlockSpec((1, C_STEP * C_TQ, HEAD_BLOCK), lambda b, i: (b, i, 0)),
        out_shape=jax.ShapeDtypeStruct((batch, seq, HEAD_BLOCK), BF16),
        scratch_shapes=[pltpu.VMEM(((C_STEP + 2) * C_TQ, HEAD_BLOCK), BF16),
                        pltpu.VMEM(((C_STEP + 2) * C_TQ, HEAD_BLOCK), BF16)],
        compiler_params=_params(("arbitrary", "arbitrary")),
        name="mixer_c",
    )(zv, zv, zv, tab)
    return out.reshape(batch * seq, HEAD_BLOCK)


def _merge_kernel(x_ref, oa0_ref, oa1_ref, oa2_ref, la0_ref, la1_ref, la2_ref, yb_ref, yc_ref, gates_ref, spread_ref,
                  wa_ref, wb_ref, wc_ref, wo_ref, g_ref, b_ref, o_ref):
    lses = [la0_ref[...], la1_ref[...], la2_ref[...]]
    top = jnp.maximum(jnp.maximum(lses[0], lses[1]), lses[2])
    weights = [jnp.exp2(lse - top) for lse in lses]
    total = weights[0] + weights[1] + weights[2]
    ya = None
    for o_g_ref, w in zip((oa0_ref, oa1_ref, oa2_ref), weights):
        share = w / total
        hi = share.astype(BF16)
        lo = (share - hi.astype(F32)).astype(BF16)
        wide = jnp.dot(jnp.concatenate([hi, lo], axis=1), spread_ref[...], preferred_element_type=F32)
        term = wide * o_g_ref[...].astype(F32)
        ya = term if ya is None else ya + term
    ya = ya.astype(BF16)

    merged = None
    for br, (y, w_ref) in enumerate(((ya, wa_ref), (yb_ref[...], wb_ref), (yc_ref[...], wc_ref))):
        proj = jnp.dot(y, w_ref[...], preferred_element_type=F32)
        gate = jax.nn.sigmoid(gates_ref[:, br * D_MODEL:(br + 1) * D_MODEL].astype(F32))
        merged = gate * proj if merged is None else merged + gate * proj
    merged = merged.astype(BF16)
    for r0 in range(0, x_ref.shape[0], MERGE_ROWS):
        rows = slice(r0, r0 + MERGE_ROWS)
        out = jnp.dot(merged[rows], wo_ref[...], preferred_element_type=F32)
        o_ref[rows, :] = _layer_norm(ALPHA * x_ref[rows, :] + out, g_ref[...], b_ref[...])


def _lse_spread():
    src = jnp.arange(2 * A_HD)[:, None]
    head = jnp.arange(HEAD_BLOCK)[None, :] // A_HD
    first_lane = (head % 2) * A_HD + (head // 2) * LSE_LANES
    once = (src == first_lane).astype(BF16)
    return jnp.concatenate([once, once], axis=0)


def merge_out_ln(x, oa, la, yb, yc, z, wa, wb, wc, wo, g, b):
    t = x.shape[0]
    tm = 1024
    row = pl.BlockSpec((tm, D_MODEL), lambda i: (i, 0))
    br = pl.BlockSpec((tm, HEAD_BLOCK), lambda i: (i, 0))
    lse = pl.BlockSpec((tm, 2 * A_HD), lambda i: (i, 0))
    w_br = _const_spec((HEAD_BLOCK, D_MODEL))
    return pl.pallas_call(
        _merge_kernel,
        grid=(t // tm,),
        in_specs=[row] + [br] * 3 + [lse] * 3 + [br] * 2 + [
            pl.BlockSpec((tm, GATE_W), lambda i: (i, 0)), _const_spec((4 * A_HD, HEAD_BLOCK)), w_br, w_br, w_br,
            _const_spec((D_MODEL, D_MODEL)), _const_spec((1, D_MODEL)), _const_spec((1, D_MODEL))],
        out_specs=row,
        out_shape=jax.ShapeDtypeStruct((t, D_MODEL), F32),
        compiler_params=_params(("arbitrary",)),
        name="merge_out_ln",
    )(x, *oa, *la, yb, yc, z, _lse_spread(), wa, wb, wc, wo, g, b)


def _split_in_weights(w_in):
    a_w = A_GROUPS * HEAD_BLOCK
    aq, ak, av = w_in[:, :a_w] * (A_HD ** -0.5 * LOG2E), w_in[:, a_w:2 * a_w], w_in[:, 2 * a_w:3 * a_w]
    b_w = 2 * B_HEADS * (B_DK + B_DV)
    rest_b = w_in[:, 3 * a_w:3 * a_w + b_w]
    cq = w_in[:, 3 * a_w + b_w:3 * a_w + b_w + HEAD_BLOCK] * (C_HD ** -0.5 * LOG2E)
    rest = jnp.concatenate([rest_b, cq, w_in[:, 3 * a_w + b_w + HEAD_BLOCK:D_IN - GATE_W]], axis=1)

    def group(g):
        cols = slice(g * HEAD_BLOCK, (g + 1) * HEAD_BLOCK)
        return jnp.concatenate([aq[:, cols], ak[:, cols], av[:, cols]], axis=1)

    main = jnp.concatenate([w_in[:, D_IN - GATE_W:], group(0), rest], axis=1)
    return main.astype(BF16), [group(g).astype(BF16) for g in range(1, A_GROUPS)]


def _trunk(x, layers):
    batch, seq, _ = x.shape
    x = x.reshape(batch * seq, D_MODEL)
    for p in layers:
        dils = tuple(d for _, d in A_PATTERNS[1:])
        x1, x1b, *x1_by_residue = ffn_ln(x, p["wg1"], p["wu1"], p["wd1"], p["g1"], p["b1"], batch, dils)
        z = in_proj(x1b, p["w_main"])
        oa, la = [], []
        o, l = mixer_a_group(z.reshape(batch, 1, seq, MAIN_W), 1, A_QKV_BLK)
        oa.append(o.reshape(batch * seq, HEAD_BLOCK))
        la.append(l.reshape(batch * seq, 2 * A_HD))
        for dilation, xg, w_g in zip(dils, x1_by_residue, p["w_groups"]):
            zg = in_proj(xg.reshape(batch * seq, D_MODEL), w_g)
            o, l = mixer_a_group(zg.reshape(batch, dilation, seq // dilation, A_QKV_W), dilation, 0)
            oa.append(_by_token(o, batch, seq))
            la.append(_by_token(l, batch, seq))
        yb = mixer_b(z, batch, seq, p["logit_fwd"], p["logit_bwd"])
        yc = mixer_c(z, batch, seq, p["rpb"])
        x2 = merge_out_ln(x1, oa, la, yb, yc, z, p["wa"], p["wb"], p["wc"], p["wo"], p["g2"], p["b2"])
        (x,) = ffn_ln(x2, p["wg2"], p["wu2"], p["wd2"], p["g3"], p["b3"])
    return x.reshape(batch, seq, D_MODEL)


def kernel(x_prompt, x_sample, ffn1_w_gate, ffn1_w_up, ffn1_w_down, ln1_g, ln1_b, w_in, ret_logit_fwd, ret_logit_bwd, na_rpb, w_branch_a, w_branch_b, w_branch_c, w_out, ln2_g, ln2_b, ffn2_w_gate, ffn2_w_up, ffn2_w_down, ln3_g, ln3_b):
    def vec(v):
        return v.astype(F32).reshape(1, D_MODEL)

    layers = []
    for i in range(DEPTH):
        w_main, w_groups = _split_in_weights(w_in[i])
        layers.append(dict(
            wg1=ffn1_w_gate[i].astype(BF16), wu1=ffn1_w_up[i].astype(BF16), wd1=ffn1_w_down[i].astype(BF16),
            g1=vec(ln1_g[i]), b1=vec(ln1_b[i]),
            w_main=w_main, w_groups=w_groups,
            logit_fwd=ret_logit_fwd[i], logit_bwd=ret_logit_bwd[i], rpb=na_rpb[i],
            wa=w_branch_a[i].astype(BF16), wb=w_branch_b[i].astype(BF16), wc=w_branch_c[i].astype(BF16),
            wo=w_out[i].astype(BF16), g2=vec(ln2_g[i]), b2=vec(ln2_b[i]),
            wg2=ffn2_w_gate[i].astype(BF16), wu2=ffn2_w_up[i].astype(BF16), wd2=ffn2_w_down[i].astype(BF16),
            g3=vec(ln3_g[i]), b3=vec(ln3_b[i])))
    return (_trunk(x_prompt, layers), _trunk(x_sample, layers))
```

```python
import functools

import jax
import jax.numpy as jnp
from jax import lax
from jax.experimental import pallas as pl
from jax.experimental.pallas import tpu as pltpu

F32 = jnp.float32
BF16 = jnp.bfloat16

D_MODEL = 1024
DEPTH = 2
D_FF = 2816
LN_EPS = 1e-5
GN_EPS = 1e-5
ALPHA = (2 * DEPTH) ** 0.25

A_PATTERNS = ((128, 1), (512, 4), (2048, 16))
A_GROUPS = len(A_PATTERNS)
A_HEADS = 8
A_HD = 64
A_HALF = 64
LSE_LANES = A_HD // (A_HEADS // 2)
B_HEADS = 4
B_DK = 64
B_DV = 128
B_CHUNK = 128
B_BLOCK = 8
C_HEADS = 8
C_HD = 64
GRID_W = 64
C_KH = 8
C_KW = 16
C_QR = 4
C_STEP = 4
D_IN = 10752
NEG = -1e30
LANES = 128
LOG2E = 1.4426950408889634

GATE_W = 3 * D_MODEL
HEAD_BLOCK = 512
A_QKV_W = 3 * HEAD_BLOCK
MAIN_W = D_IN - (A_GROUPS - 1) * A_QKV_W
A_QKV_BLK = 2
B_QKVG_BLK = 3
C_QKV_BLK = 4

VMEM_LIMIT = 56 * 1024 * 1024

FFN_ROWS = 1024
PROJ_ROWS = 2048
PROJ_COLS = 2560
MERGE_TILE_ROWS = 1024
A_TILE = 2048
FF_CHUNKS = ((0, 512), (512, 1024), (1024, 1536), (1536, 2048), (2048, 2560), (2560, 2816))
FF_NORM_ROWS = 256
MERGE_ROWS = 256


def _params(sem):
    return pltpu.CompilerParams(dimension_semantics=sem, vmem_limit_bytes=VMEM_LIMIT)


def _const_spec(shape):
    zeros = (0,) * len(shape)
    return pl.BlockSpec(shape, lambda *_: zeros)


def _layer_norm(r, g, b):
    mu = jnp.mean(r, axis=-1, keepdims=True)
    c = r - mu
    var = jnp.mean(c * c, axis=-1, keepdims=True)
    return c * lax.rsqrt(var + LN_EPS) * g + b


def _ffn_ln_kernel(x_ref, wg_ref, wu_ref, wd_ref, g_ref, b_ref, o_ref, *rest, dilations):
    x = x_ref[...]
    xb = x.astype(BF16)
    hidden = []
    for c0, c1 in FF_CHUNKS:
        gate = jnp.dot(xb, wg_ref[:, c0:c1], preferred_element_type=F32)
        up = jnp.dot(xb, wu_ref[:, c0:c1], preferred_element_type=F32)
        hidden.append((gate * jax.nn.sigmoid(gate) * up).astype(BF16))
    hidden = jnp.concatenate(hidden, axis=1)
    if rest:
        ob_ref, *og_refs, slabs = rest
    n_slabs = D_MODEL // LANES
    rows = FF_NORM_ROWS
    for r0 in range(0, x.shape[0], rows):
        acc = jnp.dot(hidden[r0:r0 + rows], wd_ref[...], preferred_element_type=F32)
        y = _layer_norm(ALPHA * x[r0:r0 + rows] + 0.5 * acc, g_ref[...], b_ref[...])
        o_ref[r0:r0 + rows, :] = y
        if not rest:
            continue
        ob_ref[r0:r0 + rows, :] = y.astype(BF16)
        done = 1
        ordered = y
        for d, og_ref in zip(dilations, og_refs):
            for s in range(n_slabs):
                slabs[s] = ordered[:, s * LANES:(s + 1) * LANES]
            step = d // done
            group = rows // done
            pieces = []
            for r in range(d):
                first = (r % done) * group + r // done
                picked = [slabs[s, pl.ds(first, rows // d, stride=step), :] for s in range(n_slabs)]
                pieces.append(jnp.concatenate(picked, axis=1))
                og_ref[0, r, r0 // d:(r0 + rows) // d, :] = pieces[-1].astype(BF16)
            ordered = jnp.concatenate(pieces, axis=0)
            done = d


def ffn_ln(x, wg, wu, wd, g, b, batch=None, dilations=()):
    t = x.shape[0]
    tm = FFN_ROWS
    row = pl.BlockSpec((tm, D_MODEL), lambda i: (i, 0))
    out_shape = [jax.ShapeDtypeStruct((t, D_MODEL), F32)]
    out_specs = [row]
    scratch = []
    if dilations:
        assert all(b % a == 0 for a, b in zip((1,) + dilations, dilations)), dilations
        tiles_per_seq = t // batch // tm
        out_shape.append(jax.ShapeDtypeStruct((t, D_MODEL), BF16))
        out_specs.append(row)
        for d in dilations:
            out_shape.append(jax.ShapeDtypeStruct((batch, d, t // batch // d, D_MODEL), BF16))
            out_specs.append(pl.BlockSpec((1, d, tm // d, D_MODEL),
                                          lambda i: (i // tiles_per_seq, 0, i % tiles_per_seq, 0)))
        scratch = [pltpu.VMEM((D_MODEL // LANES, FF_NORM_ROWS, LANES), F32)]
    return pl.pallas_call(
        functools.partial(_ffn_ln_kernel, dilations=dilations),
        grid=(t // tm,),
        in_specs=[row, _const_spec((D_MODEL, D_FF)), _const_spec((D_MODEL, D_FF)), _const_spec((D_FF, D_MODEL)),
                  _const_spec((1, D_MODEL)), _const_spec((1, D_MODEL))],
        out_specs=out_specs,
        out_shape=out_shape,
        scratch_shapes=scratch,
        compiler_params=_params(("arbitrary",)),
        name="ffn_ln",
    )(x, wg, wu, wd, g, b)


def _in_proj_kernel(x_ref, w_ref, z_ref):
    z_ref[...] = jnp.dot(x_ref[...], w_ref[...], preferred_element_type=F32).astype(BF16)


def in_proj(xb, w_in):
    t = xb.shape[0]
    width = w_in.shape[1]
    tm = PROJ_ROWS
    tn = PROJ_COLS if width % PROJ_COLS == 0 else A_QKV_W
    return pl.pallas_call(
        _in_proj_kernel,
        grid=(t // tm, width // tn),
        in_specs=[pl.BlockSpec((tm, D_MODEL), lambda i, j: (i, 0)), pl.BlockSpec((D_MODEL, tn), lambda i, j: (0, j))],
        out_specs=pl.BlockSpec((tm, tn), lambda i, j: (i, j)),
        out_shape=jax.ShapeDtypeStruct((t, width), BF16),
        compiler_params=_params(("arbitrary", "arbitrary")),
        name="in_proj",
    )(xb, w_in)


def _pair_scores(q_pair, k_pair, left):
    zero = jnp.zeros_like(q_pair)
    dims = (((1,), (1,)), ((), ()))
    return [lax.dot_general(jnp.where(left, q_pair, zero), k_pair, dims, preferred_element_type=F32),
            lax.dot_general(jnp.where(left, zero, q_pair), k_pair, dims, preferred_element_type=F32)]


def _pair_weighted_sum(probs, v_pair, left):
    rhs = []
    for hh in range(2):
        sel = left if hh == 0 else jnp.logical_not(left)
        ones = jnp.broadcast_to(jnp.where(sel, 1.0, 0.0).astype(BF16), v_pair.shape)
        rhs.append(jnp.concatenate([jnp.where(sel, v_pair, jnp.zeros_like(v_pair)), ones], axis=1))
    out = jnp.dot(jnp.concatenate(probs, axis=1), jnp.concatenate(rhs, axis=0), preferred_element_type=F32)
    width = v_pair.shape[1]
    return out[:, :width], out[:, width:]


A_QB = 128
A_KB = A_QB + 2 * A_HALF


def _mixer_a_kernel(before_ref, cur_ref, after_ref, bias_ref, o_ref, l_ref, kf, vf, *, tl, seq_len):
    i = pl.program_id(2)
    k_cols = slice(HEAD_BLOCK, 2 * HEAD_BLOCK)
    v_cols = slice(2 * HEAD_BLOCK, 3 * HEAD_BLOCK)
    kf[0:A_HALF, :] = before_ref[0, 0, :, k_cols]
    kf[A_HALF:A_HALF + tl, :] = cur_ref[0, 0, :, k_cols]
    kf[A_HALF + tl:, :] = after_ref[0, 0, :, k_cols]
    vf[0:A_HALF, :] = before_ref[0, 0, :, v_cols]
    vf[A_HALF:A_HALF + tl, :] = cur_ref[0, 0, :, v_cols]
    vf[A_HALF + tl:, :] = after_ref[0, 0, :, v_cols]

    lane = lax.broadcasted_iota(jnp.int32, (1, 2 * A_HD), 1)
    left = lane < A_HD
    lse_slot = (lane % A_HD) // LSE_LANES

    def block(j, carry):
        q0 = pl.multiple_of(j * A_QB, A_QB)
        start = i * tl + q0
        variant = (start == 0).astype(jnp.int32) + 2 * (start + A_QB == seq_len).astype(jnp.int32)
        lse_all = None
        for hp in range(A_HEADS // 2):
            cols = slice(hp * 2 * A_HD, (hp + 1) * 2 * A_HD)
            qp = cur_ref[0, 0, pl.ds(q0, A_QB), cols]
            kp = kf[pl.ds(q0, A_KB), cols]
            probs, tops = [], []
            for hh, s in enumerate(_pair_scores(qp, kp, left)):
                s = s + bias_ref[variant, hp * 2 + hh]
                m = jnp.max(s, axis=-1, keepdims=True)
                probs.append(jnp.exp2(s - m).astype(BF16))
                tops.append(m)
            num, den = _pair_weighted_sum(probs, vf[pl.ds(q0, A_KB), cols], left)
            o_ref[0, 0, pl.ds(q0, A_QB), cols] = (num / den).astype(BF16)
            lse_pair = jnp.where(left, tops[0], tops[1]) + jnp.log2(den)
            lse_all = lse_pair if hp == 0 else jnp.where(lse_slot == hp, lse_pair, lse_all)
        l_ref[0, 0, pl.ds(q0, A_QB), :] = lse_all
        return carry

    lax.fori_loop(0, tl // A_QB, block, 0, unroll=True)


def _alibi_bias(dilation):
    slopes = 2.0 ** (-8.0 * jnp.arange(1, A_HEADS + 1, dtype=F32) / A_HEADS)
    key = jnp.arange(A_KB)[None, :] - A_HALF
    rel = key - jnp.arange(A_QB)[:, None]
    dist = (jnp.abs(rel) * dilation).astype(F32)
    bias = -slopes[:, None, None] * dist[None] * LOG2E
    in_window = jnp.abs(rel) <= A_HALF
    variants = []
    for v in range(4):
        ok = in_window
        if v & 1:
            ok = ok & (key >= 0)
        if v & 2:
            ok = ok & (key < A_QB)
        variants.append(jnp.where(ok[None], bias, NEG))
    return jnp.stack(variants, 0)


def mixer_a_group(zg, dilation, qkv_blk):
    batch, _, strided_len, _ = zg.shape
    tl = min(A_TILE, strided_len)
    halo_per_tile = tl // A_HALF
    n_halo = strided_len // A_HALF
    tile = (1, 1, tl, HEAD_BLOCK)
    out_dims = (batch, dilation, strided_len, HEAD_BLOCK)
    return pl.pallas_call(
        functools.partial(_mixer_a_kernel, tl=tl, seq_len=strided_len),
        grid=(batch, dilation, strided_len // tl),
        in_specs=[pl.BlockSpec((1, 1, A_HALF, A_QKV_W),
                               lambda b, r, i: (b, r, jnp.maximum(i * halo_per_tile - 1, 0), qkv_blk)),
                  pl.BlockSpec((1, 1, tl, A_QKV_W), lambda b, r, i: (b, r, i, qkv_blk)),
                  pl.BlockSpec((1, 1, A_HALF, A_QKV_W),
                               lambda b, r, i: (b, r, jnp.minimum((i + 1) * halo_per_tile, n_halo - 1), qkv_blk)),
                  _const_spec((4, A_HEADS, A_QB, A_KB))],
        out_specs=[pl.BlockSpec(tile, lambda b, r, i: (b, r, i, 0)),
                   pl.BlockSpec((1, 1, tl, 2 * A_HD), lambda b, r, i: (b, r, i, 0))],
        out_shape=[jax.ShapeDtypeStruct(out_dims, BF16),
                   jax.ShapeDtypeStruct((batch, dilation, strided_len, 2 * A_HD), F32)],
        scratch_shapes=[pltpu.VMEM((tl + 2 * A_HALF, HEAD_BLOCK), BF16), pltpu.VMEM((tl + 2 * A_HALF, HEAD_BLOCK), BF16)],
        compiler_params=_params(("arbitrary", "arbitrary", "arbitrary")),
        name=f"mixer_a_d{dilation}",
    )(zg, zg, zg, _alibi_bias(dilation))


def _by_token(x, batch, seq):
    return x.transpose(0, 2, 1, 3).reshape(batch * seq, x.shape[-1])


def _mixer_b_kernel(qkvg_ref, dmat_ref, qdf_ref, qdb_ref, kdf_ref, kdb_ref, cdf_ref, cdb_ref,
                    o_ref, fwd_state, bwd_state, bwd_store, *, n_blocks):
    phase = pl.program_id(1)
    n = pl.program_id(2)
    qk_w = B_HEADS * B_DK
    v_w = B_HEADS * B_DV
    head_of_lane = lax.broadcasted_iota(jnp.int32, (1, qk_w), 1) // B_DK

    def rows(c):
        return slice(c * B_CHUNK, (c + 1) * B_CHUNK)

    def stacked_heads(t):
        lane_head = jnp.concatenate([head_of_lane] * (t.shape[1] // qk_w), axis=1)
        return jnp.concatenate([jnp.where(lane_head == h, t, jnp.zeros_like(t)) for h in range(B_HEADS)], axis=0)

    def head_block_diagonal(states):
        row_head = (lax.broadcasted_iota(jnp.int32, (states.shape[0], 1), 0) % qk_w) // B_DK
        return jnp.concatenate([jnp.where(row_head == h, states, jnp.zeros_like(states)) for h in range(B_HEADS)],
                               axis=1)

    def kv_outer(k_decayed, v):
        full = lax.dot_general(k_decayed, v, (((0,), (0,)), ((), ())), preferred_element_type=F32)
        return jnp.concatenate([full[h * B_DK:(h + 1) * B_DK, h * B_DV:(h + 1) * B_DV] for h in range(B_HEADS)], axis=0)

    def scaled_k(c):
        return qkvg_ref[0, rows(c), qk_w:2 * qk_w] * (B_DK ** -0.5)

    @pl.when(phase == 0)
    def _():
        @pl.when(n == 0)
        def _():
            bwd_state[...] = jnp.zeros_like(bwd_state)

        blk = n_blocks - 1 - n
        state = bwd_state[...]
        for c in reversed(range(B_BLOCK)):
            bwd_store[blk * B_BLOCK + c] = state.astype(BF16)
            k_dec = (scaled_k(c).astype(F32) * kdb_ref[...]).astype(BF16)
            state = cdb_ref[...] * state + kv_outer(k_dec, qkvg_ref[0, rows(c), 2 * qk_w:2 * qk_w + v_w])
        bwd_state[...] = state

    @pl.when(phase == 1)
    def _():
        @pl.when(n == 0)
        def _():
            fwd_state[...] = jnp.zeros_like(fwd_state)

        state = fwd_state[...]
        for c in range(B_BLOCK):
            q = qkvg_ref[0, rows(c), 0:qk_w]
            k = scaled_k(c)
            v = qkvg_ref[0, rows(c), 2 * qk_w:2 * qk_w + v_w]
            q32 = q.astype(F32)
            q_dec = jnp.concatenate([(q32 * qdf_ref[...]).astype(BF16), (q32 * qdb_ref[...]).astype(BF16)], axis=1)
            k_dec = (k.astype(F32) * kdf_ref[...]).astype(BF16)
            states = jnp.concatenate([state.astype(BF16), bwd_store[n * B_BLOCK + c]], axis=0)
            s_all = lax.dot_general(q, stacked_heads(k), (((1,), (1,)), ((), ())), preferred_element_type=F32)
            cross_all = jnp.dot(q_dec, head_block_diagonal(states), preferred_element_type=F32)
            for h in range(B_HEADS):
                vh = v[:, h * B_DV:(h + 1) * B_DV]
                inner = jnp.dot((s_all[:, rows(h)] * dmat_ref[h]).astype(BF16), vh, preferred_element_type=F32)
                y = inner + cross_all[:, h * B_DV:(h + 1) * B_DV]
                mu = jnp.mean(y, axis=-1, keepdims=True)
                cen = y - mu
                var = jnp.mean(cen * cen, axis=-1, keepdims=True)
                yn = cen * lax.rsqrt(var + GN_EPS)
                g0 = 2 * qk_w + v_w + h * B_DV
                gate = qkvg_ref[0, rows(c), g0:g0 + B_DV].astype(F32)
                o_ref[0, rows(c), h * B_DV:(h + 1) * B_DV] = (gate * jax.nn.sigmoid(gate) * yn).astype(BF16)
            state = cdf_ref[...] * state + kv_outer(k_dec, v)
        fwd_state[...] = state


def _retention_tables(logit_fwd, logit_bwd):
    lg_f = jax.nn.log_sigmoid(logit_fwd.astype(F32))
    lg_b = jax.nn.log_sigmoid(logit_bwd.astype(F32))
    idx = jnp.arange(B_CHUNK, dtype=F32)
    diff = idx[:, None] - idx[None, :]
    causal = diff >= 0
    dmat = jnp.where(causal[None],
                     jnp.exp(lg_f[:, None, None] * jnp.where(causal, diff, 0.0)[None]),
                     jnp.exp(lg_b[:, None, None] * jnp.where(causal, 0.0, -diff)[None]))

    def per_lane(lg, power):
        return jnp.repeat(jnp.exp(lg[None, :] * power[:, None]), B_DK, axis=1)

    def per_row(lg):
        return jnp.broadcast_to(jnp.repeat(jnp.exp(lg * B_CHUNK), B_DK)[:, None], (B_HEADS * B_DK, B_DV))

    return (dmat, per_lane(lg_f, idx + 1), per_lane(lg_b, B_CHUNK - idx), per_lane(lg_f, B_CHUNK - 1 - idx),
            per_lane(lg_b, idx), per_row(lg_f), per_row(lg_b))


def mixer_b(z, batch, seq, logit_fwd, logit_bwd):
    n_chunks = seq // B_CHUNK
    n_blocks = n_chunks // B_BLOCK
    block_rows = B_BLOCK * B_CHUNK
    zv = z.reshape(batch, seq, MAIN_W)
    qk_w = B_HEADS * B_DK
    v_w = B_HEADS * B_DV

    def scan_block(ph, n):
        return (1 - ph) * (n_blocks - 1 - n) + ph * n

    tables = _retention_tables(logit_fwd, logit_bwd)
    in_specs = [pl.BlockSpec((1, block_rows, 2 * (qk_w + v_w)), lambda b, ph, n: (b, scan_block(ph, n), B_QKVG_BLK))]
    in_specs += [_const_spec(t.shape) for t in tables]
    out = pl.pallas_call(
        functools.partial(_mixer_b_kernel, n_blocks=n_blocks),
        grid=(batch, 2, n_blocks),
        in_specs=in_specs,
        out_specs=pl.BlockSpec((1, block_rows, v_w), lambda b, ph, n: (b, ph * n, 0)),
        out_shape=jax.ShapeDtypeStruct((batch, seq, v_w), BF16),
        scratch_shapes=[pltpu.VMEM((qk_w, B_DV), F32), pltpu.VMEM((qk_w, B_DV), F32),
                        pltpu.VMEM((n_chunks, qk_w, B_DV), BF16)],
        compiler_params=_params(("arbitrary", "arbitrary", "arbitrary")),
        name="mixer_b",
    )(zv, *tables)
    return out.reshape(batch * seq, v_w)


C_TQ = C_QR * GRID_W
C_KROWS = 3 * C_QR
C_TK = C_KROWS * GRID_W
C_PAIRS = C_KROWS // 2
C_NTAB = 2 * C_KH - 2


def _mixer_c_kernel(above_ref, cur_ref, below_ref, tab_ref, o_ref, kf, vf, *, rows):
    cur = C_STEP * C_TQ
    k_cols = slice(HEAD_BLOCK, 2 * HEAD_BLOCK)
    v_cols = slice(2 * HEAD_BLOCK, 3 * HEAD_BLOCK)
    kf[0:C_TQ, :] = above_ref[0, :, k_cols]
    kf[C_TQ:C_TQ + cur, :] = cur_ref[0, :, k_cols]
    kf[C_TQ + cur:, :] = below_ref[0, :, k_cols]
    vf[0:C_TQ, :] = above_ref[0, :, v_cols]
    vf[C_TQ:C_TQ + cur, :] = cur_ref[0, :, v_cols]
    vf[C_TQ + cur:, :] = below_ref[0, :, v_cols]

    left = lax.broadcasted_iota(jnp.int32, (1, 2 * C_HD), 1) < C_HD
    for sb in range(C_STEP):
        blk = pl.program_id(1) * C_STEP + sb
        q_rows = slice(sb * C_TQ, (sb + 1) * C_TQ)
        k_rows = slice(sb * C_TQ, sb * C_TQ + C_TK)
        key_row = blk * C_QR - C_QR + lax.broadcasted_iota(jnp.int32, (1, C_TK), 1) // GRID_W
        pens = []
        for a in range(C_QR):
            row_start = jnp.clip(blk * C_QR + a - C_KH // 2, 0, rows - C_KH)
            pens.append(jnp.where((key_row >= row_start) & (key_row < row_start + C_KH), 0.0, NEG).astype(F32))

        for hp in range(C_HEADS // 2):
            cols = slice(hp * 2 * C_HD, (hp + 1) * 2 * C_HD)
            qp = cur_ref[0, q_rows, cols]
            kp = kf[k_rows, cols]
            probs = []
            for hh, s in enumerate(_pair_scores(qp, kp, left)):
                h = hp * 2 + hh
                rows_p = []
                for a in range(C_QR):
                    bias = jnp.concatenate([tab_ref[h, 2 * t - C_QR - a + C_KH - 1] for t in range(C_PAIRS)], axis=1)
                    sa = s[a * GRID_W:(a + 1) * GRID_W, :] + bias + pens[a]
                    m = jnp.max(sa, axis=-1, keepdims=True)
                    rows_p.append(jnp.exp2(sa - m).astype(BF16))
                probs.append(jnp.concatenate(rows_p, axis=0))
            num, den = _pair_weighted_sum(probs, vf[k_rows, cols], left)
            o_ref[0, q_rows, cols] = (num / den).astype(BF16)


def _neighbourhood_bias(rpb):
    qc = jnp.arange(GRID_W)[:, None]
    kc = jnp.arange(GRID_W)[None, :]
    col_start = jnp.clip(qc - C_KW // 2, 0, GRID_W - C_KW)
    col_ok = (kc >= col_start) & (kc < col_start + C_KW)
    onehot = ((kc - qc + (C_KW - 1))[:, :, None] == jnp.arange(2 * C_KW - 1)[None, None, :]).astype(F32)
    band = jnp.einsum("hrd,qkd->hrqk", rpb.astype(F32), onehot, precision=lax.Precision.HIGHEST)
    band = jnp.where(col_ok[None, None], band * LOG2E, NEG)
    return jnp.concatenate([band[:, :C_NTAB], band[:, 1:C_NTAB + 1]], axis=-1)


def mixer_c(z, batch, seq, rpb):
    rows = seq // GRID_W
    n_blk = rows // C_QR
    zv = z.reshape(batch, seq, MAIN_W)
    tab = _neighbourhood_bias(rpb)
    halo = (1, C_TQ, A_QKV_W)
    tile = (1, C_STEP * C_TQ, A_QKV_W)
    out = pl.pallas_call(
        functools.partial(_mixer_c_kernel, rows=rows),
        grid=(batch, n_blk // C_STEP),
        in_specs=[pl.BlockSpec(halo, lambda b, i: (b, jnp.maximum(i * C_STEP - 1, 0), C_QKV_BLK)),
                  pl.BlockSpec(tile, lambda b, i: (b, i, C_QKV_BLK)),
                  pl.BlockSpec(halo, lambda b, i: (b, jnp.minimum((i + 1) * C_STEP, n_blk - 1), C_QKV_BLK)),
                  _const_spec(tab.shape)],
        out_specs=pl.BlockSpec((1, C_STEP * C_TQ, HEAD_BLOCK), lambda b, i: (b, i, 0)),
        out_shape=jax.ShapeDtypeStruct((batch, seq, HEAD_BLOCK), BF16),
        scratch_shapes=[pltpu.VMEM(((C_STEP + 2) * C_TQ, HEAD_BLOCK), BF16),
                        pltpu.VMEM(((C_STEP + 2) * C_TQ, HEAD_BLOCK), BF16)],
        compiler_params=_params(("arbitrary", "arbitrary")),
        name="mixer_c",
    )(zv, zv, zv, tab)
    return out.reshape(batch * seq, HEAD_BLOCK)


def _merge_kernel(x_ref, oa0_ref, oa1_ref, oa2_ref, la0_ref, la1_ref, la2_ref, yb_ref, yc_ref, gates_ref, spread_ref,
                  wa_ref, wb_ref, wc_ref, wo_ref, g_ref, b_ref, o_ref):
    lses = [la0_ref[...], la1_ref[...], la2_ref[...]]
    top = jnp.maximum(jnp.maximum(lses[0], lses[1]), lses[2])
    weights = [jnp.exp2(lse - top) for lse in lses]
    total = weights[0] + weights[1] + weights[2]
    ya = None
    for o_g_ref, w in zip((oa0_ref, oa1_ref, oa2_ref), weights):
        share = w / total
        hi = share.astype(BF16)
        lo = (share - hi.astype(F32)).astype(BF16)
        wide = jnp.dot(jnp.concatenate([hi, lo], axis=1), spread_ref[...], preferred_element_type=F32)
        term = wide * o_g_ref[...].astype(F32)
        ya = term if ya is None else ya + term
    ya = ya.astype(BF16)

    merged = None
    for br, (y, w_ref) in enumerate(((ya, wa_ref), (yb_ref[...], wb_ref), (yc_ref[...], wc_ref))):
        proj = jnp.dot(y, w_ref[...], preferred_element_type=F32)
        gate = jax.nn.sigmoid(gates_ref[:, br * D_MODEL:(br + 1) * D_MODEL].astype(F32))
        merged = gate * proj if merged is None else merged + gate * proj
    merged = merged.astype(BF16)
    for r0 in range(0, x_ref.shape[0], MERGE_ROWS):
        rows = slice(r0, r0 + MERGE_ROWS)
        out = jnp.dot(merged[rows], wo_ref[...], preferred_element_type=F32)
        o_ref[rows, :] = _layer_norm(ALPHA * x_ref[rows, :] + out, g_ref[...], b_ref[...])


def _lse_spread():
    src = jnp.arange(2 * A_HD)[:, None]
    head = jnp.arange(HEAD_BLOCK)[None, :] // A_HD
    first_lane = (head % 2) * A_HD + (head // 2) * LSE_LANES
    once = (src == first_lane).astype(BF16)
    return jnp.concatenate([once, once], axis=0)


def merge_out_ln(x, oa, la, yb, yc, z, wa, wb, wc, wo, g, b):
    t = x.shape[0]
    tm = MERGE_TILE_ROWS
    row = pl.BlockSpec((tm, D_MODEL), lambda i: (i, 0))
    br = pl.BlockSpec((tm, HEAD_BLOCK), lambda i: (i, 0))
    lse = pl.BlockSpec((tm, 2 * A_HD), lambda i: (i, 0))
    w_br = _const_spec((HEAD_BLOCK, D_MODEL))
    return pl.pallas_call(
        _merge_kernel,
        grid=(t // tm,),
        in_specs=[row] + [br] * 3 + [lse] * 3 + [br] * 2 + [
            pl.BlockSpec((tm, GATE_W), lambda i: (i, 0)), _const_spec((4 * A_HD, HEAD_BLOCK)), w_br, w_br, w_br,
            _const_spec((D_MODEL, D_MODEL)), _const_spec((1, D_MODEL)), _const_spec((1, D_MODEL))],
        out_specs=row,
        out_shape=jax.ShapeDtypeStruct((t, D_MODEL), F32),
        compiler_params=_params(("arbitrary",)),
        name="merge_out_ln",
    )(x, *oa, *la, yb, yc, z, _lse_spread(), wa, wb, wc, wo, g, b)


def _split_in_weights(w_in):
    a_w = A_GROUPS * HEAD_BLOCK
    aq, ak, av = w_in[:, :a_w] * (A_HD ** -0.5 * LOG2E), w_in[:, a_w:2 * a_w], w_in[:, 2 * a_w:3 * a_w]
    b_w = 2 * B_HEADS * (B_DK + B_DV)
    rest_b = w_in[:, 3 * a_w:3 * a_w + b_w]
    cq = w_in[:, 3 * a_w + b_w:3 * a_w + b_w + HEAD_BLOCK] * (C_HD ** -0.5 * LOG2E)
    rest = jnp.concatenate([rest_b, cq, w_in[:, 3 * a_w + b_w + HEAD_BLOCK:D_IN - GATE_W]], axis=1)

    def group(g):
        cols = slice(g * HEAD_BLOCK, (g + 1) * HEAD_BLOCK)
        return jnp.concatenate([aq[:, cols], ak[:, cols], av[:, cols]], axis=1)

    main = jnp.concatenate([w_in[:, D_IN - GATE_W:], group(0), rest], axis=1)
    return main.astype(BF16), [group(g).astype(BF16) for g in range(1, A_GROUPS)]


def _trunk(x, layers):
    batch, seq, _ = x.shape
    x = x.reshape(batch * seq, D_MODEL)
    for p in layers:
        dils = tuple(d for _, d in A_PATTERNS[1:])
        x1, x1b, *x1_by_residue = ffn_ln(x, p["wg1"], p["wu1"], p["wd1"], p["g1"], p["b1"], batch, dils)
        z = in_proj(x1b, p["w_main"])
        oa, la = [], []
        o, l = mixer_a_group(z.reshape(batch, 1, seq, MAIN_W), 1, A_QKV_BLK)
        oa.append(o.reshape(batch * seq, HEAD_BLOCK))
        la.append(l.reshape(batch * seq, 2 * A_HD))
        for dilation, xg, w_g in zip(dils, x1_by_residue, p["w_groups"]):
            zg = in_proj(xg.reshape(batch * seq, D_MODEL), w_g)
            o, l = mixer_a_group(zg.reshape(batch, dilation, seq // dilation, A_QKV_W), dilation, 0)
            oa.append(_by_token(o, batch, seq))
            la.append(_by_token(l, batch, seq))
        yb = mixer_b(z, batch, seq, p["logit_fwd"], p["logit_bwd"])
        yc = mixer_c(z, batch, seq, p["rpb"])
        x2 = merge_out_ln(x1, oa, la, yb, yc, z, p["wa"], p["wb"], p["wc"], p["wo"], p["g2"], p["b2"])
        (x,) = ffn_ln(x2, p["wg2"], p["wu2"], p["wd2"], p["g3"], p["b3"])
    return x.reshape(batch, seq, D_MODEL)


def kernel(x_prompt, x_sample, ffn1_w_gate, ffn1_w_up, ffn1_w_down, ln1_g, ln1_b, w_in, ret_logit_fwd, ret_logit_bwd, na_rpb, w_branch_a, w_branch_b, w_branch_c, w_out, ln2_g, ln2_b, ffn2_w_gate, ffn2_w_up, ffn2_w_down, ln3_g, ln3_b):
    def vec(v):
        return v.astype(F32).reshape(1, D_MODEL)

    layers = []
    for i in range(DEPTH):
        w_main, w_groups = _split_in_weights(w_in[i])
        layers.append(dict(
            wg1=ffn1_w_gate[i].astype(BF16), wu1=ffn1_w_up[i].astype(BF16), wd1=ffn1_w_down[i].astype(BF16),
            g1=vec(ln1_g[i]), b1=vec(ln1_b[i]),
            w_main=w_main, w_groups=w_groups,
            logit_fwd=ret_logit_fwd[i], logit_bwd=ret_logit_bwd[i], rpb=na_rpb[i],
            wa=w_branch_a[i].astype(BF16), wb=w_branch_b[i].astype(BF16), wc=w_branch_c[i].astype(BF16),
            wo=w_out[i].astype(BF16), g2=vec(ln2_g[i]), b2=vec(ln2_b[i]),
            wg2=ffn2_w_gate[i].astype(BF16), wu2=ffn2_w_up[i].astype(BF16), wd2=ffn2_w_down[i].astype(BF16),
            g3=vec(ln3_g[i]), b3=vec(ln3_b[i])))
    return (_trunk(x_prompt, layers), _trunk(x_sample, layers))
```

```python
import functools

import jax
import jax.numpy as jnp
from jax import lax
from jax.experimental import pallas as pl
from jax.experimental.pallas import tpu as pltpu

F32 = jnp.float32
BF16 = jnp.bfloat16

D_MODEL = 1024
DEPTH = 2
D_FF = 2816
LN_EPS = 1e-5
GN_EPS = 1e-5
ALPHA = (2 * DEPTH) ** 0.25

A_PATTERNS = ((128, 1), (512, 4), (2048, 16))
A_GROUPS = len(A_PATTERNS)
A_HEADS = 8
A_HD = 64
A_HALF = 64
LSE_LANES = A_HD // (A_HEADS // 2)
B_HEADS = 4
B_DK = 64
B_DV = 128
B_CHUNK = 128
B_BLOCK = 16
C_HEADS = 8
C_HD = 64
GRID_W = 64
C_KH = 8
C_KW = 16
C_QR = 4
C_STEP = 4
D_IN = 10752
NEG = -1e30
LANES = 128
LOG2E = 1.4426950408889634

GATE_W = 3 * D_MODEL
HEAD_BLOCK = 512
A_QKV_W = 3 * HEAD_BLOCK
MAIN_W = D_IN - (A_GROUPS - 1) * A_QKV_W
A_QKV_BLK = 2
BQ_BLK256, BK_BLK256 = 18, 19
BV_BLK, BG_BLK = 10, 11
C_QKV_BLK = 4

VMEM_LIMIT = 56 * 1024 * 1024

FFN_ROWS = 1024
PROJ_ROWS = 2048
PROJ_COLS = 2560
MERGE_TILE_ROWS = 1024
A_TILE = 2048
FF_CHUNKS = ((0, 512), (512, 1024), (1024, 1536), (1536, 2048), (2048, 2560), (2560, 2816))
FF_NORM_ROWS = 256
MERGE_ROWS = 256


def _params(sem):
    return pltpu.CompilerParams(dimension_semantics=sem, vmem_limit_bytes=VMEM_LIMIT)


def _const_spec(shape):
    zeros = (0,) * len(shape)
    return pl.BlockSpec(shape, lambda *_: zeros)


def _layer_norm(r, g, b):
    mu = jnp.mean(r, axis=-1, keepdims=True)
    c = r - mu
    var = jnp.mean(c * c, axis=-1, keepdims=True)
    return c * lax.rsqrt(var + LN_EPS) * g + b


def _ffn_ln_kernel(x_ref, wg_ref, wu_ref, wd_ref, g_ref, b_ref, o_ref, *rest, dilations):
    x = x_ref[...]
    xb = x.astype(BF16)
    hidden = []
    for c0, c1 in FF_CHUNKS:
        gate = jnp.dot(xb, wg_ref[:, c0:c1], preferred_element_type=F32)
        up = jnp.dot(xb, wu_ref[:, c0:c1], preferred_element_type=F32)
        hidden.append((gate * jax.nn.sigmoid(gate) * up).astype(BF16))
    hidden = jnp.concatenate(hidden, axis=1)
    if rest:
        ob_ref, *og_refs, slabs = rest
    n_slabs = D_MODEL // LANES
    rows = FF_NORM_ROWS
    for r0 in range(0, x.shape[0], rows):
        acc = jnp.dot(hidden[r0:r0 + rows], wd_ref[...], preferred_element_type=F32)
        y = _layer_norm(ALPHA * x[r0:r0 + rows] + 0.5 * acc, g_ref[...], b_ref[...])
        o_ref[r0:r0 + rows, :] = y
        if not rest:
            continue
        ob_ref[r0:r0 + rows, :] = y.astype(BF16)
        done = 1
        ordered = y
        for d, og_ref in zip(dilations, og_refs):
            for s in range(n_slabs):
                slabs[s] = ordered[:, s * LANES:(s + 1) * LANES]
            step = d // done
            group = rows // done
            pieces = []
            for r in range(d):
                first = (r % done) * group + r // done
                picked = [slabs[s, pl.ds(first, rows // d, stride=step), :] for s in range(n_slabs)]
                pieces.append(jnp.concatenate(picked, axis=1))
                og_ref[0, r, r0 // d:(r0 + rows) // d, :] = pieces[-1].astype(BF16)
            ordered = jnp.concatenate(pieces, axis=0)
            done = d


def ffn_ln(x, wg, wu, wd, g, b, batch=None, dilations=()):
    t = x.shape[0]
    tm = FFN_ROWS
    row = pl.BlockSpec((tm, D_MODEL), lambda i: (i, 0))
    out_shape = [jax.ShapeDtypeStruct((t, D_MODEL), F32)]
    out_specs = [row]
    scratch = []
    if dilations:
        assert all(b % a == 0 for a, b in zip((1,) + dilations, dilations)), dilations
        tiles_per_seq = t // batch // tm
        out_shape.append(jax.ShapeDtypeStruct((t, D_MODEL), BF16))
        out_specs.append(row)
        for d in dilations:
            out_shape.append(jax.ShapeDtypeStruct((batch, d, t // batch // d, D_MODEL), BF16))
            out_specs.append(pl.BlockSpec((1, d, tm // d, D_MODEL),
                                          lambda i: (i // tiles_per_seq, 0, i % tiles_per_seq, 0)))
        scratch = [pltpu.VMEM((D_MODEL // LANES, FF_NORM_ROWS, LANES), F32)]
    return pl.pallas_call(
        functools.partial(_ffn_ln_kernel, dilations=dilations),
        grid=(t // tm,),
        in_specs=[row, _const_spec((D_MODEL, D_FF)), _const_spec((D_MODEL, D_FF)), _const_spec((D_FF, D_MODEL)),
                  _const_spec((1, D_MODEL)), _const_spec((1, D_MODEL))],
        out_specs=out_specs,
        out_shape=out_shape,
        scratch_shapes=scratch,
        compiler_params=_params(("arbitrary",)),
        name="ffn_ln",
    )(x, wg, wu, wd, g, b)


def _in_proj_kernel(x_ref, w_ref, z_ref):
    z_ref[...] = jnp.dot(x_ref[...], w_ref[...], preferred_element_type=F32).astype(BF16)


def in_proj(xb, w_in):
    t = xb.shape[0]
    width = w_in.shape[1]
    tm = PROJ_ROWS
    tn = PROJ_COLS if width % PROJ_COLS == 0 else A_QKV_W
    return pl.pallas_call(
        _in_proj_kernel,
        grid=(t // tm, width // tn),
        in_specs=[pl.BlockSpec((tm, D_MODEL), lambda i, j: (i, 0)), pl.BlockSpec((D_MODEL, tn), lambda i, j: (0, j))],
        out_specs=pl.BlockSpec((tm, tn), lambda i, j: (i, j)),
        out_shape=jax.ShapeDtypeStruct((t, width), BF16),
        compiler_params=_params(("arbitrary", "arbitrary")),
        name="in_proj",
    )(xb, w_in)


def _pair_scores(q_pair, k_pair, left):
    zero = jnp.zeros_like(q_pair)
    dims = (((1,), (1,)), ((), ()))
    return [lax.dot_general(jnp.where(left, q_pair, zero), k_pair, dims, preferred_element_type=F32),
            lax.dot_general(jnp.where(left, zero, q_pair), k_pair, dims, preferred_element_type=F32)]


def _pair_weighted_sum(probs, v_pair, left):
    rhs = []
    for hh in range(2):
        sel = left if hh == 0 else jnp.logical_not(left)
        ones = jnp.broadcast_to(jnp.where(sel, 1.0, 0.0).astype(BF16), v_pair.shape)
        rhs.append(jnp.concatenate([jnp.where(sel, v_pair, jnp.zeros_like(v_pair)), ones], axis=1))
    out = jnp.dot(jnp.concatenate(probs, axis=1), jnp.concatenate(rhs, axis=0), preferred_element_type=F32)
    width = v_pair.shape[1]
    return out[:, :width], out[:, width:]


A_QB = 128
A_KB = A_QB + 2 * A_HALF


def _mixer_a_kernel(before_ref, cur_ref, after_ref, bias_ref, o_ref, l_ref, kf, vf, *, tl, seq_len):
    i = pl.program_id(2)
    k_cols = slice(HEAD_BLOCK, 2 * HEAD_BLOCK)
    v_cols = slice(2 * HEAD_BLOCK, 3 * HEAD_BLOCK)
    lane = lax.broadcasted_iota(jnp.int32, (1, 2 * A_HD), 1)
    left = lane < A_HD
    lse_slot = (lane % A_HD) // LSE_LANES

    for res in range(cur_ref.shape[1]):
        kf[res, 0:A_HALF, :] = before_ref[0, res, :, k_cols]
        kf[res, A_HALF:A_HALF + tl, :] = cur_ref[0, res, :, k_cols]
        kf[res, A_HALF + tl:, :] = after_ref[0, res, :, k_cols]
        vf[res, 0:A_HALF, :] = before_ref[0, res, :, v_cols]
        vf[res, A_HALF:A_HALF + tl, :] = cur_ref[0, res, :, v_cols]
        vf[res, A_HALF + tl:, :] = after_ref[0, res, :, v_cols]

        for j in range(tl // A_QB):
            q0 = j * A_QB
            start = i * tl + q0
            variant = (start == 0).astype(jnp.int32) + 2 * (start + A_QB == seq_len).astype(jnp.int32)
            lse_all = None
            for hp in range(A_HEADS // 2):
                cols = slice(hp * 2 * A_HD, (hp + 1) * 2 * A_HD)
                qp = cur_ref[0, res, q0:q0 + A_QB, cols]
                kp = kf[res, q0:q0 + A_KB, cols]
                probs, tops = [], []
                for hh, s in enumerate(_pair_scores(qp, kp, left)):
                    s = s + bias_ref[variant, hp * 2 + hh]
                    m = jnp.max(s, axis=-1, keepdims=True)
                    probs.append(jnp.exp2(s - m).astype(BF16))
                    tops.append(m)
                num, den = _pair_weighted_sum(probs, vf[res, q0:q0 + A_KB, cols], left)
                o_ref[0, res, q0:q0 + A_QB, cols] = (num / den).astype(BF16)
                lse_pair = jnp.where(left, tops[0], tops[1]) + jnp.log2(den)
                lse_all = lse_pair if hp == 0 else jnp.where(lse_slot == hp, lse_pair, lse_all)
            l_ref[0, res, q0:q0 + A_QB, :] = lse_all


def _alibi_bias(dilation):
    slopes = 2.0 ** (-8.0 * jnp.arange(1, A_HEADS + 1, dtype=F32) / A_HEADS)
    key = jnp.arange(A_KB)[None, :] - A_HALF
    rel = key - jnp.arange(A_QB)[:, None]
    dist = (jnp.abs(rel) * dilation).astype(F32)
    bias = -slopes[:, None, None] * dist[None] * LOG2E
    in_window = jnp.abs(rel) <= A_HALF
    variants = []
    for v in range(4):
        ok = in_window
        if v & 1:
            ok = ok & (key >= 0)
        if v & 2:
            ok = ok & (key < A_QB)
        variants.append(jnp.where(ok[None], bias, NEG))
    return jnp.stack(variants, 0)


def mixer_a_group(zg, dilation, qkv_blk):
    batch, _, strided_len, _ = zg.shape
    tl = min(A_TILE, strided_len)
    n_res = min(A_TILE // tl, dilation)
    halo_per_tile = tl // A_HALF
    n_halo = strided_len // A_HALF
    out_dims = (batch, dilation, strided_len, HEAD_BLOCK)
    return pl.pallas_call(
        functools.partial(_mixer_a_kernel, tl=tl, seq_len=strided_len),
        grid=(batch, dilation // n_res, strided_len // tl),
        in_specs=[pl.BlockSpec((1, n_res, A_HALF, A_QKV_W),
                               lambda b, r, i: (b, r, jnp.maximum(i * halo_per_tile - 1, 0), qkv_blk)),
                  pl.BlockSpec((1, n_res, tl, A_QKV_W), lambda b, r, i: (b, r, i, qkv_blk)),
                  pl.BlockSpec((1, n_res, A_HALF, A_QKV_W),
                               lambda b, r, i: (b, r, jnp.minimum((i + 1) * halo_per_tile, n_halo - 1), qkv_blk)),
                  _const_spec((4, A_HEADS, A_QB, A_KB))],
        out_specs=[pl.BlockSpec((1, n_res, tl, HEAD_BLOCK), lambda b, r, i: (b, r, i, 0)),
                   pl.BlockSpec((1, n_res, tl, 2 * A_HD), lambda b, r, i: (b, r, i, 0))],
        out_shape=[jax.ShapeDtypeStruct(out_dims, BF16),
                   jax.ShapeDtypeStruct((batch, dilation, strided_len, 2 * A_HD), F32)],
        scratch_shapes=[pltpu.VMEM((n_res, tl + 2 * A_HALF, HEAD_BLOCK), BF16),
                        pltpu.VMEM((n_res, tl + 2 * A_HALF, HEAD_BLOCK), BF16)],
        compiler_params=_params(("arbitrary", "arbitrary", "arbitrary")),
        name=f"mixer_a_d{dilation}",
    )(zg, zg, zg, _alibi_bias(dilation))


def _by_token(x, batch, seq):
    return x.transpose(0, 2, 1, 3).reshape(batch * seq, x.shape[-1])


def _mixer_b_kernel(q_ref, k_ref, v_ref, g_ref, dmat_ref, qdf_ref, qdb_ref, kdf_ref, kdb_ref, cdf_ref, cdb_ref,
                    o_ref, fwd_state, bwd_state, bwd_store, *, n_blocks):
    phase = pl.program_id(1)
    n = pl.program_id(2)
    qk_w = B_HEADS * B_DK
    head_of_lane = lax.broadcasted_iota(jnp.int32, (1, qk_w), 1) // B_DK

    def rows(c):
        return slice(c * B_CHUNK, (c + 1) * B_CHUNK)

    def stacked_heads(t):
        lane_head = jnp.concatenate([head_of_lane] * (t.shape[1] // qk_w), axis=1)
        return jnp.concatenate([jnp.where(lane_head == h, t, jnp.zeros_like(t)) for h in range(B_HEADS)], axis=0)

    def head_block_diagonal(states):
        row_head = (lax.broadcasted_iota(jnp.int32, (states.shape[0], 1), 0) % qk_w) // B_DK
        return jnp.concatenate([jnp.where(row_head == h, states, jnp.zeros_like(states)) for h in range(B_HEADS)],
                               axis=1)

    def kv_outer(k_decayed, v):
        full = lax.dot_general(k_decayed, v, (((0,), (0,)), ((), ())), preferred_element_type=F32)
        return jnp.concatenate([full[h * B_DK:(h + 1) * B_DK, h * B_DV:(h + 1) * B_DV] for h in range(B_HEADS)], axis=0)

    def scaled_k(c):
        return k_ref[0, rows(c), :] * (B_DK ** -0.5)

    @pl.when(phase == 0)
    def _():
        @pl.when(n == 0)
        def _():
            bwd_state[...] = jnp.zeros_like(bwd_state)

        blk = n_blocks - 1 - n
        state = bwd_state[...]
        for c in reversed(range(B_BLOCK)):
            bwd_store[blk * B_BLOCK + c] = state.astype(BF16)
            k_dec = (scaled_k(c).astype(F32) * kdb_ref[...]).astype(BF16)
            state = cdb_ref[...] * state + kv_outer(k_dec, v_ref[0, rows(c), :])
        bwd_state[...] = state

    @pl.when(phase == 1)
    def _():
        @pl.when(n == 0)
        def _():
            fwd_state[...] = jnp.zeros_like(fwd_state)

        state = fwd_state[...]
        for c in range(B_BLOCK):
            q = q_ref[0, rows(c), :]
            k = scaled_k(c)
            v = v_ref[0, rows(c), :]
            q32 = q.astype(F32)
            q_dec = jnp.concatenate([(q32 * qdf_ref[...]).astype(BF16), (q32 * qdb_ref[...]).astype(BF16)], axis=1)
            k_dec = (k.astype(F32) * kdf_ref[...]).astype(BF16)
            states = jnp.concatenate([state.astype(BF16), bwd_store[n * B_BLOCK + c]], axis=0)
            s_all = lax.dot_general(q, stacked_heads(k), (((1,), (1,)), ((), ())), preferred_element_type=F32)
            cross_all = jnp.dot(q_dec, head_block_diagonal(states), preferred_element_type=F32)
            for h in range(B_HEADS):
                vh = v[:, h * B_DV:(h + 1) * B_DV]
                inner = jnp.dot((s_all[:, rows(h)] * dmat_ref[h]).astype(BF16), vh, preferred_element_type=F32)
                y = inner + cross_all[:, h * B_DV:(h + 1) * B_DV]
                mu = jnp.mean(y, axis=-1, keepdims=True)
                cen = y - mu
                var = jnp.mean(cen * cen, axis=-1, keepdims=True)
                yn = cen * lax.rsqrt(var + GN_EPS)
                gate = g_ref[0, rows(c), h * B_DV:(h + 1) * B_DV].astype(F32)
                o_ref[0, rows(c), h * B_DV:(h + 1) * B_DV] = (gate * jax.nn.sigmoid(gate) * yn).astype(BF16)
            state = cdf_ref[...] * state + kv_outer(k_dec, v)
        fwd_state[...] = state


def _retention_tables(logit_fwd, logit_bwd):
    lg_f = jax.nn.log_sigmoid(logit_fwd.astype(F32))
    lg_b = jax.nn.log_sigmoid(logit_bwd.astype(F32))
    idx = jnp.arange(B_CHUNK, dtype=F32)
    diff = idx[:, None] - idx[None, :]
    causal = diff >= 0
    dmat = jnp.where(causal[None],
                     jnp.exp(lg_f[:, None, None] * jnp.where(causal, diff, 0.0)[None]),
                     jnp.exp(lg_b[:, None, None] * jnp.where(causal, 0.0, -diff)[None]))

    def per_lane(lg, power):
        return jnp.repeat(jnp.exp(lg[None, :] * power[:, None]), B_DK, axis=1)

    def per_row(lg):
        return jnp.broadcast_to(jnp.repeat(jnp.exp(lg * B_CHUNK), B_DK)[:, None], (B_HEADS * B_DK, B_DV))

    return (dmat, per_lane(lg_f, idx + 1), per_lane(lg_b, B_CHUNK - idx), per_lane(lg_f, B_CHUNK - 1 - idx),
            per_lane(lg_b, idx), per_row(lg_f), per_row(lg_b))


def mixer_b(z, batch, seq, logit_fwd, logit_bwd):
    n_chunks = seq // B_CHUNK
    n_blocks = n_chunks // B_BLOCK
    block_rows = B_BLOCK * B_CHUNK
    zv = z.reshape(batch, seq, MAIN_W)
    qk_w = B_HEADS * B_DK
    v_w = B_HEADS * B_DV

    def scan_block(ph, n):
        return (1 - ph) * (n_blocks - 1 - n) + ph * n

    tables = _retention_tables(logit_fwd, logit_bwd)
    in_specs = [pl.BlockSpec((1, block_rows, qk_w), lambda b, ph, n: (b, ph * n, BQ_BLK256)),
                pl.BlockSpec((1, block_rows, qk_w), lambda b, ph, n: (b, scan_block(ph, n), BK_BLK256)),
                pl.BlockSpec((1, block_rows, v_w), lambda b, ph, n: (b, scan_block(ph, n), BV_BLK)),
                pl.BlockSpec((1, block_rows, v_w), lambda b, ph, n: (b, ph * n, BG_BLK))]
    in_specs += [_const_spec(t.shape) for t in tables]
    out = pl.pallas_call(
        functools.partial(_mixer_b_kernel, n_blocks=n_blocks),
        grid=(batch, 2, n_blocks),
        in_specs=in_specs,
        out_specs=pl.BlockSpec((1, block_rows, v_w), lambda b, ph, n: (b, ph * n, 0)),
        out_shape=jax.ShapeDtypeStruct((batch, seq, v_w), BF16),
        scratch_shapes=[pltpu.VMEM((qk_w, B_DV), F32), pltpu.VMEM((qk_w, B_DV), F32),
                        pltpu.VMEM((n_chunks, qk_w, B_DV), BF16)],
        compiler_params=_params(("arbitrary", "arbitrary", "arbitrary")),
        name="mixer_b",
    )(zv, zv, zv, zv, *tables)
    return out.reshape(batch * seq, v_w)


C_TQ = C_QR * GRID_W
C_KROWS = 3 * C_QR
C_TK = C_KROWS * GRID_W
C_PAIRS = C_KROWS // 2
C_NTAB = 2 * C_KH - 2


def _mixer_c_kernel(above_ref, cur_ref, below_ref, tab_ref, o_ref, kf, vf, *, rows):
    cur = C_STEP * C_TQ
    k_cols = slice(HEAD_BLOCK, 2 * HEAD_BLOCK)
    v_cols = slice(2 * HEAD_BLOCK, 3 * HEAD_BLOCK)
    kf[0:C_TQ, :] = above_ref[0, :, k_cols]
    kf[C_TQ:C_TQ + cur, :] = cur_ref[0, :, k_cols]
    kf[C_TQ + cur:, :] = below_ref[0, :, k_cols]
    vf[0:C_TQ, :] = above_ref[0, :, v_cols]
    vf[C_TQ:C_TQ + cur, :] = cur_ref[0, :, v_cols]
    vf[C_TQ + cur:, :] = below_ref[0, :, v_cols]

    left = lax.broadcasted_iota(jnp.int32, (1, 2 * C_HD), 1) < C_HD
    for sb in range(C_STEP):
        blk = pl.program_id(1) * C_STEP + sb
        q_rows = slice(sb * C_TQ, (sb + 1) * C_TQ)
        k_rows = slice(sb * C_TQ, sb * C_TQ + C_TK)
        key_row = blk * C_QR - C_QR + lax.broadcasted_iota(jnp.int32, (1, C_TK), 1) // GRID_W
        pens = []
        for a in range(C_QR):
            row_start = jnp.clip(blk * C_QR + a - C_KH // 2, 0, rows - C_KH)
            pens.append(jnp.where((key_row >= row_start) & (key_row < row_start + C_KH), 0.0, NEG).astype(F32))

        for hp in range(C_HEADS // 2):
            cols = slice(hp * 2 * C_HD, (hp + 1) * 2 * C_HD)
            qp = cur_ref[0, q_rows, cols]
            kp = kf[k_rows, cols]
            probs = []
            for hh, s in enumerate(_pair_scores(qp, kp, left)):
                h = hp * 2 + hh
                rows_p = []
                for a in range(C_QR):
                    bias = jnp.concatenate([tab_ref[h, 2 * t - C_QR - a + C_KH - 1] for t in range(C_PAIRS)], axis=1)
                    sa = s[a * GRID_W:(a + 1) * GRID_W, :] + bias + pens[a]
                    m = jnp.max(sa, axis=-1, keepdims=True)
                    rows_p.append(jnp.exp2(sa - m).astype(BF16))
                probs.append(jnp.concatenate(rows_p, axis=0))
            num, den = _pair_weighted_sum(probs, vf[k_rows, cols], left)
            o_ref[0, q_rows, cols] = (num / den).astype(BF16)


def _neighbourhood_bias(rpb):
    qc = jnp.arange(GRID_W)[:, None]
    kc = jnp.arange(GRID_W)[None, :]
    col_start = jnp.clip(qc - C_KW // 2, 0, GRID_W - C_KW)
    col_ok = (kc >= col_start) & (kc < col_start + C_KW)
    onehot = ((kc - qc + (C_KW - 1))[:, :, None] == jnp.arange(2 * C_KW - 1)[None, None, :]).astype(F32)
    band = jnp.einsum("hrd,qkd->hrqk", rpb.astype(F32), onehot, precision=lax.Precision.HIGHEST)
    band = jnp.where(col_ok[None, None], band * LOG2E, NEG)
    return jnp.concatenate([band[:, :C_NTAB], band[:, 1:C_NTAB + 1]], axis=-1)


def mixer_c(z, batch, seq, rpb):
    rows = seq // GRID_W
    n_blk = rows // C_QR
    zv = z.reshape(batch, seq, MAIN_W)
    tab = _neighbourhood_bias(rpb)
    halo = (1, C_TQ, A_QKV_W)
    tile = (1, C_STEP * C_TQ, A_QKV_W)
    out = pl.pallas_call(
        functools.partial(_mixer_c_kernel, rows=rows),
        grid=(batch, n_blk // C_STEP),
        in_specs=[pl.BlockSpec(halo, lambda b, i: (b, jnp.maximum(i * C_STEP - 1, 0), C_QKV_BLK)),
                  pl.BlockSpec(tile, lambda b, i: (b, i, C_QKV_BLK)),
                  pl.BlockSpec(halo, lambda b, i: (b, jnp.minimum((i + 1) * C_STEP, n_blk - 1), C_QKV_BLK)),
                  _const_spec(tab.shape)],
        out_specs=pl.BlockSpec((1, C_STEP * C_TQ, HEAD_BLOCK), lambda b, i: (b, i, 0)),
        out_shape=jax.ShapeDtypeStruct((batch, seq, HEAD_BLOCK), BF16),
        scratch_shapes=[pltpu.VMEM(((C_STEP + 2) * C_TQ, HEAD_BLOCK), BF16),
                        pltpu.VMEM(((C_STEP + 2) * C_TQ, HEAD_BLOCK), BF16)],
        compiler_params=_params(("arbitrary", "arbitrary")),
        name="mixer_c",
    )(zv, zv, zv, tab)
    return out.reshape(batch * seq, HEAD_BLOCK)


def _merge_kernel(x_ref, oa0_ref, oa1_ref, oa2_ref, la0_ref, la1_ref, la2_ref, yb_ref, yc_ref, gates_ref, spread_ref,
                  wa_ref, wb_ref, wc_ref, wo_ref, g_ref, b_ref, o_ref):
    lses = [la0_ref[...], la1_ref[...], la2_ref[...]]
    top = jnp.maximum(jnp.maximum(lses[0], lses[1]), lses[2])
    weights = [jnp.exp2(lse - top) for lse in lses]
    total = weights[0] + weights[1] + weights[2]
    ya = None
    for o_g_ref, w in zip((oa0_ref, oa1_ref, oa2_ref), weights):
        share = w / total
        hi = share.astype(BF16)
        lo = (share - hi.astype(F32)).astype(BF16)
        wide = jnp.dot(jnp.concatenate([hi, lo], axis=1), spread_ref[...], preferred_element_type=F32)
        term = wide * o_g_ref[...].astype(F32)
        ya = term if ya is None else ya + term
    ya = ya.astype(BF16)

    merged = None
    for br, (y, w_ref) in enumerate(((ya, wa_ref), (yb_ref[...], wb_ref), (yc_ref[...], wc_ref))):
        proj = jnp.dot(y, w_ref[...], preferred_element_type=F32)
        gate = jax.nn.sigmoid(gates_ref[:, br * D_MODEL:(br + 1) * D_MODEL].astype(F32))
        merged = gate * proj if merged is None else merged + gate * proj
    merged = merged.astype(BF16)
    for r0 in range(0, x_ref.shape[0], MERGE_ROWS):
        rows = slice(r0, r0 + MERGE_ROWS)
        out = jnp.dot(merged[rows], wo_ref[...], preferred_element_type=F32)
        o_ref[rows, :] = _layer_norm(ALPHA * x_ref[rows, :] + out, g_ref[...], b_ref[...])


def _lse_spread():
    src = jnp.arange(2 * A_HD)[:, None]
    head = jnp.arange(HEAD_BLOCK)[None, :] // A_HD
    first_lane = (head % 2) * A_HD + (head // 2) * LSE_LANES
    once = (src == first_lane).astype(BF16)
    return jnp.concatenate([once, once], axis=0)


def merge_out_ln(x, oa, la, yb, yc, z, wa, wb, wc, wo, g, b):
    t = x.shape[0]
    tm = MERGE_TILE_ROWS
    row = pl.BlockSpec((tm, D_MODEL), lambda i: (i, 0))
    br = pl.BlockSpec((tm, HEAD_BLOCK), lambda i: (i, 0))
    lse = pl.BlockSpec((tm, 2 * A_HD), lambda i: (i, 0))
    w_br = _const_spec((HEAD_BLOCK, D_MODEL))
    return pl.pallas_call(
        _merge_kernel,
        grid=(t // tm,),
        in_specs=[row] + [br] * 3 + [lse] * 3 + [br] * 2 + [
            pl.BlockSpec((tm, GATE_W), lambda i: (i, 0)), _const_spec((4 * A_HD, HEAD_BLOCK)), w_br, w_br, w_br,
            _const_spec((D_MODEL, D_MODEL)), _const_spec((1, D_MODEL)), _const_spec((1, D_MODEL))],
        out_specs=row,
        out_shape=jax.ShapeDtypeStruct((t, D_MODEL), F32),
        compiler_params=_params(("arbitrary",)),
        name="merge_out_ln",
    )(x, *oa, *la, yb, yc, z, _lse_spread(), wa, wb, wc, wo, g, b)


def _split_in_weights(w_in):
    a_w = A_GROUPS * HEAD_BLOCK
    aq, ak, av = w_in[:, :a_w] * (A_HD ** -0.5 * LOG2E), w_in[:, a_w:2 * a_w], w_in[:, 2 * a_w:3 * a_w]
    b_w = 2 * B_HEADS * (B_DK + B_DV)
    rest_b = w_in[:, 3 * a_w:3 * a_w + b_w]
    cq = w_in[:, 3 * a_w + b_w:3 * a_w + b_w + HEAD_BLOCK] * (C_HD ** -0.5 * LOG2E)
    rest = jnp.concatenate([rest_b, cq, w_in[:, 3 * a_w + b_w + HEAD_BLOCK:D_IN - GATE_W]], axis=1)

    def group(g):
        cols = slice(g * HEAD_BLOCK, (g + 1) * HEAD_BLOCK)
        return jnp.concatenate([aq[:, cols], ak[:, cols], av[:, cols]], axis=1)

    main = jnp.concatenate([w_in[:, D_IN - GATE_W:], group(0), rest], axis=1)
    return main.astype(BF16), [group(g).astype(BF16) for g in range(1, A_GROUPS)]


def _trunk(x, layers):
    batch, seq, _ = x.shape
    x = x.reshape(batch * seq, D_MODEL)
    for p in layers:
        dils = tuple(d for _, d in A_PATTERNS[1:])
        x1, x1b, *x1_by_residue = ffn_ln(x, p["wg1"], p["wu1"], p["wd1"], p["g1"], p["b1"], batch, dils)
        z = in_proj(x1b, p["w_main"])
        oa, la = [], []
        o, l = mixer_a_group(z.reshape(batch, 1, seq, MAIN_W), 1, A_QKV_BLK)
        oa.append(o.reshape(batch * seq, HEAD_BLOCK))
        la.append(l.reshape(batch * seq, 2 * A_HD))
        for dilation, xg, w_g in zip(dils, x1_by_residue, p["w_groups"]):
            zg = in_proj(xg.reshape(batch * seq, D_MODEL), w_g)
            o, l = mixer_a_group(zg.reshape(batch, dilation, seq // dilation, A_QKV_W), dilation, 0)
            oa.append(_by_token(o, batch, seq))
            la.append(_by_token(l, batch, seq))
        yb = mixer_b(z, batch, seq, p["logit_fwd"], p["logit_bwd"])
        yc = mixer_c(z, batch, seq, p["rpb"])
        x2 = merge_out_ln(x1, oa, la, yb, yc, z, p["wa"], p["wb"], p["wc"], p["wo"], p["g2"], p["b2"])
        (x,) = ffn_ln(x2, p["wg2"], p["wu2"], p["wd2"], p["g3"], p["b3"])
    return x.reshape(batch, seq, D_MODEL)


def kernel(x_prompt, x_sample, ffn1_w_gate, ffn1_w_up, ffn1_w_down, ln1_g, ln1_b, w_in, ret_logit_fwd, ret_logit_bwd, na_rpb, w_branch_a, w_branch_b, w_branch_c, w_out, ln2_g, ln2_b, ffn2_w_gate, ffn2_w_up, ffn2_w_down, ln3_g, ln3_b):
    def vec(v):
        return v.astype(F32).reshape(1, D_MODEL)

    layers = []
    for i in range(DEPTH):
        w_main, w_groups = _split_in_weights(w_in[i])
        layers.append(dict(
            wg1=ffn1_w_gate[i].astype(BF16), wu1=ffn1_w_up[i].astype(BF16), wd1=ffn1_w_down[i].astype(BF16),
            g1=vec(ln1_g[i]), b1=vec(ln1_b[i]),
            w_main=w_main, w_groups=w_groups,
            logit_fwd=ret_logit_fwd[i], logit_bwd=ret_logit_bwd[i], rpb=na_rpb[i],
            wa=w_branch_a[i].astype(BF16), wb=w_branch_b[i].astype(BF16), wc=w_branch_c[i].astype(BF16),
            wo=w_out[i].astype(BF16), g2=vec(ln2_g[i]), b2=vec(ln2_b[i]),
            wg2=ffn2_w_gate[i].astype(BF16), wu2=ffn2_w_up[i].astype(BF16), wd2=ffn2_w_down[i].astype(BF16),
            g3=vec(ln3_g[i]), b3=vec(ln3_b[i])))
    return (_trunk(x_prompt, layers), _trunk(x_sample, layers))
```

```python
import functools

import jax
import jax.numpy as jnp
from jax import lax
from jax.experimental import pallas as pl
from jax.experimental.pallas import tpu as pltpu

F32 = jnp.float32
BF16 = jnp.bfloat16

D_MODEL = 1024
DEPTH = 2
D_FF = 2816
LN_EPS = 1e-5
GN_EPS = 1e-5
ALPHA = (2 * DEPTH) ** 0.25

A_PATTERNS = ((128, 1), (512, 4), (2048, 16))
A_GROUPS = len(A_PATTERNS)
A_HEADS = 8
A_HD = 64
A_HALF = 64
LSE_LANES = A_HD // (A_HEADS // 2)
B_HEADS = 4
B_DK = 64
B_DV = 128
B_CHUNK = 128
B_BLOCK = 16
C_HEADS = 8
C_HD = 64
GRID_W = 64
C_KH = 8
C_KW = 16
C_QR = 4
C_STEP = 8
D_IN = 10752
NEG = -1e30
LANES = 128
LOG2E = 1.4426950408889634

GATE_W = 3 * D_MODEL
HEAD_BLOCK = 512
A_QKV_W = 3 * HEAD_BLOCK
MAIN_W = D_IN - (A_GROUPS - 1) * A_QKV_W
A_QKV_BLK = 2
BQ_BLK256, BK_BLK256 = 18, 19
BV_BLK, BG_BLK = 10, 11
C_QKV_BLK = 4

VMEM_LIMIT = 56 * 1024 * 1024

FFN_ROWS = 1024
PROJ_ROWS = 2048
PROJ_COLS = 2560
MERGE_TILE_ROWS = 1024
A_TILE = 2048
FF_CHUNKS = ((0, 512), (512, 1024), (1024, 1536), (1536, 2048), (2048, 2560), (2560, 2816))
FF_NORM_ROWS = 256
MERGE_ROWS = 256


def _params(sem):
    return pltpu.CompilerParams(dimension_semantics=sem, vmem_limit_bytes=VMEM_LIMIT)


def _const_spec(shape):
    zeros = (0,) * len(shape)
    return pl.BlockSpec(shape, lambda *_: zeros)


def _layer_norm(r, g, b):
    mu = jnp.mean(r, axis=-1, keepdims=True)
    c = r - mu
    var = jnp.mean(c * c, axis=-1, keepdims=True)
    return c * lax.rsqrt(var + LN_EPS) * g + b


def _ffn_ln_kernel(x_ref, wg_ref, wu_ref, wd_ref, g_ref, b_ref, o_ref, *rest, dilations):
    x = x_ref[...]
    xb = x.astype(BF16)
    hidden = []
    for c0, c1 in FF_CHUNKS:
        gate = jnp.dot(xb, wg_ref[:, c0:c1], preferred_element_type=F32)
        up = jnp.dot(xb, wu_ref[:, c0:c1], preferred_element_type=F32)
        hidden.append((gate * jax.nn.sigmoid(gate) * up).astype(BF16))
    hidden = jnp.concatenate(hidden, axis=1)
    if rest:
        *og_refs, slabs = rest
    n_slabs = D_MODEL // LANES
    rows = FF_NORM_ROWS
    for r0 in range(0, x.shape[0], rows):
        acc = jnp.dot(hidden[r0:r0 + rows], wd_ref[...], preferred_element_type=F32)
        y = _layer_norm(ALPHA * x[r0:r0 + rows] + 0.5 * acc, g_ref[...], b_ref[...])
        o_ref[r0:r0 + rows, :] = y
        if not rest:
            continue
        done = 1
        ordered = y
        for d, og_ref in zip(dilations, og_refs):
            for s in range(n_slabs):
                slabs[s] = ordered[:, s * LANES:(s + 1) * LANES]
            step = d // done
            group = rows // done
            pieces = []
            for r in range(d):
                first = (r % done) * group + r // done
                picked = [slabs[s, pl.ds(first, rows // d, stride=step), :] for s in range(n_slabs)]
                pieces.append(jnp.concatenate(picked, axis=1))
                og_ref[0, r, r0 // d:(r0 + rows) // d, :] = pieces[-1].astype(BF16)
            ordered = jnp.concatenate(pieces, axis=0)
            done = d


def ffn_ln(x, wg, wu, wd, g, b, batch=None, dilations=()):
    t = x.shape[0]
    tm = FFN_ROWS
    row = pl.BlockSpec((tm, D_MODEL), lambda i: (i, 0))
    out_shape = [jax.ShapeDtypeStruct((t, D_MODEL), F32)]
    out_specs = [row]
    scratch = []
    if dilations:
        assert all(b % a == 0 for a, b in zip((1,) + dilations, dilations)), dilations
        tiles_per_seq = t // batch // tm
        for d in dilations:
            out_shape.append(jax.ShapeDtypeStruct((batch, d, t // batch // d, D_MODEL), BF16))
            out_specs.append(pl.BlockSpec((1, d, tm // d, D_MODEL),
                                          lambda i: (i // tiles_per_seq, 0, i % tiles_per_seq, 0)))
        scratch = [pltpu.VMEM((D_MODEL // LANES, FF_NORM_ROWS, LANES), F32)]
    return pl.pallas_call(
        functools.partial(_ffn_ln_kernel, dilations=dilations),
        grid=(t // tm,),
        in_specs=[row, _const_spec((D_MODEL, D_FF)), _const_spec((D_MODEL, D_FF)), _const_spec((D_FF, D_MODEL)),
                  _const_spec((1, D_MODEL)), _const_spec((1, D_MODEL))],
        out_specs=out_specs,
        out_shape=out_shape,
        scratch_shapes=scratch,
        compiler_params=_params(("arbitrary",)),
        name="ffn_ln",
    )(x, wg, wu, wd, g, b)


def _in_proj_kernel(x_ref, w_ref, z_ref):
    z_ref[...] = jnp.dot(x_ref[...].astype(BF16), w_ref[...], preferred_element_type=F32).astype(BF16)


def in_proj(xb, w_in):
    t = xb.shape[0]
    width = w_in.shape[1]
    tm = PROJ_ROWS
    tn = PROJ_COLS if width % PROJ_COLS == 0 else A_QKV_W
    return pl.pallas_call(
        _in_proj_kernel,
        grid=(t // tm, width // tn),
        in_specs=[pl.BlockSpec((tm, D_MODEL), lambda i, j: (i, 0)), pl.BlockSpec((D_MODEL, tn), lambda i, j: (0, j))],
        out_specs=pl.BlockSpec((tm, tn), lambda i, j: (i, j)),
        out_shape=jax.ShapeDtypeStruct((t, width), BF16),
        compiler_params=_params(("arbitrary", "arbitrary")),
        name="in_proj",
    )(xb, w_in)


def _pair_scores(q_pair, k_pair, left):
    zero = jnp.zeros_like(q_pair)
    dims = (((1,), (1,)), ((), ()))
    return [lax.dot_general(jnp.where(left, q_pair, zero), k_pair, dims, preferred_element_type=F32),
            lax.dot_general(jnp.where(left, zero, q_pair), k_pair, dims, preferred_element_type=F32)]


def _pair_weighted_sum(probs, v_pair, left):
    rhs = []
    for hh in range(2):
        sel = left if hh == 0 else jnp.logical_not(left)
        ones = jnp.broadcast_to(jnp.where(sel, 1.0, 0.0).astype(BF16), v_pair.shape)
        rhs.append(jnp.concatenate([jnp.where(sel, v_pair, jnp.zeros_like(v_pair)), ones], axis=1))
    out = jnp.dot(jnp.concatenate(probs, axis=1), jnp.concatenate(rhs, axis=0), preferred_element_type=F32)
    width = v_pair.shape[1]
    return out[:, :width], out[:, width:]


A_QB = 128
A_KB = A_QB + 2 * A_HALF


def _mixer_a_kernel(before_ref, cur_ref, after_ref, bias_ref, o_ref, l_ref, kf, vf, *, tl, seq_len):
    i = pl.program_id(2)
    k_cols = slice(HEAD_BLOCK, 2 * HEAD_BLOCK)
    v_cols = slice(2 * HEAD_BLOCK, 3 * HEAD_BLOCK)
    lane = lax.broadcasted_iota(jnp.int32, (1, 2 * A_HD), 1)
    left = lane < A_HD
    lse_slot = (lane % A_HD) // LSE_LANES

    for res in range(cur_ref.shape[1]):
        kf[res, 0:A_HALF, :] = before_ref[0, res, :, k_cols]
        kf[res, A_HALF:A_HALF + tl, :] = cur_ref[0, res, :, k_cols]
        kf[res, A_HALF + tl:, :] = after_ref[0, res, :, k_cols]
        vf[res, 0:A_HALF, :] = before_ref[0, res, :, v_cols]
        vf[res, A_HALF:A_HALF + tl, :] = cur_ref[0, res, :, v_cols]
        vf[res, A_HALF + tl:, :] = after_ref[0, res, :, v_cols]

        for j in range(tl // A_QB):
            q0 = j * A_QB
            start = i * tl + q0
            variant = (start == 0).astype(jnp.int32) + 2 * (start + A_QB == seq_len).astype(jnp.int32)
            lse_all = None
            for hp in range(A_HEADS // 2):
                cols = slice(hp * 2 * A_HD, (hp + 1) * 2 * A_HD)
                qp = cur_ref[0, res, q0:q0 + A_QB, cols]
                kp = kf[res, q0:q0 + A_KB, cols]
                probs, tops = [], []
                for hh, s in enumerate(_pair_scores(qp, kp, left)):
                    s = s + bias_ref[variant, hp * 2 + hh]
                    m = jnp.max(s, axis=-1, keepdims=True)
                    probs.append(jnp.exp2(s - m).astype(BF16))
                    tops.append(m)
                num, den = _pair_weighted_sum(probs, vf[res, q0:q0 + A_KB, cols], left)
                o_ref[0, res, q0:q0 + A_QB, cols] = (num / den).astype(BF16)
                lse_pair = jnp.where(left, tops[0], tops[1]) + jnp.log2(den)
                lse_all = lse_pair if hp == 0 else jnp.where(lse_slot == hp, lse_pair, lse_all)
            l_ref[0, res, q0:q0 + A_QB, :] = lse_all


def _alibi_bias(dilation):
    slopes = 2.0 ** (-8.0 * jnp.arange(1, A_HEADS + 1, dtype=F32) / A_HEADS)
    key = jnp.arange(A_KB)[None, :] - A_HALF
    rel = key - jnp.arange(A_QB)[:, None]
    dist = (jnp.abs(rel) * dilation).astype(F32)
    bias = -slopes[:, None, None] * dist[None] * LOG2E
    in_window = jnp.abs(rel) <= A_HALF
    variants = []
    for v in range(4):
        ok = in_window
        if v & 1:
            ok = ok & (key >= 0)
        if v & 2:
            ok = ok & (key < A_QB)
        variants.append(jnp.where(ok[None], bias, NEG))
    return jnp.stack(variants, 0)


def mixer_a_group(zg, dilation, qkv_blk):
    batch, _, strided_len, _ = zg.shape
    tl = min(A_TILE, strided_len)
    n_res = min(A_TILE // tl, dilation)
    halo_per_tile = tl // A_HALF
    n_halo = strided_len // A_HALF
    out_dims = (batch, dilation, strided_len, HEAD_BLOCK)
    return pl.pallas_call(
        functools.partial(_mixer_a_kernel, tl=tl, seq_len=strided_len),
        grid=(batch, dilation // n_res, strided_len // tl),
        in_specs=[pl.BlockSpec((1, n_res, A_HALF, A_QKV_W),
                               lambda b, r, i: (b, r, jnp.maximum(i * halo_per_tile - 1, 0), qkv_blk)),
                  pl.BlockSpec((1, n_res, tl, A_QKV_W), lambda b, r, i: (b, r, i, qkv_blk)),
                  pl.BlockSpec((1, n_res, A_HALF, A_QKV_W),
                               lambda b, r, i: (b, r, jnp.minimum((i + 1) * halo_per_tile, n_halo - 1), qkv_blk)),
                  _const_spec((4, A_HEADS, A_QB, A_KB))],
        out_specs=[pl.BlockSpec((1, n_res, tl, HEAD_BLOCK), lambda b, r, i: (b, r, i, 0)),
                   pl.BlockSpec((1, n_res, tl, 2 * A_HD), lambda b, r, i: (b, r, i, 0))],
        out_shape=[jax.ShapeDtypeStruct(out_dims, BF16),
                   jax.ShapeDtypeStruct((batch, dilation, strided_len, 2 * A_HD), F32)],
        scratch_shapes=[pltpu.VMEM((n_res, tl + 2 * A_HALF, HEAD_BLOCK), BF16),
                        pltpu.VMEM((n_res, tl + 2 * A_HALF, HEAD_BLOCK), BF16)],
        compiler_params=_params(("arbitrary", "arbitrary", "arbitrary")),
        name=f"mixer_a_d{dilation}",
    )(zg, zg, zg, _alibi_bias(dilation))


def _by_token(x, batch, seq):
    return x.transpose(0, 2, 1, 3).reshape(batch * seq, x.shape[-1])


def _mixer_b_kernel(q_ref, k_ref, v_ref, g_ref, dmat_ref, qdf_ref, qdb_ref, kdf_ref, kdb_ref, cdf_ref, cdb_ref,
                    o_ref, fwd_state, bwd_state, bwd_store, *, n_blocks):
    phase = pl.program_id(1)
    n = pl.program_id(2)
    qk_w = B_HEADS * B_DK
    head_of_lane = lax.broadcasted_iota(jnp.int32, (1, qk_w), 1) // B_DK

    def rows(c):
        return slice(c * B_CHUNK, (c + 1) * B_CHUNK)

    def stacked_heads(t):
        lane_head = jnp.concatenate([head_of_lane] * (t.shape[1] // qk_w), axis=1)
        return jnp.concatenate([jnp.where(lane_head == h, t, jnp.zeros_like(t)) for h in range(B_HEADS)], axis=0)

    def head_block_diagonal(states):
        row_head = (lax.broadcasted_iota(jnp.int32, (states.shape[0], 1), 0) % qk_w) // B_DK
        return jnp.concatenate([jnp.where(row_head == h, states, jnp.zeros_like(states)) for h in range(B_HEADS)],
                               axis=1)

    def kv_outer(k_decayed, v):
        full = lax.dot_general(k_decayed, v, (((0,), (0,)), ((), ())), preferred_element_type=F32)
        return jnp.concatenate([full[h * B_DK:(h + 1) * B_DK, h * B_DV:(h + 1) * B_DV] for h in range(B_HEADS)], axis=0)

    def scaled_k(c):
        return k_ref[0, rows(c), :] * (B_DK ** -0.5)

    @pl.when(phase == 0)
    def _():
        @pl.when(n == 0)
        def _():
            bwd_state[...] = jnp.zeros_like(bwd_state)

        blk = n_blocks - 1 - n
        state = bwd_state[...]
        for c in reversed(range(B_BLOCK)):
            bwd_store[blk * B_BLOCK + c] = state.astype(BF16)
            k_dec = (scaled_k(c).astype(F32) * kdb_ref[...]).astype(BF16)
            state = cdb_ref[...] * state + kv_outer(k_dec, v_ref[0, rows(c), :])
        bwd_state[...] = state

    @pl.when(phase == 1)
    def _():
        @pl.when(n == 0)
        def _():
            fwd_state[...] = jnp.zeros_like(fwd_state)

        state = fwd_state[...]
        for c in range(B_BLOCK):
            q = q_ref[0, rows(c), :]
            k = scaled_k(c)
            v = v_ref[0, rows(c), :]
            q32 = q.astype(F32)
            q_dec = jnp.concatenate([(q32 * qdf_ref[...]).astype(BF16), (q32 * qdb_ref[...]).astype(BF16)], axis=1)
            k_dec = (k.astype(F32) * kdf_ref[...]).astype(BF16)
            states = jnp.concatenate([state.astype(BF16), bwd_store[n * B_BLOCK + c]], axis=0)
            s_all = lax.dot_general(q, stacked_heads(k), (((1,), (1,)), ((), ())), preferred_element_type=F32)
            cross_all = jnp.dot(q_dec, head_block_diagonal(states), preferred_element_type=F32)
            for h in range(B_HEADS):
                vh = v[:, h * B_DV:(h + 1) * B_DV]
                inner = jnp.dot((s_all[:, rows(h)] * dmat_ref[h]).astype(BF16), vh, preferred_element_type=F32)
                y = inner + cross_all[:, h * B_DV:(h + 1) * B_DV]
                mu = jnp.mean(y, axis=-1, keepdims=True)
                cen = y - mu
                var = jnp.mean(cen * cen, axis=-1, keepdims=True)
                yn = cen * lax.rsqrt(var + GN_EPS)
                gate = g_ref[0, rows(c), h * B_DV:(h + 1) * B_DV].astype(F32)
                o_ref[0, rows(c), h * B_DV:(h + 1) * B_DV] = (gate * jax.nn.sigmoid(gate) * yn).astype(BF16)
            state = cdf_ref[...] * state + kv_outer(k_dec, v)
        fwd_state[...] = state


def _retention_tables(logit_fwd, logit_bwd):
    lg_f = jax.nn.log_sigmoid(logit_fwd.astype(F32))
    lg_b = jax.nn.log_sigmoid(logit_bwd.astype(F32))
    idx = jnp.arange(B_CHUNK, dtype=F32)
    diff = idx[:, None] - idx[None, :]
    causal = diff >= 0
    dmat = jnp.where(causal[None],
                     jnp.exp(lg_f[:, None, None] * jnp.where(causal, diff, 0.0)[None]),
                     jnp.exp(lg_b[:, None, None] * jnp.where(causal, 0.0, -diff)[None]))

    def per_lane(lg, power):
        return jnp.repeat(jnp.exp(lg[None, :] * power[:, None]), B_DK, axis=1)

    def per_row(lg):
        return jnp.broadcast_to(jnp.repeat(jnp.exp(lg * B_CHUNK), B_DK)[:, None], (B_HEADS * B_DK, B_DV))

    return (dmat, per_lane(lg_f, idx + 1), per_lane(lg_b, B_CHUNK - idx), per_lane(lg_f, B_CHUNK - 1 - idx),
            per_lane(lg_b, idx), per_row(lg_f), per_row(lg_b))


def mixer_b(z, batch, seq, logit_fwd, logit_bwd):
    n_chunks = seq // B_CHUNK
    n_blocks = n_chunks // B_BLOCK
    block_rows = B_BLOCK * B_CHUNK
    zv = z.reshape(batch, seq, MAIN_W)
    qk_w = B_HEADS * B_DK
    v_w = B_HEADS * B_DV

    def scan_block(ph, n):
        return (1 - ph) * (n_blocks - 1 - n) + ph * n

    tables = _retention_tables(logit_fwd, logit_bwd)
    in_specs = [pl.BlockSpec((1, block_rows, qk_w), lambda b, ph, n: (b, ph * n, BQ_BLK256)),
                pl.BlockSpec((1, block_rows, qk_w), lambda b, ph, n: (b, scan_block(ph, n), BK_BLK256)),
                pl.BlockSpec((1, block_rows, v_w), lambda b, ph, n: (b, scan_block(ph, n), BV_BLK)),
                pl.BlockSpec((1, block_rows, v_w), lambda b, ph, n: (b, ph * n, BG_BLK))]
    in_specs += [_const_spec(t.shape) for t in tables]
    out = pl.pallas_call(
        functools.partial(_mixer_b_kernel, n_blocks=n_blocks),
        grid=(batch, 2, n_blocks),
        in_specs=in_specs,
        out_specs=pl.BlockSpec((1, block_rows, v_w), lambda b, ph, n: (b, ph * n, 0)),
        out_shape=jax.ShapeDtypeStruct((batch, seq, v_w), BF16),
        scratch_shapes=[pltpu.VMEM((qk_w, B_DV), F32), pltpu.VMEM((qk_w, B_DV), F32),
                        pltpu.VMEM((n_chunks, qk_w, B_DV), BF16)],
        compiler_params=_params(("arbitrary", "arbitrary", "arbitrary")),
        name="mixer_b",
    )(zv, zv, zv, zv, *tables)
    return out.reshape(batch * seq, v_w)


C_TQ = C_QR * GRID_W
C_KROWS = 3 * C_QR
C_TK = C_KROWS * GRID_W
C_PAIRS = C_KROWS // 2
C_NTAB = 2 * C_KH - 2


def _mixer_c_kernel(above_ref, cur_ref, below_ref, tab_ref, o_ref, kf, vf, *, rows):
    cur = C_STEP * C_TQ
    k_cols = slice(HEAD_BLOCK, 2 * HEAD_BLOCK)
    v_cols = slice(2 * HEAD_BLOCK, 3 * HEAD_BLOCK)
    kf[0:C_TQ, :] = above_ref[0, :, k_cols]
    kf[C_TQ:C_TQ + cur, :] = cur_ref[0, :, k_cols]
    kf[C_TQ + cur:, :] = below_ref[0, :, k_cols]
    vf[0:C_TQ, :] = above_ref[0, :, v_cols]
    vf[C_TQ:C_TQ + cur, :] = cur_ref[0, :, v_cols]
    vf[C_TQ + cur:, :] = below_ref[0, :, v_cols]

    left = lax.broadcasted_iota(jnp.int32, (1, 2 * C_HD), 1) < C_HD
    for sb in range(C_STEP):
        blk = pl.program_id(1) * C_STEP + sb
        q_rows = slice(sb * C_TQ, (sb + 1) * C_TQ)
        k_rows = slice(sb * C_TQ, sb * C_TQ + C_TK)
        key_row = blk * C_QR - C_QR + lax.broadcasted_iota(jnp.int32, (1, C_TK), 1) // GRID_W
        pens = []
        for a in range(C_QR):
            row_start = jnp.clip(blk * C_QR + a - C_KH // 2, 0, rows - C_KH)
            pens.append(jnp.where((key_row >= row_start) & (key_row < row_start + C_KH), 0.0, NEG).astype(F32))

        for hp in range(C_HEADS // 2):
            cols = slice(hp * 2 * C_HD, (hp + 1) * 2 * C_HD)
            qp = cur_ref[0, q_rows, cols]
            kp = kf[k_rows, cols]
            probs = []
            for hh, s in enumerate(_pair_scores(qp, kp, left)):
                h = hp * 2 + hh
                rows_p = []
                for a in range(C_QR):
                    bias = jnp.concatenate([tab_ref[h, 2 * t - C_QR - a + C_KH - 1] for t in range(C_PAIRS)], axis=1)
                    sa = s[a * GRID_W:(a + 1) * GRID_W, :] + bias + pens[a]
                    m = jnp.max(sa, axis=-1, keepdims=True)
                    rows_p.append(jnp.exp2(sa - m).astype(BF16))
                probs.append(jnp.concatenate(rows_p, axis=0))
            num, den = _pair_weighted_sum(probs, vf[k_rows, cols], left)
            o_ref[0, q_rows, cols] = (num / den).astype(BF16)


def _neighbourhood_bias(rpb):
    qc = jnp.arange(GRID_W)[:, None]
    kc = jnp.arange(GRID_W)[None, :]
    col_start = jnp.clip(qc - C_KW // 2, 0, GRID_W - C_KW)
    col_ok = (kc >= col_start) & (kc < col_start + C_KW)
    onehot = ((kc - qc + (C_KW - 1))[:, :, None] == jnp.arange(2 * C_KW - 1)[None, None, :]).astype(F32)
    band = jnp.einsum("hrd,qkd->hrqk", rpb.astype(F32), onehot, precision=lax.Precision.HIGHEST)
    band = jnp.where(col_ok[None, None], band * LOG2E, NEG)
    return jnp.concatenate([band[:, :C_NTAB], band[:, 1:C_NTAB + 1]], axis=-1)


def mixer_c(z, batch, seq, rpb):
    rows = seq // GRID_W
    n_blk = rows // C_QR
    zv = z.reshape(batch, seq, MAIN_W)
    tab = _neighbourhood_bias(rpb)
    halo = (1, C_TQ, A_QKV_W)
    tile = (1, C_STEP * C_TQ, A_QKV_W)
    out = pl.pallas_call(
        functools.partial(_mixer_c_kernel, rows=rows),
        grid=(batch, n_blk // C_STEP),
        in_specs=[pl.BlockSpec(halo, lambda b, i: (b, jnp.maximum(i * C_STEP - 1, 0), C_QKV_BLK)),
                  pl.BlockSpec(tile, lambda b, i: (b, i, C_QKV_BLK)),
                  pl.BlockSpec(halo, lambda b, i: (b, jnp.minimum((i + 1) * C_STEP, n_blk - 1), C_QKV_BLK)),
                  _const_spec(tab.shape)],
        out_specs=pl.BlockSpec((1, C_STEP * C_TQ, HEAD_BLOCK), lambda b, i: (b, i, 0)),
        out_shape=jax.ShapeDtypeStruct((batch, seq, HEAD_BLOCK), BF16),
        scratch_shapes=[pltpu.VMEM(((C_STEP + 2) * C_TQ, HEAD_BLOCK), BF16),
                        pltpu.VMEM(((C_STEP + 2) * C_TQ, HEAD_BLOCK), BF16)],
        compiler_params=_params(("arbitrary", "arbitrary")),
        name="mixer_c",
    )(zv, zv, zv, tab)
    return out.reshape(batch * seq, HEAD_BLOCK)


def _merge_kernel(x_ref, oa0_ref, oa1_ref, oa2_ref, la0_ref, la1_ref, la2_ref, yb_ref, yc_ref, gates_ref, spread_ref,
                  wa_ref, wb_ref, wc_ref, wo_ref, g_ref, b_ref, o_ref):
    lses = [la0_ref[...], la1_ref[...], la2_ref[...]]
    top = jnp.maximum(jnp.maximum(lses[0], lses[1]), lses[2])
    weights = [jnp.exp2(lse - top) for lse in lses]
    total = weights[0] + weights[1] + weights[2]
    ya = None
    for o_g_ref, w in zip((oa0_ref, oa1_ref, oa2_ref), weights):
        share = w / total
        hi = share.astype(BF16)
        lo = (share - hi.astype(F32)).astype(BF16)
        wide = jnp.dot(jnp.concatenate([hi, lo], axis=1), spread_ref[...], preferred_element_type=F32)
        term = wide * o_g_ref[...].astype(F32)
        ya = term if ya is None else ya + term
    ya = ya.astype(BF16)

    merged = None
    for br, (y, w_ref) in enumerate(((ya, wa_ref), (yb_ref[...], wb_ref), (yc_ref[...], wc_ref))):
        proj = jnp.dot(y, w_ref[...], preferred_element_type=F32)
        gate = jax.nn.sigmoid(gates_ref[:, br * D_MODEL:(br + 1) * D_MODEL].astype(F32))
        merged = gate * proj if merged is None else merged + gate * proj
    merged = merged.astype(BF16)
    for r0 in range(0, x_ref.shape[0], MERGE_ROWS):
        rows = slice(r0, r0 + MERGE_ROWS)
        out = jnp.dot(merged[rows], wo_ref[...], preferred_element_type=F32)
        o_ref[rows, :] = _layer_norm(ALPHA * x_ref[rows, :] + out, g_ref[...], b_ref[...])


def _lse_spread():
    src = jnp.arange(2 * A_HD)[:, None]
    head = jnp.arange(HEAD_BLOCK)[None, :] // A_HD
    first_lane = (head % 2) * A_HD + (head // 2) * LSE_LANES
    once = (src == first_lane).astype(BF16)
    return jnp.concatenate([once, once], axis=0)


def merge_out_ln(x, oa, la, yb, yc, z, wa, wb, wc, wo, g, b):
    t = x.shape[0]
    tm = MERGE_TILE_ROWS
    row = pl.BlockSpec((tm, D_MODEL), lambda i: (i, 0))
    br = pl.BlockSpec((tm, HEAD_BLOCK), lambda i: (i, 0))
    lse = pl.BlockSpec((tm, 2 * A_HD), lambda i: (i, 0))
    w_br = _const_spec((HEAD_BLOCK, D_MODEL))
    return pl.pallas_call(
        _merge_kernel,
        grid=(t // tm,),
        in_specs=[row] + [br] * 3 + [lse] * 3 + [br] * 2 + [
            pl.BlockSpec((tm, GATE_W), lambda i: (i, 0)), _const_spec((4 * A_HD, HEAD_BLOCK)), w_br, w_br, w_br,
            _const_spec((D_MODEL, D_MODEL)), _const_spec((1, D_MODEL)), _const_spec((1, D_MODEL))],
        out_specs=row,
        out_shape=jax.ShapeDtypeStruct((t, D_MODEL), F32),
        compiler_params=_params(("arbitrary",)),
        name="merge_out_ln",
    )(x, *oa, *la, yb, yc, z, _lse_spread(), wa, wb, wc, wo, g, b)


def _split_in_weights(w_in):
    a_w = A_GROUPS * HEAD_BLOCK
    aq, ak, av = w_in[:, :a_w] * (A_HD ** -0.5 * LOG2E), w_in[:, a_w:2 * a_w], w_in[:, 2 * a_w:3 * a_w]
    b_w = 2 * B_HEADS * (B_DK + B_DV)
    rest_b = w_in[:, 3 * a_w:3 * a_w + b_w]
    cq = w_in[:, 3 * a_w + b_w:3 * a_w + b_w + HEAD_BLOCK] * (C_HD ** -0.5 * LOG2E)
    rest = jnp.concatenate([rest_b, cq, w_in[:, 3 * a_w + b_w + HEAD_BLOCK:D_IN - GATE_W]], axis=1)

    def group(g):
        cols = slice(g * HEAD_BLOCK, (g + 1) * HEAD_BLOCK)
        return jnp.concatenate([aq[:, cols], ak[:, cols], av[:, cols]], axis=1)

    main = jnp.concatenate([w_in[:, D_IN - GATE_W:], group(0), rest], axis=1)
    return main.astype(BF16), [group(g).astype(BF16) for g in range(1, A_GROUPS)]


def _trunk(x, layers):
    batch, seq, _ = x.shape
    x = x.reshape(batch * seq, D_MODEL)
    for p in layers:
        dils = tuple(d for _, d in A_PATTERNS[1:])
        x1, *x1_by_residue = ffn_ln(x, p["wg1"], p["wu1"], p["wd1"], p["g1"], p["b1"], batch, dils)
        z = in_proj(x1, p["w_main"])
        oa, la = [], []
        o, l = mixer_a_group(z.reshape(batch, 1, seq, MAIN_W), 1, A_QKV_BLK)
        oa.append(o.reshape(batch * seq, HEAD_BLOCK))
        la.append(l.reshape(batch * seq, 2 * A_HD))
        for dilation, xg, w_g in zip(dils, x1_by_residue, p["w_groups"]):
            zg = in_proj(xg.reshape(batch * seq, D_MODEL), w_g)
            o, l = mixer_a_group(zg.reshape(batch, dilation, seq // dilation, A_QKV_W), dilation, 0)
            oa.append(_by_token(o, batch, seq))
            la.append(_by_token(l, batch, seq))
        yb = mixer_b(z, batch, seq, p["logit_fwd"], p["logit_bwd"])
        yc = mixer_c(z, batch, seq, p["rpb"])
        x2 = merge_out_ln(x1, oa, la, yb, yc, z, p["wa"], p["wb"], p["wc"], p["wo"], p["g2"], p["b2"])
        (x,) = ffn_ln(x2, p["wg2"], p["wu2"], p["wd2"], p["g3"], p["b3"])
    return x.reshape(batch, seq, D_MODEL)


def kernel(x_prompt, x_sample, ffn1_w_gate, ffn1_w_up, ffn1_w_down, ln1_g, ln1_b, w_in, ret_logit_fwd, ret_logit_bwd, na_rpb, w_branch_a, w_branch_b, w_branch_c, w_out, ln2_g, ln2_b, ffn2_w_gate, ffn2_w_up, ffn2_w_down, ln3_g, ln3_b):
    def vec(v):
        return v.astype(F32).reshape(1, D_MODEL)

    layers = []
    for i in range(DEPTH):
        w_main, w_groups = _split_in_weights(w_in[i])
        layers.append(dict(
            wg1=ffn1_w_gate[i].astype(BF16), wu1=ffn1_w_up[i].astype(BF16), wd1=ffn1_w_down[i].astype(BF16),
            g1=vec(ln1_g[i]), b1=vec(ln1_b[i]),
            w_main=w_main, w_groups=w_groups,
            logit_fwd=ret_logit_fwd[i], logit_bwd=ret_logit_bwd[i], rpb=na_rpb[i],
            wa=w_branch_a[i].astype(BF16), wb=w_branch_b[i].astype(BF16), wc=w_branch_c[i].astype(BF16),
            wo=w_out[i].astype(BF16), g2=vec(ln2_g[i]), b2=vec(ln2_b[i]),
            wg2=ffn2_w_gate[i].astype(BF16), wu2=ffn2_w_up[i].astype(BF16), wd2=ffn2_w_down[i].astype(BF16),
            g3=vec(ln3_g[i]), b3=vec(ln3_b[i])))
    return (_trunk(x_prompt, layers), _trunk(x_sample, layers))
```

```python
import functools

import jax
import jax.numpy as jnp
from jax import lax
from jax.experimental import pallas as pl
from jax.experimental.pallas import tpu as pltpu

F32 = jnp.float32
BF16 = jnp.bfloat16

D_MODEL = 1024
DEPTH = 2
D_FF = 2816
LN_EPS = 1e-5
GN_EPS = 1e-5
ALPHA = (2 * DEPTH) ** 0.25

A_PATTERNS = ((128, 1), (512, 4), (2048, 16))
A_GROUPS = len(A_PATTERNS)
A_HEADS = 8
A_HD = 64
A_HALF = 64
LSE_LANES = A_HD // (A_HEADS // 2)
B_HEADS = 4
B_DK = 64
B_DV = 128
B_CHUNK = 128
B_BLOCK = 32
C_HEADS = 8
C_HD = 64
GRID_W = 64
C_KH = 8
C_KW = 16
C_QR = 4
C_STEP = 8
D_IN = 10752
NEG = -1e30
LANES = 128
LOG2E = 1.4426950408889634

GATE_W = 3 * D_MODEL
HEAD_BLOCK = 512
A_QKV_W = 3 * HEAD_BLOCK
MAIN_W = D_IN - (A_GROUPS - 1) * A_QKV_W
A_QKV_BLK = 2
BQ_BLK256, BK_BLK256 = 18, 19
BV_BLK, BG_BLK = 10, 11
C_QKV_BLK = 4

VMEM_LIMIT = 56 * 1024 * 1024

FFN_ROWS = 1024
PROJ_ROWS = 2048
PROJ_COLS = 2560
MERGE_TILE_ROWS = 1024
A_TILE = 2048
FF_CHUNKS = ((0, 512), (512, 1024), (1024, 1536), (1536, 2048), (2048, 2560), (2560, 2816))
FF_NORM_ROWS = 256
MERGE_ROWS = 256


def _params(sem):
    return pltpu.CompilerParams(dimension_semantics=sem, vmem_limit_bytes=VMEM_LIMIT)


def _const_spec(shape):
    zeros = (0,) * len(shape)
    return pl.BlockSpec(shape, lambda *_: zeros)


def _layer_norm(r, g, b):
    mu = jnp.mean(r, axis=-1, keepdims=True)
    c = r - mu
    var = jnp.mean(c * c, axis=-1, keepdims=True)
    return c * lax.rsqrt(var + LN_EPS) * g + b


def _ffn_ln_kernel(x_ref, wg_ref, wu_ref, wd_ref, g_ref, b_ref, o_ref, *rest, dilations):
    x = x_ref[...]
    xb = x.astype(BF16)
    hidden = []
    for c0, c1 in FF_CHUNKS:
        gate = jnp.dot(xb, wg_ref[:, c0:c1], preferred_element_type=F32)
        up = jnp.dot(xb, wu_ref[:, c0:c1], preferred_element_type=F32)
        hidden.append((gate * jax.nn.sigmoid(gate) * up).astype(BF16))
    hidden = jnp.concatenate(hidden, axis=1)
    if rest:
        *og_refs, slabs = rest
    n_slabs = D_MODEL // LANES
    rows = FF_NORM_ROWS
    for r0 in range(0, x.shape[0], rows):
        acc = jnp.dot(hidden[r0:r0 + rows], wd_ref[...], preferred_element_type=F32)
        y = _layer_norm(ALPHA * x[r0:r0 + rows] + 0.5 * acc, g_ref[...], b_ref[...])
        o_ref[r0:r0 + rows, :] = y
        if not rest:
            continue
        done = 1
        ordered = y
        for d, og_ref in zip(dilations, og_refs):
            for s in range(n_slabs):
                slabs[s] = ordered[:, s * LANES:(s + 1) * LANES]
            step = d // done
            group = rows // done
            pieces = []
            for r in range(d):
                first = (r % done) * group + r // done
                picked = [slabs[s, pl.ds(first, rows // d, stride=step), :] for s in range(n_slabs)]
                pieces.append(jnp.concatenate(picked, axis=1))
                og_ref[0, r, r0 // d:(r0 + rows) // d, :] = pieces[-1].astype(BF16)
            ordered = jnp.concatenate(pieces, axis=0)
            done = d


def ffn_ln(x, wg, wu, wd, g, b, batch=None, dilations=()):
    t = x.shape[0]
    tm = FFN_ROWS
    row = pl.BlockSpec((tm, D_MODEL), lambda i: (i, 0))
    out_shape = [jax.ShapeDtypeStruct((t, D_MODEL), F32)]
    out_specs = [row]
    scratch = []
    if dilations:
        assert all(b % a == 0 for a, b in zip((1,) + dilations, dilations)), dilations
        tiles_per_seq = t // batch // tm
        for d in dilations:
            out_shape.append(jax.ShapeDtypeStruct((batch, d, t // batch // d, D_MODEL), BF16))
            out_specs.append(pl.BlockSpec((1, d, tm // d, D_MODEL),
                                          lambda i: (i // tiles_per_seq, 0, i % tiles_per_seq, 0)))
        scratch = [pltpu.VMEM((D_MODEL // LANES, FF_NORM_ROWS, LANES), F32)]
    return pl.pallas_call(
        functools.partial(_ffn_ln_kernel, dilations=dilations),
        grid=(t // tm,),
        in_specs=[row, _const_spec((D_MODEL, D_FF)), _const_spec((D_MODEL, D_FF)), _const_spec((D_FF, D_MODEL)),
                  _const_spec((1, D_MODEL)), _const_spec((1, D_MODEL))],
        out_specs=out_specs,
        out_shape=out_shape,
        scratch_shapes=scratch,
        compiler_params=_params(("arbitrary",)),
        name="ffn_ln",
    )(x, wg, wu, wd, g, b)


def _in_proj_kernel(x_ref, w_ref, z_ref):
    z_ref[...] = jnp.dot(x_ref[...].astype(BF16), w_ref[...], preferred_element_type=F32).astype(BF16)


def in_proj(xb, w_in):
    t = xb.shape[0]
    width = w_in.shape[1]
    tm = PROJ_ROWS
    tn = PROJ_COLS if width % PROJ_COLS == 0 else A_QKV_W
    return pl.pallas_call(
        _in_proj_kernel,
        grid=(t // tm, width // tn),
        in_specs=[pl.BlockSpec((tm, D_MODEL), lambda i, j: (i, 0)), pl.BlockSpec((D_MODEL, tn), lambda i, j: (0, j))],
        out_specs=pl.BlockSpec((tm, tn), lambda i, j: (i, j)),
        out_shape=jax.ShapeDtypeStruct((t, width), BF16),
        compiler_params=_params(("arbitrary", "arbitrary")),
        name="in_proj",
    )(xb, w_in)


def _pair_scores(q_pair, k_pair, left):
    zero = jnp.zeros_like(q_pair)
    dims = (((1,), (1,)), ((), ()))
    return [lax.dot_general(jnp.where(left, q_pair, zero), k_pair, dims, preferred_element_type=F32),
            lax.dot_general(jnp.where(left, zero, q_pair), k_pair, dims, preferred_element_type=F32)]


def _pair_weighted_sum(probs, v_pair, left):
    rhs = []
    for hh in range(2):
        sel = left if hh == 0 else jnp.logical_not(left)
        ones = jnp.broadcast_to(jnp.where(sel, 1.0, 0.0).astype(BF16), v_pair.shape)
        rhs.append(jnp.concatenate([jnp.where(sel, v_pair, jnp.zeros_like(v_pair)), ones], axis=1))
    out = jnp.dot(jnp.concatenate(probs, axis=1), jnp.concatenate(rhs, axis=0), preferred_element_type=F32)
    width = v_pair.shape[1]
    return out[:, :width], out[:, width:]


A_QB = 128
A_KB = A_QB + 2 * A_HALF


def _mixer_a_kernel(before_ref, cur_ref, after_ref, bias_ref, o_ref, l_ref, kf, vf, *, tl, seq_len):
    i = pl.program_id(2)
    k_cols = slice(HEAD_BLOCK, 2 * HEAD_BLOCK)
    v_cols = slice(2 * HEAD_BLOCK, 3 * HEAD_BLOCK)
    lane = lax.broadcasted_iota(jnp.int32, (1, 2 * A_HD), 1)
    left = lane < A_HD
    lse_slot = (lane % A_HD) // LSE_LANES

    for res in range(cur_ref.shape[1]):
        kf[res, 0:A_HALF, :] = before_ref[0, res, :, k_cols]
        kf[res, A_HALF:A_HALF + tl, :] = cur_ref[0, res, :, k_cols]
        kf[res, A_HALF + tl:, :] = after_ref[0, res, :, k_cols]
        vf[res, 0:A_HALF, :] = before_ref[0, res, :, v_cols]
        vf[res, A_HALF:A_HALF + tl, :] = cur_ref[0, res, :, v_cols]
        vf[res, A_HALF + tl:, :] = after_ref[0, res, :, v_cols]

        for j in range(tl // A_QB):
            q0 = j * A_QB
            start = i * tl + q0
            variant = (start == 0).astype(jnp.int32) + 2 * (start + A_QB == seq_len).astype(jnp.int32)
            lse_all = None
            for hp in range(A_HEADS // 2):
                cols = slice(hp * 2 * A_HD, (hp + 1) * 2 * A_HD)
                qp = cur_ref[0, res, q0:q0 + A_QB, cols]
                kp = kf[res, q0:q0 + A_KB, cols]
                probs, tops = [], []
                for hh, s in enumerate(_pair_scores(qp, kp, left)):
                    s = s + bias_ref[variant, hp * 2 + hh]
                    m = jnp.max(s, axis=-1, keepdims=True)
                    probs.append(jnp.exp2(s - m).astype(BF16))
                    tops.append(m)
                num, den = _pair_weighted_sum(probs, vf[res, q0:q0 + A_KB, cols], left)
                o_ref[0, res, q0:q0 + A_QB, cols] = (num / den).astype(BF16)
                lse_pair = jnp.where(left, tops[0], tops[1]) + jnp.log2(den)
                lse_all = lse_pair if hp == 0 else jnp.where(lse_slot == hp, lse_pair, lse_all)
            l_ref[0, res, q0:q0 + A_QB, :] = lse_all


def _alibi_bias(dilation):
    slopes = 2.0 ** (-8.0 * jnp.arange(1, A_HEADS + 1, dtype=F32) / A_HEADS)
    key = jnp.arange(A_KB)[None, :] - A_HALF
    rel = key - jnp.arange(A_QB)[:, None]
    dist = (jnp.abs(rel) * dilation).astype(F32)
    bias = -slopes[:, None, None] * dist[None] * LOG2E
    in_window = jnp.abs(rel) <= A_HALF
    variants = []
    for v in range(4):
        ok = in_window
        if v & 1:
            ok = ok & (key >= 0)
        if v & 2:
            ok = ok & (key < A_QB)
        variants.append(jnp.where(ok[None], bias, NEG))
    return jnp.stack(variants, 0)


def mixer_a_group(zg, dilation, qkv_blk):
    batch, _, strided_len, _ = zg.shape
    tl = min(A_TILE, strided_len)
    n_res = min(A_TILE // tl, dilation)
    halo_per_tile = tl // A_HALF
    n_halo = strided_len // A_HALF
    out_dims = (batch, dilation, strided_len, HEAD_BLOCK)
    return pl.pallas_call(
        functools.partial(_mixer_a_kernel, tl=tl, seq_len=strided_len),
        grid=(batch, dilation // n_res, strided_len // tl),
        in_specs=[pl.BlockSpec((1, n_res, A_HALF, A_QKV_W),
                               lambda b, r, i: (b, r, jnp.maximum(i * halo_per_tile - 1, 0), qkv_blk)),
                  pl.BlockSpec((1, n_res, tl, A_QKV_W), lambda b, r, i: (b, r, i, qkv_blk)),
                  pl.BlockSpec((1, n_res, A_HALF, A_QKV_W),
                               lambda b, r, i: (b, r, jnp.minimum((i + 1) * halo_per_tile, n_halo - 1), qkv_blk)),
                  _const_spec((4, A_HEADS, A_QB, A_KB))],
        out_specs=[pl.BlockSpec((1, n_res, tl, HEAD_BLOCK), lambda b, r, i: (b, r, i, 0)),
                   pl.BlockSpec((1, n_res, tl, 2 * A_HD), lambda b, r, i: (b, r, i, 0))],
        out_shape=[jax.ShapeDtypeStruct(out_dims, BF16),
                   jax.ShapeDtypeStruct((batch, dilation, strided_len, 2 * A_HD), F32)],
        scratch_shapes=[pltpu.VMEM((n_res, tl + 2 * A_HALF, HEAD_BLOCK), BF16),
                        pltpu.VMEM((n_res, tl + 2 * A_HALF, HEAD_BLOCK), BF16)],
        compiler_params=_params(("arbitrary", "arbitrary", "arbitrary")),
        name=f"mixer_a_d{dilation}",
    )(zg, zg, zg, _alibi_bias(dilation))


def _by_token(x, batch, seq):
    return x.transpose(0, 2, 1, 3).reshape(batch * seq, x.shape[-1])


def _mixer_b_kernel(q_ref, k_ref, v_ref, g_ref, dmat_ref, qdf_ref, qdb_ref, kdf_ref, kdb_ref, cdf_ref, cdb_ref,
                    o_ref, fwd_state, bwd_state, bwd_store, *, n_blocks):
    phase = pl.program_id(1)
    n = pl.program_id(2)
    qk_w = B_HEADS * B_DK
    head_of_lane = lax.broadcasted_iota(jnp.int32, (1, qk_w), 1) // B_DK

    def rows(c):
        return slice(c * B_CHUNK, (c + 1) * B_CHUNK)

    def stacked_heads(t):
        lane_head = jnp.concatenate([head_of_lane] * (t.shape[1] // qk_w), axis=1)
        return jnp.concatenate([jnp.where(lane_head == h, t, jnp.zeros_like(t)) for h in range(B_HEADS)], axis=0)

    def head_block_diagonal(states):
        row_head = (lax.broadcasted_iota(jnp.int32, (states.shape[0], 1), 0) % qk_w) // B_DK
        return jnp.concatenate([jnp.where(row_head == h, states, jnp.zeros_like(states)) for h in range(B_HEADS)],
                               axis=1)

    def kv_outer(k_decayed, v):
        full = lax.dot_general(k_decayed, v, (((0,), (0,)), ((), ())), preferred_element_type=F32)
        return jnp.concatenate([full[h * B_DK:(h + 1) * B_DK, h * B_DV:(h + 1) * B_DV] for h in range(B_HEADS)], axis=0)

    def scaled_k(c):
        return k_ref[0, rows(c), :] * (B_DK ** -0.5)

    @pl.when(phase == 0)
    def _():
        @pl.when(n == 0)
        def _():
            bwd_state[...] = jnp.zeros_like(bwd_state)

        blk = n_blocks - 1 - n
        state = bwd_state[...]
        for c in reversed(range(B_BLOCK)):
            bwd_store[blk * B_BLOCK + c] = state.astype(BF16)
            k_dec = (scaled_k(c).astype(F32) * kdb_ref[...]).astype(BF16)
            state = cdb_ref[...] * state + kv_outer(k_dec, v_ref[0, rows(c), :])
        bwd_state[...] = state

    @pl.when(phase == 1)
    def _():
        @pl.when(n == 0)
        def _():
            fwd_state[...] = jnp.zeros_like(fwd_state)

        state = fwd_state[...]
        for c in range(B_BLOCK):
            q = q_ref[0, rows(c), :]
            k = scaled_k(c)
            v = v_ref[0, rows(c), :]
            q32 = q.astype(F32)
            q_dec = jnp.concatenate([(q32 * qdf_ref[...]).astype(BF16), (q32 * qdb_ref[...]).astype(BF16)], axis=1)
            k_dec = (k.astype(F32) * kdf_ref[...]).astype(BF16)
            states = jnp.concatenate([state.astype(BF16), bwd_store[n * B_BLOCK + c]], axis=0)
            s_all = lax.dot_general(q, stacked_heads(k), (((1,), (1,)), ((), ())), preferred_element_type=F32)
            cross_all = jnp.dot(q_dec, head_block_diagonal(states), preferred_element_type=F32)
            for h in range(B_HEADS):
                vh = v[:, h * B_DV:(h + 1) * B_DV]
                inner = jnp.dot((s_all[:, rows(h)] * dmat_ref[h]).astype(BF16), vh, preferred_element_type=F32)
                y = inner + cross_all[:, h * B_DV:(h + 1) * B_DV]
                mu = jnp.mean(y, axis=-1, keepdims=True)
                cen = y - mu
                var = jnp.mean(cen * cen, axis=-1, keepdims=True)
                yn = cen * lax.rsqrt(var + GN_EPS)
                gate = g_ref[0, rows(c), h * B_DV:(h + 1) * B_DV].astype(F32)
                o_ref[0, rows(c), h * B_DV:(h + 1) * B_DV] = (gate * jax.nn.sigmoid(gate) * yn).astype(BF16)
            state = cdf_ref[...] * state + kv_outer(k_dec, v)
        fwd_state[...] = state


def _retention_tables(logit_fwd, logit_bwd):
    lg_f = jax.nn.log_sigmoid(logit_fwd.astype(F32))
    lg_b = jax.nn.log_sigmoid(logit_bwd.astype(F32))
    idx = jnp.arange(B_CHUNK, dtype=F32)
    diff = idx[:, None] - idx[None, :]
    causal = diff >= 0
    dmat = jnp.where(causal[None],
                     jnp.exp(lg_f[:, None, None] * jnp.where(causal, diff, 0.0)[None]),
                     jnp.exp(lg_b[:, None, None] * jnp.where(causal, 0.0, -diff)[None]))

    def per_lane(lg, power):
        return jnp.repeat(jnp.exp(lg[None, :] * power[:, None]), B_DK, axis=1)

    def per_row(lg):
        return jnp.broadcast_to(jnp.repeat(jnp.exp(lg * B_CHUNK), B_DK)[:, None], (B_HEADS * B_DK, B_DV))

    return (dmat, per_lane(lg_f, idx + 1), per_lane(lg_b, B_CHUNK - idx), per_lane(lg_f, B_CHUNK - 1 - idx),
            per_lane(lg_b, idx), per_row(lg_f), per_row(lg_b))


def mixer_b(z, batch, seq, logit_fwd, logit_bwd):
    n_chunks = seq // B_CHUNK
    n_blocks = n_chunks // B_BLOCK
    block_rows = B_BLOCK * B_CHUNK
    zv = z.reshape(batch, seq, MAIN_W)
    qk_w = B_HEADS * B_DK
    v_w = B_HEADS * B_DV

    def scan_block(ph, n):
        return (1 - ph) * (n_blocks - 1 - n) + ph * n

    tables = _retention_tables(logit_fwd, logit_bwd)
    in_specs = [pl.BlockSpec((1, block_rows, qk_w), lambda b, ph, n: (b, ph * n, BQ_BLK256)),
                pl.BlockSpec((1, block_rows, qk_w), lambda b, ph, n: (b, scan_block(ph, n), BK_BLK256)),
                pl.BlockSpec((1, block_rows, v_w), lambda b, ph, n: (b, scan_block(ph, n), BV_BLK)),
                pl.BlockSpec((1, block_rows, v_w), lambda b, ph, n: (b, ph * n, BG_BLK))]
    in_specs += [_const_spec(t.shape) for t in tables]
    out = pl.pallas_call(
        functools.partial(_mixer_b_kernel, n_blocks=n_blocks),
        grid=(batch, 2, n_blocks),
        in_specs=in_specs,
        out_specs=pl.BlockSpec((1, block_rows, v_w), lambda b, ph, n: (b, ph * n, 0)),
        out_shape=jax.ShapeDtypeStruct((batch, seq, v_w), BF16),
        scratch_shapes=[pltpu.VMEM((qk_w, B_DV), F32), pltpu.VMEM((qk_w, B_DV), F32),
                        pltpu.VMEM((n_chunks, qk_w, B_DV), BF16)],
        compiler_params=_params(("arbitrary", "arbitrary", "arbitrary")),
        name="mixer_b",
    )(zv, zv, zv, zv, *tables)
    return out.reshape(batch * seq, v_w)


C_TQ = C_QR * GRID_W
C_KROWS = 3 * C_QR
C_TK = C_KROWS * GRID_W
C_PAIRS = C_KROWS // 2
C_NTAB = 2 * C_KH - 2


def _mixer_c_kernel(above_ref, cur_ref, below_ref, tab_ref, o_ref, kf, vf, *, rows):
    cur = C_STEP * C_TQ
    k_cols = slice(HEAD_BLOCK, 2 * HEAD_BLOCK)
    v_cols = slice(2 * HEAD_BLOCK, 3 * HEAD_BLOCK)
    kf[0:C_TQ, :] = above_ref[0, :, k_cols]
    kf[C_TQ:C_TQ + cur, :] = cur_ref[0, :, k_cols]
    kf[C_TQ + cur:, :] = below_ref[0, :, k_cols]
    vf[0:C_TQ, :] = above_ref[0, :, v_cols]
    vf[C_TQ:C_TQ + cur, :] = cur_ref[0, :, v_cols]
    vf[C_TQ + cur:, :] = below_ref[0, :, v_cols]

    left = lax.broadcasted_iota(jnp.int32, (1, 2 * C_HD), 1) < C_HD
    for sb in range(C_STEP):
        blk = pl.program_id(1) * C_STEP + sb
        q_rows = slice(sb * C_TQ, (sb + 1) * C_TQ)
        k_rows = slice(sb * C_TQ, sb * C_TQ + C_TK)
        key_row = blk * C_QR - C_QR + lax.broadcasted_iota(jnp.int32, (1, C_TK), 1) // GRID_W
        pens = []
        for a in range(C_QR):
            row_start = jnp.clip(blk * C_QR + a - C_KH // 2, 0, rows - C_KH)
            pens.append(jnp.where((key_row >= row_start) & (key_row < row_start + C_KH), 0.0, NEG).astype(F32))

        for hp in range(C_HEADS // 2):
            cols = slice(hp * 2 * C_HD, (hp + 1) * 2 * C_HD)
            qp = cur_ref[0, q_rows, cols]
            kp = kf[k_rows, cols]
            probs = []
            for hh, s in enumerate(_pair_scores(qp, kp, left)):
                h = hp * 2 + hh
                rows_p = []
                for a in range(C_QR):
                    bias = jnp.concatenate([tab_ref[h, 2 * t - C_QR - a + C_KH - 1] for t in range(C_PAIRS)], axis=1)
                    sa = s[a * GRID_W:(a + 1) * GRID_W, :] + bias + pens[a]
                    m = jnp.max(sa, axis=-1, keepdims=True)
                    rows_p.append(jnp.exp2(sa - m).astype(BF16))
                probs.append(jnp.concatenate(rows_p, axis=0))
            num, den = _pair_weighted_sum(probs, vf[k_rows, cols], left)
            o_ref[0, q_rows, cols] = (num / den).astype(BF16)


def _neighbourhood_bias(rpb):
    qc = jnp.arange(GRID_W)[:, None]
    kc = jnp.arange(GRID_W)[None, :]
    col_start = jnp.clip(qc - C_KW // 2, 0, GRID_W - C_KW)
    col_ok = (kc >= col_start) & (kc < col_start + C_KW)
    onehot = ((kc - qc + (C_KW - 1))[:, :, None] == jnp.arange(2 * C_KW - 1)[None, None, :]).astype(F32)
    band = jnp.einsum("hrd,qkd->hrqk", rpb.astype(F32), onehot, precision=lax.Precision.HIGHEST)
    band = jnp.where(col_ok[None, None], band * LOG2E, NEG)
    return jnp.concatenate([band[:, :C_NTAB], band[:, 1:C_NTAB + 1]], axis=-1)


def mixer_c(z, batch, seq, rpb):
    rows = seq // GRID_W
    n_blk = rows // C_QR
    zv = z.reshape(batch, seq, MAIN_W)
    tab = _neighbourhood_bias(rpb)
    halo = (1, C_TQ, A_QKV_W)
    tile = (1, C_STEP * C_TQ, A_QKV_W)
    out = pl.pallas_call(
        functools.partial(_mixer_c_kernel, rows=rows),
        grid=(batch, n_blk // C_STEP),
        in_specs=[pl.BlockSpec(halo, lambda b, i: (b, jnp.maximum(i * C_STEP - 1, 0), C_QKV_BLK)),
                  pl.BlockSpec(tile, lambda b, i: (b, i, C_QKV_BLK)),
                  pl.BlockSpec(halo, lambda b, i: (b, jnp.minimum((i + 1) * C_STEP, n_blk - 1), C_QKV_BLK)),
                  _const_spec(tab.shape)],
        out_specs=pl.BlockSpec((1, C_STEP * C_TQ, HEAD_BLOCK), lambda b, i: (b, i, 0)),
        out_shape=jax.ShapeDtypeStruct((batch, seq, HEAD_BLOCK), BF16),
        scratch_shapes=[pltpu.VMEM(((C_STEP + 2) * C_TQ, HEAD_BLOCK), BF16),
                        pltpu.VMEM(((C_STEP + 2) * C_TQ, HEAD_BLOCK), BF16)],
        compiler_params=_params(("arbitrary", "arbitrary")),
        name="mixer_c",
    )(zv, zv, zv, tab)
    return out.reshape(batch * seq, HEAD_BLOCK)


def _merge_kernel(x_ref, oa0_ref, oa1_ref, oa2_ref, la0_ref, la1_ref, la2_ref, yb_ref, yc_ref, gates_ref, spread_ref,
                  wa_ref, wb_ref, wc_ref, wo_ref, g_ref, b_ref, o_ref):
    lses = [la0_ref[...], la1_ref[...], la2_ref[...]]
    top = jnp.maximum(jnp.maximum(lses[0], lses[1]), lses[2])
    weights = [jnp.exp2(lse - top) for lse in lses]
    total = weights[0] + weights[1] + weights[2]
    ya = None
    for o_g_ref, w in zip((oa0_ref, oa1_ref, oa2_ref), weights):
        share = w / total
        hi = share.astype(BF16)
        lo = (share - hi.astype(F32)).astype(BF16)
        wide = jnp.dot(jnp.concatenate([hi, lo], axis=1), spread_ref[...], preferred_element_type=F32)
        term = wide * o_g_ref[...].astype(F32)
        ya = term if ya is None else ya + term
    ya = ya.astype(BF16)

    merged = None
    for br, (y, w_ref) in enumerate(((ya, wa_ref), (yb_ref[...], wb_ref), (yc_ref[...], wc_ref))):
        proj = jnp.dot(y, w_ref[...], preferred_element_type=F32)
        gate = jax.nn.sigmoid(gates_ref[:, br * D_MODEL:(br + 1) * D_MODEL].astype(F32))
        merged = gate * proj if merged is None else merged + gate * proj
    merged = merged.astype(BF16)
    for r0 in range(0, x_ref.shape[0], MERGE_ROWS):
        rows = slice(r0, r0 + MERGE_ROWS)
        out = jnp.dot(merged[rows], wo_ref[...], preferred_element_type=F32)
        o_ref[rows, :] = _layer_norm(ALPHA * x_ref[rows, :] + out, g_ref[...], b_ref[...])


def _lse_spread():
    src = jnp.arange(2 * A_HD)[:, None]
    head = jnp.arange(HEAD_BLOCK)[None, :] // A_HD
    first_lane = (head % 2) * A_HD + (head // 2) * LSE_LANES
    once = (src == first_lane).astype(BF16)
    return jnp.concatenate([once, once], axis=0)


def merge_out_ln(x, oa, la, yb, yc, z, wa, wb, wc, wo, g, b):
    t = x.shape[0]
    tm = MERGE_TILE_ROWS
    row = pl.BlockSpec((tm, D_MODEL), lambda i: (i, 0))
    br = pl.BlockSpec((tm, HEAD_BLOCK), lambda i: (i, 0))
    lse = pl.BlockSpec((tm, 2 * A_HD), lambda i: (i, 0))
    w_br = _const_spec((HEAD_BLOCK, D_MODEL))
    return pl.pallas_call(
        _merge_kernel,
        grid=(t // tm,),
        in_specs=[row] + [br] * 3 + [lse] * 3 + [br] * 2 + [
            pl.BlockSpec((tm, GATE_W), lambda i: (i, 0)), _const_spec((4 * A_HD, HEAD_BLOCK)), w_br, w_br, w_br,
            _const_spec((D_MODEL, D_MODEL)), _const_spec((1, D_MODEL)), _const_spec((1, D_MODEL))],
        out_specs=row,
        out_shape=jax.ShapeDtypeStruct((t, D_MODEL), F32),
        compiler_params=_params(("arbitrary",)),
        name="merge_out_ln",
    )(x, *oa, *la, yb, yc, z, _lse_spread(), wa, wb, wc, wo, g, b)


def _split_in_weights(w_in):
    a_w = A_GROUPS * HEAD_BLOCK
    aq, ak, av = w_in[:, :a_w] * (A_HD ** -0.5 * LOG2E), w_in[:, a_w:2 * a_w], w_in[:, 2 * a_w:3 * a_w]
    b_w = 2 * B_HEADS * (B_DK + B_DV)
    rest_b = w_in[:, 3 * a_w:3 * a_w + b_w]
    cq = w_in[:, 3 * a_w + b_w:3 * a_w + b_w + HEAD_BLOCK] * (C_HD ** -0.5 * LOG2E)
    rest = jnp.concatenate([rest_b, cq, w_in[:, 3 * a_w + b_w + HEAD_BLOCK:D_IN - GATE_W]], axis=1)

    def group(g):
        cols = slice(g * HEAD_BLOCK, (g + 1) * HEAD_BLOCK)
        return jnp.concatenate([aq[:, cols], ak[:, cols], av[:, cols]], axis=1)

    main = jnp.concatenate([w_in[:, D_IN - GATE_W:], group(0), rest], axis=1)
    return main.astype(BF16), [group(g).astype(BF16) for g in range(1, A_GROUPS)]


def _trunk(x, layers):
    batch, seq, _ = x.shape
    x = x.reshape(batch * seq, D_MODEL)
    for p in layers:
        dils = tuple(d for _, d in A_PATTERNS[1:])
        x1, *x1_by_residue = ffn_ln(x, p["wg1"], p["wu1"], p["wd1"], p["g1"], p["b1"], batch, dils)
        z = in_proj(x1, p["w_main"])
        oa, la = [], []
        o, l = mixer_a_group(z.reshape(batch, 1, seq, MAIN_W), 1, A_QKV_BLK)
        oa.append(o.reshape(batch * seq, HEAD_BLOCK))
        la.append(l.reshape(batch * seq, 2 * A_HD))
        for dilation, xg, w_g in zip(dils, x1_by_residue, p["w_groups"]):
            zg = in_proj(xg.reshape(batch * seq, D_MODEL), w_g)
            o, l = mixer_a_group(zg.reshape(batch, dilation, seq // dilation, A_QKV_W), dilation, 0)
            oa.append(_by_token(o, batch, seq))
            la.append(_by_token(l, batch, seq))
        yb = mixer_b(z, batch, seq, p["logit_fwd"], p["logit_bwd"])
        yc = mixer_c(z, batch, seq, p["rpb"])
        x2 = merge_out_ln(x1, oa, la, yb, yc, z, p["wa"], p["wb"], p["wc"], p["wo"], p["g2"], p["b2"])
        (x,) = ffn_ln(x2, p["wg2"], p["wu2"], p["wd2"], p["g3"], p["b3"])
    return x.reshape(batch, seq, D_MODEL)


def kernel(x_prompt, x_sample, ffn1_w_gate, ffn1_w_up, ffn1_w_down, ln1_g, ln1_b, w_in, ret_logit_fwd, ret_logit_bwd, na_rpb, w_branch_a, w_branch_b, w_branch_c, w_out, ln2_g, ln2_b, ffn2_w_gate, ffn2_w_up, ffn2_w_down, ln3_g, ln3_b):
    def vec(v):
        return v.astype(F32).reshape(1, D_MODEL)

    layers = []
    for i in range(DEPTH):
        w_main, w_groups = _split_in_weights(w_in[i])
        layers.append(dict(
            wg1=ffn1_w_gate[i].astype(BF16), wu1=ffn1_w_up[i].astype(BF16), wd1=ffn1_w_down[i].astype(BF16),
            g1=vec(ln1_g[i]), b1=vec(ln1_b[i]),
            w_main=w_main, w_groups=w_groups,
            logit_fwd=ret_logit_fwd[i], logit_bwd=ret_logit_bwd[i], rpb=na_rpb[i],
            wa=w_branch_a[i].astype(BF16), wb=w_branch_b[i].astype(BF16), wc=w_branch_c[i].astype(BF16),
            wo=w_out[i].astype(BF16), g2=vec(ln2_g[i]), b2=vec(ln2_b[i]),
            wg2=ffn2_w_gate[i].astype(BF16), wu2=ffn2_w_up[i].astype(BF16), wd2=ffn2_w_down[i].astype(BF16),
            g3=vec(ln3_g[i]), b3=vec(ln3_b[i])))
    return (_trunk(x_prompt, layers), _trunk(x_sample, layers))
```

```python
import functools

import jax
import jax.numpy as jnp
from jax import lax
from jax.experimental import pallas as pl
from jax.experimental.pallas import tpu as pltpu

F32 = jnp.float32
BF16 = jnp.bfloat16

D_MODEL = 1024
DEPTH = 2
D_FF = 2816
LN_EPS = 1e-5
GN_EPS = 1e-5
ALPHA = (2 * DEPTH) ** 0.25

A_PATTERNS = ((128, 1), (512, 4), (2048, 16))
A_GROUPS = len(A_PATTERNS)
A_HEADS = 8
A_HD = 64
A_HALF = 64
LSE_LANES = A_HD // (A_HEADS // 2)
B_HEADS = 4
B_DK = 64
B_DV = 128
B_CHUNK = 128
B_BLOCK = 16
C_HEADS = 8
C_HD = 64
GRID_W = 64
C_KH = 8
C_KW = 16
C_QR = 4
C_STEP = 8
D_IN = 10752
NEG = -1e30
LANES = 128
LOG2E = 1.4426950408889634

GATE_W = 3 * D_MODEL
HEAD_BLOCK = 512
A_QKV_W = 3 * HEAD_BLOCK
MAIN_W = D_IN - (A_GROUPS - 1) * A_QKV_W
A_QKV_BLK = 2
BQ_BLK256, BK_BLK256 = 18, 19
BV_BLK, BG_BLK = 10, 11
C_QKV_BLK = 4

VMEM_LIMIT = 56 * 1024 * 1024

FFN_ROWS = 1024
PROJ_ROWS = 2048
PROJ_COLS = 2560
MERGE_TILE_ROWS = 1024
A_TILE = 2048
FF_CHUNKS = ((0, 512), (512, 1024), (1024, 1536), (1536, 2048), (2048, 2560), (2560, 2816))
FF_NORM_ROWS = 256
MERGE_ROWS = 256


def _params(sem):
    return pltpu.CompilerParams(dimension_semantics=sem, vmem_limit_bytes=VMEM_LIMIT)


def _const_spec(shape):
    zeros = (0,) * len(shape)
    return pl.BlockSpec(shape, lambda *_: zeros)


def _layer_norm(r, g, b):
    mu = jnp.mean(r, axis=-1, keepdims=True)
    c = r - mu
    var = jnp.mean(c * c, axis=-1, keepdims=True)
    return c * lax.rsqrt(var + LN_EPS) * g + b


def _ffn_ln_kernel(x_ref, wg_ref, wu_ref, wd_ref, g_ref, b_ref, o_ref, *rest, dilations):
    x = x_ref[...]
    xb = x.astype(BF16)
    hidden = []
    for c0, c1 in FF_CHUNKS:
        gate = jnp.dot(xb, wg_ref[:, c0:c1], preferred_element_type=F32)
        up = jnp.dot(xb, wu_ref[:, c0:c1], preferred_element_type=F32)
        hidden.append((gate * jax.nn.sigmoid(gate) * up).astype(BF16))
    hidden = jnp.concatenate(hidden, axis=1)
    if rest:
        *og_refs, slabs = rest
    n_slabs = D_MODEL // LANES
    rows = FF_NORM_ROWS
    for r0 in range(0, x.shape[0], rows):
        acc = jnp.dot(hidden[r0:r0 + rows], wd_ref[...], preferred_element_type=F32)
        y = _layer_norm(ALPHA * x[r0:r0 + rows] + 0.5 * acc, g_ref[...], b_ref[...])
        o_ref[r0:r0 + rows, :] = y
        if not rest:
            continue
        done = 1
        ordered = y
        for d, og_ref in zip(dilations, og_refs):
            for s in range(n_slabs):
                slabs[s] = ordered[:, s * LANES:(s + 1) * LANES]
            step = d // done
            group = rows // done
            pieces = []
            for r in range(d):
                first = (r % done) * group + r // done
                picked = [slabs[s, pl.ds(first, rows // d, stride=step), :] for s in range(n_slabs)]
                pieces.append(jnp.concatenate(picked, axis=1))
                og_ref[0, r, r0 // d:(r0 + rows) // d, :] = pieces[-1].astype(BF16)
            ordered = jnp.concatenate(pieces, axis=0)
            done = d


def ffn_ln(x, wg, wu, wd, g, b, batch=None, dilations=()):
    t = x.shape[0]
    tm = FFN_ROWS
    row = pl.BlockSpec((tm, D_MODEL), lambda i: (i, 0))
    out_shape = [jax.ShapeDtypeStruct((t, D_MODEL), F32)]
    out_specs = [row]
    scratch = []
    if dilations:
        assert all(b % a == 0 for a, b in zip((1,) + dilations, dilations)), dilations
        tiles_per_seq = t // batch // tm
        for d in dilations:
            out_shape.append(jax.ShapeDtypeStruct((batch, d, t // batch // d, D_MODEL), BF16))
            out_specs.append(pl.BlockSpec((1, d, tm // d, D_MODEL),
                                          lambda i: (i // tiles_per_seq, 0, i % tiles_per_seq, 0)))
        scratch = [pltpu.VMEM((D_MODEL // LANES, FF_NORM_ROWS, LANES), F32)]
    return pl.pallas_call(
        functools.partial(_ffn_ln_kernel, dilations=dilations),
        grid=(t // tm,),
        in_specs=[row, _const_spec((D_MODEL, D_FF)), _const_spec((D_MODEL, D_FF)), _const_spec((D_FF, D_MODEL)),
                  _const_spec((1, D_MODEL)), _const_spec((1, D_MODEL))],
        out_specs=out_specs,
        out_shape=out_shape,
        scratch_shapes=scratch,
        compiler_params=_params(("arbitrary",)),
        name="ffn_ln",
    )(x, wg, wu, wd, g, b)


def _in_proj_kernel(x_ref, w_ref, z_ref, *, gate_cols):
    tn = z_ref.shape[1]

    def tile(n_gate):
        acc = jnp.dot(x_ref[...].astype(BF16), w_ref[...], preferred_element_type=F32)
        if n_gate == tn:
            acc = jax.nn.sigmoid(acc)
        elif n_gate:
            acc = jnp.concatenate([jax.nn.sigmoid(acc[:, :n_gate]), acc[:, n_gate:]], axis=1)
        z_ref[...] = acc.astype(BF16)

    if not gate_cols:
        tile(0)
        return
    j = pl.program_id(1)
    gate_tiles = -(-gate_cols // tn)
    for jj in range(gate_tiles):
        pl.when(j == jj)(functools.partial(tile, min(tn, gate_cols - jj * tn)))
    pl.when(j >= gate_tiles)(functools.partial(tile, 0))


def in_proj(xb, w_in, gate_cols=0):
    t = xb.shape[0]
    width = w_in.shape[1]
    tm = PROJ_ROWS
    tn = PROJ_COLS if width % PROJ_COLS == 0 else A_QKV_W
    return pl.pallas_call(
        functools.partial(_in_proj_kernel, gate_cols=gate_cols),
        grid=(t // tm, width // tn),
        in_specs=[pl.BlockSpec((tm, D_MODEL), lambda i, j: (i, 0)), pl.BlockSpec((D_MODEL, tn), lambda i, j: (0, j))],
        out_specs=pl.BlockSpec((tm, tn), lambda i, j: (i, j)),
        out_shape=jax.ShapeDtypeStruct((t, width), BF16),
        compiler_params=_params(("arbitrary", "arbitrary")),
        name="in_proj",
    )(xb, w_in)


def _pair_scores(q_pair, k_pair, left):
    zero = jnp.zeros_like(q_pair)
    dims = (((1,), (1,)), ((), ()))
    return [lax.dot_general(jnp.where(left, q_pair, zero), k_pair, dims, preferred_element_type=F32),
            lax.dot_general(jnp.where(left, zero, q_pair), k_pair, dims, preferred_element_type=F32)]


def _pair_weighted_sum(probs, v_pair, left):
    rhs = []
    for hh in range(2):
        sel = left if hh == 0 else jnp.logical_not(left)
        ones = jnp.broadcast_to(jnp.where(sel, 1.0, 0.0).astype(BF16), v_pair.shape)
        rhs.append(jnp.concatenate([jnp.where(sel, v_pair, jnp.zeros_like(v_pair)), ones], axis=1))
    out = jnp.dot(jnp.concatenate(probs, axis=1), jnp.concatenate(rhs, axis=0), preferred_element_type=F32)
    width = v_pair.shape[1]
    return out[:, :width], out[:, width:]


A_QB = 128
A_KB = A_QB + 2 * A_HALF


def _mixer_a_kernel(before_ref, cur_ref, after_ref, bias_ref, o_ref, l_ref, kf, vf, *, tl, seq_len):
    i = pl.program_id(2)
    k_cols = slice(HEAD_BLOCK, 2 * HEAD_BLOCK)
    v_cols = slice(2 * HEAD_BLOCK, 3 * HEAD_BLOCK)
    lane = lax.broadcasted_iota(jnp.int32, (1, 2 * A_HD), 1)
    left = lane < A_HD
    lse_slot = (lane % A_HD) // LSE_LANES

    for res in range(cur_ref.shape[1]):
        kf[res, 0:A_HALF, :] = before_ref[0, res, :, k_cols]
        kf[res, A_HALF:A_HALF + tl, :] = cur_ref[0, res, :, k_cols]
        kf[res, A_HALF + tl:, :] = after_ref[0, res, :, k_cols]
        vf[res, 0:A_HALF, :] = before_ref[0, res, :, v_cols]
        vf[res, A_HALF:A_HALF + tl, :] = cur_ref[0, res, :, v_cols]
        vf[res, A_HALF + tl:, :] = after_ref[0, res, :, v_cols]

        for j in range(tl // A_QB):
            q0 = j * A_QB
            start = i * tl + q0
            variant = (start == 0).astype(jnp.int32) + 2 * (start + A_QB == seq_len).astype(jnp.int32)
            lse_all = None
            for hp in range(A_HEADS // 2):
                cols = slice(hp * 2 * A_HD, (hp + 1) * 2 * A_HD)
                qp = cur_ref[0, res, q0:q0 + A_QB, cols]
                kp = kf[res, q0:q0 + A_KB, cols]
                probs, tops = [], []
                for hh, s in enumerate(_pair_scores(qp, kp, left)):
                    s = s + bias_ref[variant, hp * 2 + hh]
                    m = jnp.max(s, axis=-1, keepdims=True)
                    probs.append(jnp.exp2(s - m).astype(BF16))
                    tops.append(m)
                num, den = _pair_weighted_sum(probs, vf[res, q0:q0 + A_KB, cols], left)
                o_ref[0, res, q0:q0 + A_QB, cols] = (num / den).astype(BF16)
                lse_pair = jnp.where(left, tops[0], tops[1]) + jnp.log2(den)
                lse_all = lse_pair if hp == 0 else jnp.where(lse_slot == hp, lse_pair, lse_all)
            l_ref[0, res, q0:q0 + A_QB, :] = lse_all


def _alibi_bias(dilation):
    slopes = 2.0 ** (-8.0 * jnp.arange(1, A_HEADS + 1, dtype=F32) / A_HEADS)
    key = jnp.arange(A_KB)[None, :] - A_HALF
    rel = key - jnp.arange(A_QB)[:, None]
    dist = (jnp.abs(rel) * dilation).astype(F32)
    bias = -slopes[:, None, None] * dist[None] * LOG2E
    in_window = jnp.abs(rel) <= A_HALF
    variants = []
    for v in range(4):
        ok = in_window
        if v & 1:
            ok = ok & (key >= 0)
        if v & 2:
            ok = ok & (key < A_QB)
        variants.append(jnp.where(ok[None], bias, NEG))
    return jnp.stack(variants, 0)


def mixer_a_group(zg, dilation, qkv_blk):
    batch, _, strided_len, _ = zg.shape
    tl = min(A_TILE, strided_len)
    n_res = min(A_TILE // tl, dilation)
    halo_per_tile = tl // A_HALF
    n_halo = strided_len // A_HALF
    out_dims = (batch, dilation, strided_len, HEAD_BLOCK)
    return pl.pallas_call(
        functools.partial(_mixer_a_kernel, tl=tl, seq_len=strided_len),
        grid=(batch, dilation // n_res, strided_len // tl),
        in_specs=[pl.BlockSpec((1, n_res, A_HALF, A_QKV_W),
                               lambda b, r, i: (b, r, jnp.maximum(i * halo_per_tile - 1, 0), qkv_blk)),
                  pl.BlockSpec((1, n_res, tl, A_QKV_W), lambda b, r, i: (b, r, i, qkv_blk)),
                  pl.BlockSpec((1, n_res, A_HALF, A_QKV_W),
                               lambda b, r, i: (b, r, jnp.minimum((i + 1) * halo_per_tile, n_halo - 1), qkv_blk)),
                  _const_spec((4, A_HEADS, A_QB, A_KB))],
        out_specs=[pl.BlockSpec((1, n_res, tl, HEAD_BLOCK), lambda b, r, i: (b, r, i, 0)),
                   pl.BlockSpec((1, n_res, tl, 2 * A_HD), lambda b, r, i: (b, r, i, 0))],
        out_shape=[jax.ShapeDtypeStruct(out_dims, BF16),
                   jax.ShapeDtypeStruct((batch, dilation, strided_len, 2 * A_HD), F32)],
        scratch_shapes=[pltpu.VMEM((n_res, tl + 2 * A_HALF, HEAD_BLOCK), BF16),
                        pltpu.VMEM((n_res, tl + 2 * A_HALF, HEAD_BLOCK), BF16)],
        compiler_params=_params(("arbitrary", "arbitrary", "arbitrary")),
        name=f"mixer_a_d{dilation}",
    )(zg, zg, zg, _alibi_bias(dilation))


def _by_token(x, batch, seq):
    return x.transpose(0, 2, 1, 3).reshape(batch * seq, x.shape[-1])


def _mixer_b_kernel(q_ref, k_ref, v_ref, g_ref, dmat_ref, qdf_ref, qdb_ref, kdf_ref, kdb_ref, cdf_ref, cdb_ref,
                    o_ref, fwd_state, bwd_state, bwd_store, *, n_blocks):
    phase = pl.program_id(1)
    n = pl.program_id(2)
    qk_w = B_HEADS * B_DK
    head_of_lane = lax.broadcasted_iota(jnp.int32, (1, qk_w), 1) // B_DK

    def rows(c):
        return slice(c * B_CHUNK, (c + 1) * B_CHUNK)

    def stacked_heads(t):
        lane_head = jnp.concatenate([head_of_lane] * (t.shape[1] // qk_w), axis=1)
        return jnp.concatenate([jnp.where(lane_head == h, t, jnp.zeros_like(t)) for h in range(B_HEADS)], axis=0)

    def head_block_diagonal(states):
        row_head = (lax.broadcasted_iota(jnp.int32, (states.shape[0], 1), 0) % qk_w) // B_DK
        return jnp.concatenate([jnp.where(row_head == h, states, jnp.zeros_like(states)) for h in range(B_HEADS)],
                               axis=1)

    def kv_outer(k_decayed, v):
        full = lax.dot_general(k_decayed, v, (((0,), (0,)), ((), ())), preferred_element_type=F32)
        return jnp.concatenate([full[h * B_DK:(h + 1) * B_DK, h * B_DV:(h + 1) * B_DV] for h in range(B_HEADS)], axis=0)

    def scaled_k(c):
        return k_ref[0, rows(c), :] * (B_DK ** -0.5)

    @pl.when(phase == 0)
    def _():
        @pl.when(n == 0)
        def _():
            bwd_state[...] = jnp.zeros_like(bwd_state)

        blk = n_blocks - 1 - n
        state = bwd_state[...]
        for c in reversed(range(B_BLOCK)):
            bwd_store[blk * B_BLOCK + c] = state.astype(BF16)
            k_dec = (scaled_k(c).astype(F32) * kdb_ref[...]).astype(BF16)
            state = cdb_ref[...] * state + kv_outer(k_dec, v_ref[0, rows(c), :])
        bwd_state[...] = state

    @pl.when(phase == 1)
    def _():
        @pl.when(n == 0)
        def _():
            fwd_state[...] = jnp.zeros_like(fwd_state)

        state = fwd_state[...]
        for c in range(B_BLOCK):
            q = q_ref[0, rows(c), :]
            k = scaled_k(c)
            v = v_ref[0, rows(c), :]
            q32 = q.astype(F32)
            q_dec = jnp.concatenate([(q32 * qdf_ref[...]).astype(BF16), (q32 * qdb_ref[...]).astype(BF16)], axis=1)
            k_dec = (k.astype(F32) * kdf_ref[...]).astype(BF16)
            states = jnp.concatenate([state.astype(BF16), bwd_store[n * B_BLOCK + c]], axis=0)
            s_all = lax.dot_general(q, stacked_heads(k), (((1,), (1,)), ((), ())), preferred_element_type=F32)
            cross_all = jnp.dot(q_dec, head_block_diagonal(states), preferred_element_type=F32)
            for h in range(B_HEADS):
                vh = v[:, h * B_DV:(h + 1) * B_DV]
                inner = jnp.dot((s_all[:, rows(h)] * dmat_ref[h]).astype(BF16), vh, preferred_element_type=F32)
                y = inner + cross_all[:, h * B_DV:(h + 1) * B_DV]
                mu = jnp.mean(y, axis=-1, keepdims=True)
                cen = y - mu
                var = jnp.mean(cen * cen, axis=-1, keepdims=True)
                yn = cen * lax.rsqrt(var + GN_EPS)
                gate = g_ref[0, rows(c), h * B_DV:(h + 1) * B_DV].astype(F32)
                o_ref[0, rows(c), h * B_DV:(h + 1) * B_DV] = (gate * jax.nn.sigmoid(gate) * yn).astype(BF16)
            state = cdf_ref[...] * state + kv_outer(k_dec, v)
        fwd_state[...] = state


def _retention_tables(logit_fwd, logit_bwd):
    lg_f = jax.nn.log_sigmoid(logit_fwd.astype(F32))
    lg_b = jax.nn.log_sigmoid(logit_bwd.astype(F32))
    idx = jnp.arange(B_CHUNK, dtype=F32)
    diff = idx[:, None] - idx[None, :]
    causal = diff >= 0
    dmat = jnp.where(causal[None],
                     jnp.exp(lg_f[:, None, None] * jnp.where(causal, diff, 0.0)[None]),
                     jnp.exp(lg_b[:, None, None] * jnp.where(causal, 0.0, -diff)[None]))

    def per_lane(lg, power):
        return jnp.repeat(jnp.exp(lg[None, :] * power[:, None]), B_DK, axis=1)

    def per_row(lg):
        return jnp.broadcast_to(jnp.repeat(jnp.exp(lg * B_CHUNK), B_DK)[:, None], (B_HEADS * B_DK, B_DV))

    return (dmat, per_lane(lg_f, idx + 1), per_lane(lg_b, B_CHUNK - idx), per_lane(lg_f, B_CHUNK - 1 - idx),
            per_lane(lg_b, idx), per_row(lg_f), per_row(lg_b))


def mixer_b(z, batch, seq, logit_fwd, logit_bwd):
    n_chunks = seq // B_CHUNK
    n_blocks = n_chunks // B_BLOCK
    block_rows = B_BLOCK * B_CHUNK
    zv = z.reshape(batch, seq, MAIN_W)
    qk_w = B_HEADS * B_DK
    v_w = B_HEADS * B_DV

    def scan_block(ph, n):
        return (1 - ph) * (n_blocks - 1 - n) + ph * n

    tables = _retention_tables(logit_fwd, logit_bwd)
    in_specs = [pl.BlockSpec((1, block_rows, qk_w), lambda b, ph, n: (b, ph * n, BQ_BLK256)),
                pl.BlockSpec((1, block_rows, qk_w), lambda b, ph, n: (b, scan_block(ph, n), BK_BLK256)),
                pl.BlockSpec((1, block_rows, v_w), lambda b, ph, n: (b, scan_block(ph, n), BV_BLK)),
                pl.BlockSpec((1, block_rows, v_w), lambda b, ph, n: (b, ph * n, BG_BLK))]
    in_specs += [_const_spec(t.shape) for t in tables]
    out = pl.pallas_call(
        functools.partial(_mixer_b_kernel, n_blocks=n_blocks),
        grid=(batch, 2, n_blocks),
        in_specs=in_specs,
        out_specs=pl.BlockSpec((1, block_rows, v_w), lambda b, ph, n: (b, ph * n, 0)),
        out_shape=jax.ShapeDtypeStruct((batch, seq, v_w), BF16),
        scratch_shapes=[pltpu.VMEM((qk_w, B_DV), F32), pltpu.VMEM((qk_w, B_DV), F32),
                        pltpu.VMEM((n_chunks, qk_w, B_DV), BF16)],
        compiler_params=_params(("arbitrary", "arbitrary", "arbitrary")),
        name="mixer_b",
    )(zv, zv, zv, zv, *tables)
    return out.reshape(batch * seq, v_w)


C_TQ = C_QR * GRID_W
C_KROWS = 3 * C_QR
C_TK = C_KROWS * GRID_W
C_PAIRS = C_KROWS // 2
C_NTAB = 2 * C_KH - 2


def _mixer_c_kernel(above_ref, cur_ref, below_ref, tab_ref, o_ref, kf, vf, *, rows):
    cur = C_STEP * C_TQ
    k_cols = slice(HEAD_BLOCK, 2 * HEAD_BLOCK)
    v_cols = slice(2 * HEAD_BLOCK, 3 * HEAD_BLOCK)
    kf[0:C_TQ, :] = above_ref[0, :, k_cols]
    kf[C_TQ:C_TQ + cur, :] = cur_ref[0, :, k_cols]
    kf[C_TQ + cur:, :] = below_ref[0, :, k_cols]
    vf[0:C_TQ, :] = above_ref[0, :, v_cols]
    vf[C_TQ:C_TQ + cur, :] = cur_ref[0, :, v_cols]
    vf[C_TQ + cur:, :] = below_ref[0, :, v_cols]

    left = lax.broadcasted_iota(jnp.int32, (1, 2 * C_HD), 1) < C_HD
    for sb in range(C_STEP):
        blk = pl.program_id(1) * C_STEP + sb
        q_rows = slice(sb * C_TQ, (sb + 1) * C_TQ)
        k_rows = slice(sb * C_TQ, sb * C_TQ + C_TK)
        key_row = blk * C_QR - C_QR + lax.broadcasted_iota(jnp.int32, (1, C_TK), 1) // GRID_W
        pens = []
        for a in range(C_QR):
            row_start = jnp.clip(blk * C_QR + a - C_KH // 2, 0, rows - C_KH)
            pens.append(jnp.where((key_row >= row_start) & (key_row < row_start + C_KH), 0.0, NEG).astype(F32))

        for hp in range(C_HEADS // 2):
            cols = slice(hp * 2 * C_HD, (hp + 1) * 2 * C_HD)
            qp = cur_ref[0, q_rows, cols]
            kp = kf[k_rows, cols]
            probs = []
            for hh, s in enumerate(_pair_scores(qp, kp, left)):
                h = hp * 2 + hh
                rows_p = []
                for a in range(C_QR):
                    bias = jnp.concatenate([tab_ref[h, 2 * t - C_QR - a + C_KH - 1] for t in range(C_PAIRS)], axis=1)
                    sa = s[a * GRID_W:(a + 1) * GRID_W, :] + bias + pens[a]
                    m = jnp.max(sa, axis=-1, keepdims=True)
                    rows_p.append(jnp.exp2(sa - m).astype(BF16))
                probs.append(jnp.concatenate(rows_p, axis=0))
            num, den = _pair_weighted_sum(probs, vf[k_rows, cols], left)
            o_ref[0, q_rows, cols] = (num / den).astype(BF16)


def _neighbourhood_bias(rpb):
    qc = jnp.arange(GRID_W)[:, None]
    kc = jnp.arange(GRID_W)[None, :]
    col_start = jnp.clip(qc - C_KW // 2, 0, GRID_W - C_KW)
    col_ok = (kc >= col_start) & (kc < col_start + C_KW)
    onehot = ((kc - qc + (C_KW - 1))[:, :, None] == jnp.arange(2 * C_KW - 1)[None, None, :]).astype(F32)
    band = jnp.einsum("hrd,qkd->hrqk", rpb.astype(F32), onehot, precision=lax.Precision.HIGHEST)
    band = jnp.where(col_ok[None, None], band * LOG2E, NEG)
    return jnp.concatenate([band[:, :C_NTAB], band[:, 1:C_NTAB + 1]], axis=-1)


def mixer_c(z, batch, seq, rpb):
    rows = seq // GRID_W
    n_blk = rows // C_QR
    zv = z.reshape(batch, seq, MAIN_W)
    tab = _neighbourhood_bias(rpb)
    halo = (1, C_TQ, A_QKV_W)
    tile = (1, C_STEP * C_TQ, A_QKV_W)
    out = pl.pallas_call(
        functools.partial(_mixer_c_kernel, rows=rows),
        grid=(batch, n_blk // C_STEP),
        in_specs=[pl.BlockSpec(halo, lambda b, i: (b, jnp.maximum(i * C_STEP - 1, 0), C_QKV_BLK)),
                  pl.BlockSpec(tile, lambda b, i: (b, i, C_QKV_BLK)),
                  pl.BlockSpec(halo, lambda b, i: (b, jnp.minimum((i + 1) * C_STEP, n_blk - 1), C_QKV_BLK)),
                  _const_spec(tab.shape)],
        out_specs=pl.BlockSpec((1, C_STEP * C_TQ, HEAD_BLOCK), lambda b, i: (b, i, 0)),
        out_shape=jax.ShapeDtypeStruct((batch, seq, HEAD_BLOCK), BF16),
        scratch_shapes=[pltpu.VMEM(((C_STEP + 2) * C_TQ, HEAD_BLOCK), BF16),
                        pltpu.VMEM(((C_STEP + 2) * C_TQ, HEAD_BLOCK), BF16)],
        compiler_params=_params(("arbitrary", "arbitrary")),
        name="mixer_c",
    )(zv, zv, zv, tab)
    return out.reshape(batch * seq, HEAD_BLOCK)


def _merge_kernel(x_ref, oa0_ref, oa1_ref, oa2_ref, la0_ref, la1_ref, la2_ref, yb_ref, yc_ref, gates_ref, spread_ref,
                  wa_ref, wb_ref, wc_ref, wo_ref, g_ref, b_ref, o_ref):
    lses = [la0_ref[...], la1_ref[...], la2_ref[...]]
    top = jnp.maximum(jnp.maximum(lses[0], lses[1]), lses[2])
    weights = [jnp.exp2(lse - top) for lse in lses]
    total = weights[0] + weights[1] + weights[2]
    ya = None
    for o_g_ref, w in zip((oa0_ref, oa1_ref, oa2_ref), weights):
        share = w / total
        hi = share.astype(BF16)
        lo = (share - hi.astype(F32)).astype(BF16)
        wide = jnp.dot(jnp.concatenate([hi, lo], axis=1), spread_ref[...], preferred_element_type=F32)
        term = wide * o_g_ref[...].astype(F32)
        ya = term if ya is None else ya + term
    ya = ya.astype(BF16)

    merged = None
    for br, (y, w_ref) in enumerate(((ya, wa_ref), (yb_ref[...], wb_ref), (yc_ref[...], wc_ref))):
        proj = jnp.dot(y, w_ref[...], preferred_element_type=F32)
        gate = gates_ref[:, br * D_MODEL:(br + 1) * D_MODEL].astype(F32)
        merged = gate * proj if merged is None else merged + gate * proj
    merged = merged.astype(BF16)
    for r0 in range(0, x_ref.shape[0], MERGE_ROWS):
        rows = slice(r0, r0 + MERGE_ROWS)
        out = jnp.dot(merged[rows], wo_ref[...], preferred_element_type=F32)
        o_ref[rows, :] = _layer_norm(ALPHA * x_ref[rows, :] + out, g_ref[...], b_ref[...])


def _lse_spread():
    src = jnp.arange(2 * A_HD)[:, None]
    head = jnp.arange(HEAD_BLOCK)[None, :] // A_HD
    first_lane = (head % 2) * A_HD + (head // 2) * LSE_LANES
    once = (src == first_lane).astype(BF16)
    return jnp.concatenate([once, once], axis=0)


def merge_out_ln(x, oa, la, yb, yc, z, wa, wb, wc, wo, g, b):
    t = x.shape[0]
    tm = MERGE_TILE_ROWS
    row = pl.BlockSpec((tm, D_MODEL), lambda i: (i, 0))
    br = pl.BlockSpec((tm, HEAD_BLOCK), lambda i: (i, 0))
    lse = pl.BlockSpec((tm, 2 * A_HD), lambda i: (i, 0))
    w_br = _const_spec((HEAD_BLOCK, D_MODEL))
    return pl.pallas_call(
        _merge_kernel,
        grid=(t // tm,),
        in_specs=[row] + [br] * 3 + [lse] * 3 + [br] * 2 + [
            pl.BlockSpec((tm, GATE_W), lambda i: (i, 0)), _const_spec((4 * A_HD, HEAD_BLOCK)), w_br, w_br, w_br,
            _const_spec((D_MODEL, D_MODEL)), _const_spec((1, D_MODEL)), _const_spec((1, D_MODEL))],
        out_specs=row,
        out_shape=jax.ShapeDtypeStruct((t, D_MODEL), F32),
        compiler_params=_params(("arbitrary",)),
        name="merge_out_ln",
    )(x, *oa, *la, yb, yc, z, _lse_spread(), wa, wb, wc, wo, g, b)


def _split_in_weights(w_in):
    a_w = A_GROUPS * HEAD_BLOCK
    aq, ak, av = w_in[:, :a_w] * (A_HD ** -0.5 * LOG2E), w_in[:, a_w:2 * a_w], w_in[:, 2 * a_w:3 * a_w]
    b_w = 2 * B_HEADS * (B_DK + B_DV)
    rest_b = w_in[:, 3 * a_w:3 * a_w + b_w]
    cq = w_in[:, 3 * a_w + b_w:3 * a_w + b_w + HEAD_BLOCK] * (C_HD ** -0.5 * LOG2E)
    rest = jnp.concatenate([rest_b, cq, w_in[:, 3 * a_w + b_w + HEAD_BLOCK:D_IN - GATE_W]], axis=1)

    def group(g):
        cols = slice(g * HEAD_BLOCK, (g + 1) * HEAD_BLOCK)
        return jnp.concatenate([aq[:, cols], ak[:, cols], av[:, cols]], axis=1)

    main = jnp.concatenate([w_in[:, D_IN - GATE_W:], group(0), rest], axis=1)
    return main.astype(BF16), [group(g).astype(BF16) for g in range(1, A_GROUPS)]


def _trunk(x, layers):
    batch, seq, _ = x.shape
    x = x.reshape(batch * seq, D_MODEL)
    for p in layers:
        dils = tuple(d for _, d in A_PATTERNS[1:])
        x1, *x1_by_residue = ffn_ln(x, p["wg1"], p["wu1"], p["wd1"], p["g1"], p["b1"], batch, dils)
        z = in_proj(x1, p["w_main"], gate_cols=GATE_W)
        oa, la = [], []
        o, l = mixer_a_group(z.reshape(batch, 1, seq, MAIN_W), 1, A_QKV_BLK)
        oa.append(o.reshape(batch * seq, HEAD_BLOCK))
        la.append(l.reshape(batch * seq, 2 * A_HD))
        for dilation, xg, w_g in zip(dils, x1_by_residue, p["w_groups"]):
            zg = in_proj(xg.reshape(batch * seq, D_MODEL), w_g)
            o, l = mixer_a_group(zg.reshape(batch, dilation, seq // dilation, A_QKV_W), dilation, 0)
            oa.append(_by_token(o, batch, seq))
            la.append(_by_token(l, batch, seq))
        yb = mixer_b(z, batch, seq, p["logit_fwd"], p["logit_bwd"])
        yc = mixer_c(z, batch, seq, p["rpb"])
        x2 = merge_out_ln(x1, oa, la, yb, yc, z, p["wa"], p["wb"], p["wc"], p["wo"], p["g2"], p["b2"])
        (x,) = ffn_ln(x2, p["wg2"], p["wu2"], p["wd2"], p["g3"], p["b3"])
    return x.reshape(batch, seq, D_MODEL)


def kernel(x_prompt, x_sample, ffn1_w_gate, ffn1_w_up, ffn1_w_down, ln1_g, ln1_b, w_in, ret_logit_fwd, ret_logit_bwd, na_rpb, w_branch_a, w_branch_b, w_branch_c, w_out, ln2_g, ln2_b, ffn2_w_gate, ffn2_w_up, ffn2_w_down, ln3_g, ln3_b):
    def vec(v):
        return v.astype(F32).reshape(1, D_MODEL)

    layers = []
    for i in range(DEPTH):
        w_main, w_groups = _split_in_weights(w_in[i])
        layers.append(dict(
            wg1=ffn1_w_gate[i].astype(BF16), wu1=ffn1_w_up[i].astype(BF16), wd1=ffn1_w_down[i].astype(BF16),
            g1=vec(ln1_g[i]), b1=vec(ln1_b[i]),
            w_main=w_main, w_groups=w_groups,
            logit_fwd=ret_logit_fwd[i], logit_bwd=ret_logit_bwd[i], rpb=na_rpb[i],
            wa=w_branch_a[i].astype(BF16), wb=w_branch_b[i].astype(BF16), wc=w_branch_c[i].astype(BF16),
            wo=w_out[i].astype(BF16), g2=vec(ln2_g[i]), b2=vec(ln2_b[i]),
            wg2=ffn2_w_gate[i].astype(BF16), wu2=ffn2_w_up[i].astype(BF16), wd2=ffn2_w_down[i].astype(BF16),
            g3=vec(ln3_g[i]), b3=vec(ln3_b[i])))
    return (_trunk(x_prompt, layers), _trunk(x_sample, layers))
```

```python
import functools

import jax
import jax.numpy as jnp
from jax import lax
from jax.experimental import pallas as pl
from jax.experimental.pallas import tpu as pltpu

F32 = jnp.float32
BF16 = jnp.bfloat16

D_MODEL = 1024
DEPTH = 2
D_FF = 2816
LN_EPS = 1e-5
GN_EPS = 1e-5
ALPHA = (2 * DEPTH) ** 0.25

A_PATTERNS = ((128, 1), (512, 4), (2048, 16))
A_GROUPS = len(A_PATTERNS)
A_HEADS = 8
A_HD = 64
A_HALF = 64
LSE_LANES = A_HD // (A_HEADS // 2)
B_HEADS = 4
B_DK = 64
B_DV = 128
B_CHUNK = 128
B_BLOCK = 16
C_HEADS = 8
C_HD = 64
GRID_W = 64
C_KH = 8
C_KW = 16
C_QR = 4
C_STEP = 8
D_IN = 10752
NEG = -1e30
LANES = 128
LOG2E = 1.4426950408889634

GATE_W = 3 * D_MODEL
HEAD_BLOCK = 512
A_QKV_W = 3 * HEAD_BLOCK
MAIN_W = D_IN - (A_GROUPS - 1) * A_QKV_W
A_QKV_BLK = 2
BQ_BLK256, BK_BLK256 = 18, 19
BV_BLK, BG_BLK = 10, 11
C_QKV_BLK = 4

VMEM_LIMIT = 56 * 1024 * 1024

FFN_ROWS = 1024
PROJ_ROWS = 2048
PROJ_COLS = 2560
MERGE_TILE_ROWS = 1024
A_TILE = 2048
FF_CHUNKS = ((0, 512), (512, 1024), (1024, 1536), (1536, 2048), (2048, 2560), (2560, 2816))
FF_NORM_ROWS = 256
MERGE_ROWS = 256


def _params(sem):
    return pltpu.CompilerParams(dimension_semantics=sem, vmem_limit_bytes=VMEM_LIMIT)


def _const_spec(shape):
    zeros = (0,) * len(shape)
    return pl.BlockSpec(shape, lambda *_: zeros)


def _layer_norm(r, g, b):
    mu = jnp.mean(r, axis=-1, keepdims=True)
    c = r - mu
    var = jnp.mean(c * c, axis=-1, keepdims=True)
    return c * lax.rsqrt(var + LN_EPS) * g + b


def _ffn_ln_kernel(x_ref, wg_ref, wu_ref, wd_ref, g_ref, b_ref, o_ref, *rest, dilations):
    x = x_ref[...]
    xb = x.astype(BF16)
    hidden = []
    for c0, c1 in FF_CHUNKS:
        gate = jnp.dot(xb, wg_ref[:, c0:c1], preferred_element_type=F32)
        up = jnp.dot(xb, wu_ref[:, c0:c1], preferred_element_type=F32)
        hidden.append((gate * jax.nn.sigmoid(gate) * up).astype(BF16))
    hidden = jnp.concatenate(hidden, axis=1)
    if rest:
        *og_refs, slabs = rest
    n_slabs = D_MODEL // LANES
    rows = FF_NORM_ROWS
    for r0 in range(0, x.shape[0], rows):
        acc = jnp.dot(hidden[r0:r0 + rows], wd_ref[...], preferred_element_type=F32)
        y = _layer_norm(ALPHA * x[r0:r0 + rows] + 0.5 * acc, g_ref[...], b_ref[...])
        o_ref[r0:r0 + rows, :] = y
        if not rest:
            continue
        done = 1
        ordered = y
        for d, og_ref in zip(dilations, og_refs):
            for s in range(n_slabs):
                slabs[s] = ordered[:, s * LANES:(s + 1) * LANES]
            step = d // done
            group = rows // done
            pieces = []
            for r in range(d):
                first = (r % done) * group + r // done
                picked = [slabs[s, pl.ds(first, rows // d, stride=step), :] for s in range(n_slabs)]
                pieces.append(jnp.concatenate(picked, axis=1))
                og_ref[0, r, r0 // d:(r0 + rows) // d, :] = pieces[-1].astype(BF16)
            ordered = jnp.concatenate(pieces, axis=0)
            done = d


def ffn_ln(x, wg, wu, wd, g, b, batch=None, dilations=()):
    t = x.shape[0]
    tm = FFN_ROWS
    row = pl.BlockSpec((tm, D_MODEL), lambda i: (i, 0))
    out_shape = [jax.ShapeDtypeStruct((t, D_MODEL), F32)]
    out_specs = [row]
    scratch = []
    if dilations:
        assert all(b % a == 0 for a, b in zip((1,) + dilations, dilations)), dilations
        tiles_per_seq = t // batch // tm
        for d in dilations:
            out_shape.append(jax.ShapeDtypeStruct((batch, d, t // batch // d, D_MODEL), BF16))
            out_specs.append(pl.BlockSpec((1, d, tm // d, D_MODEL),
                                          lambda i: (i // tiles_per_seq, 0, i % tiles_per_seq, 0)))
        scratch = [pltpu.VMEM((D_MODEL // LANES, FF_NORM_ROWS, LANES), F32)]
    return pl.pallas_call(
        functools.partial(_ffn_ln_kernel, dilations=dilations),
        grid=(t // tm,),
        in_specs=[row, _const_spec((D_MODEL, D_FF)), _const_spec((D_MODEL, D_FF)), _const_spec((D_FF, D_MODEL)),
                  _const_spec((1, D_MODEL)), _const_spec((1, D_MODEL))],
        out_specs=out_specs,
        out_shape=out_shape,
        scratch_shapes=scratch,
        compiler_params=_params(("arbitrary",)),
        name="ffn_ln",
    )(x, wg, wu, wd, g, b)


def _in_proj_kernel(x_ref, w_ref, z_ref):
    z_ref[...] = jnp.dot(x_ref[...].astype(BF16), w_ref[...], preferred_element_type=F32).astype(BF16)


def in_proj(xb, w_in):
    t = xb.shape[0]
    width = w_in.shape[1]
    tm = PROJ_ROWS
    tn = PROJ_COLS if width % PROJ_COLS == 0 else A_QKV_W
    return pl.pallas_call(
        _in_proj_kernel,
        grid=(t // tm, width // tn),
        in_specs=[pl.BlockSpec((tm, D_MODEL), lambda i, j: (i, 0)), pl.BlockSpec((D_MODEL, tn), lambda i, j: (0, j))],
        out_specs=pl.BlockSpec((tm, tn), lambda i, j: (i, j)),
        out_shape=jax.ShapeDtypeStruct((t, width), BF16),
        compiler_params=_params(("arbitrary", "arbitrary")),
        name="in_proj",
    )(xb, w_in)


def _pair_scores(q_pair, k_pair, left):
    zero = jnp.zeros_like(q_pair)
    dims = (((1,), (1,)), ((), ()))
    return [lax.dot_general(jnp.where(left, q_pair, zero), k_pair, dims, preferred_element_type=F32),
            lax.dot_general(jnp.where(left, zero, q_pair), k_pair, dims, preferred_element_type=F32)]


def _pair_weighted_sum(probs, v_pair, left):
    rhs = []
    for hh in range(2):
        sel = left if hh == 0 else jnp.logical_not(left)
        ones = jnp.broadcast_to(jnp.where(sel, 1.0, 0.0).astype(BF16), v_pair.shape)
        rhs.append(jnp.concatenate([jnp.where(sel, v_pair, jnp.zeros_like(v_pair)), ones], axis=1))
    out = jnp.dot(jnp.concatenate(probs, axis=1), jnp.concatenate(rhs, axis=0), preferred_element_type=F32)
    width = v_pair.shape[1]
    return out[:, :width], out[:, width:]


A_QB = 128
A_KB = A_QB + 2 * A_HALF


def _mixer_a_kernel(before_ref, cur_ref, after_ref, bias_ref, o_ref, l_ref, kf, vf, *, tl, seq_len):
    i = pl.program_id(2)
    k_cols = slice(HEAD_BLOCK, 2 * HEAD_BLOCK)
    v_cols = slice(2 * HEAD_BLOCK, 3 * HEAD_BLOCK)
    lane = lax.broadcasted_iota(jnp.int32, (1, 2 * A_HD), 1)
    left = lane < A_HD
    lse_slot = (lane % A_HD) // LSE_LANES

    for res in range(cur_ref.shape[1]):
        kf[res, 0:A_HALF, :] = before_ref[0, res, :, k_cols]
        kf[res, A_HALF:A_HALF + tl, :] = cur_ref[0, res, :, k_cols]
        kf[res, A_HALF + tl:, :] = after_ref[0, res, :, k_cols]
        vf[res, 0:A_HALF, :] = before_ref[0, res, :, v_cols]
        vf[res, A_HALF:A_HALF + tl, :] = cur_ref[0, res, :, v_cols]
        vf[res, A_HALF + tl:, :] = after_ref[0, res, :, v_cols]

        for j in range(tl // A_QB):
            q0 = j * A_QB
            start = i * tl + q0
            variant = (start == 0).astype(jnp.int32) + 2 * (start + A_QB == seq_len).astype(jnp.int32)
            lse_all = None
            for hp in range(A_HEADS // 2):
                cols = slice(hp * 2 * A_HD, (hp + 1) * 2 * A_HD)
                qp = cur_ref[0, res, q0:q0 + A_QB, cols]
                kp = kf[res, q0:q0 + A_KB, cols]
                probs, tops = [], []
                for hh, s in enumerate(_pair_scores(qp, kp, left)):
                    s = s + bias_ref[variant, hp * 2 + hh]
                    m = jnp.max(s, axis=-1, keepdims=True)
                    probs.append(jnp.exp2(s - m).astype(BF16))
                    tops.append(m)
                num, den = _pair_weighted_sum(probs, vf[res, q0:q0 + A_KB, cols], left)
                o_ref[0, res, q0:q0 + A_QB, cols] = (num / den).astype(BF16)
                lse_pair = jnp.where(left, tops[0], tops[1]) + jnp.log2(den)
                lse_all = lse_pair if hp == 0 else jnp.where(lse_slot == hp, lse_pair, lse_all)
            l_ref[0, res, q0:q0 + A_QB, :] = lse_all


def _alibi_bias(dilation):
    slopes = 2.0 ** (-8.0 * jnp.arange(1, A_HEADS + 1, dtype=F32) / A_HEADS)
    key = jnp.arange(A_KB)[None, :] - A_HALF
    rel = key - jnp.arange(A_QB)[:, None]
    dist = (jnp.abs(rel) * dilation).astype(F32)
    bias = -slopes[:, None, None] * dist[None] * LOG2E
    in_window = jnp.abs(rel) <= A_HALF
    variants = []
    for v in range(4):
        ok = in_window
        if v & 1:
            ok = ok & (key >= 0)
        if v & 2:
            ok = ok & (key < A_QB)
        variants.append(jnp.where(ok[None], bias, NEG))
    return jnp.stack(variants, 0)


def mixer_a_group(zg, dilation, qkv_blk):
    batch, _, strided_len, _ = zg.shape
    tl = min(A_TILE, strided_len)
    n_res = min(A_TILE // tl, dilation)
    halo_per_tile = tl // A_HALF
    n_halo = strided_len // A_HALF
    out_dims = (batch, dilation, strided_len, HEAD_BLOCK)
    return pl.pallas_call(
        functools.partial(_mixer_a_kernel, tl=tl, seq_len=strided_len),
        grid=(batch, dilation // n_res, strided_len // tl),
        in_specs=[pl.BlockSpec((1, n_res, A_HALF, A_QKV_W),
                               lambda b, r, i: (b, r, jnp.maximum(i * halo_per_tile - 1, 0), qkv_blk)),
                  pl.BlockSpec((1, n_res, tl, A_QKV_W), lambda b, r, i: (b, r, i, qkv_blk)),
                  pl.BlockSpec((1, n_res, A_HALF, A_QKV_W),
                               lambda b, r, i: (b, r, jnp.minimum((i + 1) * halo_per_tile, n_halo - 1), qkv_blk)),
                  _const_spec((4, A_HEADS, A_QB, A_KB))],
        out_specs=[pl.BlockSpec((1, n_res, tl, HEAD_BLOCK), lambda b, r, i: (b, r, i, 0)),
                   pl.BlockSpec((1, n_res, tl, 2 * A_HD), lambda b, r, i: (b, r, i, 0))],
        out_shape=[jax.ShapeDtypeStruct(out_dims, BF16),
                   jax.ShapeDtypeStruct((batch, dilation, strided_len, 2 * A_HD), F32)],
        scratch_shapes=[pltpu.VMEM((n_res, tl + 2 * A_HALF, HEAD_BLOCK), BF16),
                        pltpu.VMEM((n_res, tl + 2 * A_HALF, HEAD_BLOCK), BF16)],
        compiler_params=_params(("arbitrary", "arbitrary", "arbitrary")),
        name=f"mixer_a_d{dilation}",
    )(zg, zg, zg, _alibi_bias(dilation))


def _by_token(x, batch, seq):
    return x.transpose(0, 2, 1, 3).reshape(batch * seq, x.shape[-1])


def _mixer_b_kernel(q_ref, k_ref, v_ref, g_ref, dmat_ref, qdf_ref, qdb_ref, kdf_ref, kdb_ref, cdf_ref, cdb_ref,
                    o_ref, fwd_state, bwd_state, bwd_store, *, n_blocks):
    phase = pl.program_id(1)
    n = pl.program_id(2)
    qk_w = B_HEADS * B_DK
    head_of_lane = lax.broadcasted_iota(jnp.int32, (1, qk_w), 1) // B_DK

    def rows(c):
        return slice(c * B_CHUNK, (c + 1) * B_CHUNK)

    def stacked_heads(t):
        lane_head = jnp.concatenate([head_of_lane] * (t.shape[1] // qk_w), axis=1)
        return jnp.concatenate([jnp.where(lane_head == h, t, jnp.zeros_like(t)) for h in range(B_HEADS)], axis=0)

    def head_block_diagonal(states):
        row_head = (lax.broadcasted_iota(jnp.int32, (states.shape[0], 1), 0) % qk_w) // B_DK
        return jnp.concatenate([jnp.where(row_head == h, states, jnp.zeros_like(states)) for h in range(B_HEADS)],
                               axis=1)

    def kv_outer(k_decayed, v):
        full = lax.dot_general(k_decayed, v, (((0,), (0,)), ((), ())), preferred_element_type=F32)
        return jnp.concatenate([full[h * B_DK:(h + 1) * B_DK, h * B_DV:(h + 1) * B_DV] for h in range(B_HEADS)], axis=0)

    def scaled_k(c):
        return k_ref[0, rows(c), :] * (B_DK ** -0.5)

    @pl.when(phase == 0)
    def _():
        @pl.when(n == 0)
        def _():
            bwd_state[...] = jnp.zeros_like(bwd_state)

        blk = n_blocks - 1 - n
        state = bwd_state[...]
        for c in reversed(range(B_BLOCK)):
            bwd_store[blk * B_BLOCK + c] = state.astype(BF16)
            k_dec = (scaled_k(c).astype(F32) * kdb_ref[...]).astype(BF16)
            state = cdb_ref[...] * state + kv_outer(k_dec, v_ref[0, rows(c), :])
        bwd_state[...] = state

    @pl.when(phase == 1)
    def _():
        @pl.when(n == 0)
        def _():
            fwd_state[...] = jnp.zeros_like(fwd_state)

        state = fwd_state[...]
        for c in range(B_BLOCK):
            q = q_ref[0, rows(c), :]
            k = scaled_k(c)
            v = v_ref[0, rows(c), :]
            q32 = q.astype(F32)
            q_dec = jnp.concatenate([(q32 * qdf_ref[...]).astype(BF16), (q32 * qdb_ref[...]).astype(BF16)], axis=1)
            k_dec = (k.astype(F32) * kdf_ref[...]).astype(BF16)
            states = jnp.concatenate([state.astype(BF16), bwd_store[n * B_BLOCK + c]], axis=0)
            s_all = lax.dot_general(q, stacked_heads(k), (((1,), (1,)), ((), ())), preferred_element_type=F32)
            cross_all = jnp.dot(q_dec, head_block_diagonal(states), preferred_element_type=F32)
            for h in range(B_HEADS):
                vh = v[:, h * B_DV:(h + 1) * B_DV]
                inner = jnp.dot((s_all[:, rows(h)] * dmat_ref[h]).astype(BF16), vh, preferred_element_type=F32)
                y = inner + cross_all[:, h * B_DV:(h + 1) * B_DV]
                mu = jnp.mean(y, axis=-1, keepdims=True)
                cen = y - mu
                var = jnp.mean(cen * cen, axis=-1, keepdims=True)
                yn = cen * lax.rsqrt(var + GN_EPS)
                gate = g_ref[0, rows(c), h * B_DV:(h + 1) * B_DV].astype(F32)
                o_ref[0, rows(c), h * B_DV:(h + 1) * B_DV] = (gate * jax.nn.sigmoid(gate) * yn).astype(BF16)
            state = cdf_ref[...] * state + kv_outer(k_dec, v)
        fwd_state[...] = state


def _retention_tables(logit_fwd, logit_bwd):
    lg_f = jax.nn.log_sigmoid(logit_fwd.astype(F32))
    lg_b = jax.nn.log_sigmoid(logit_bwd.astype(F32))
    idx = jnp.arange(B_CHUNK, dtype=F32)
    diff = idx[:, None] - idx[None, :]
    causal = diff >= 0
    dmat = jnp.where(causal[None],
                     jnp.exp(lg_f[:, None, None] * jnp.where(causal, diff, 0.0)[None]),
                     jnp.exp(lg_b[:, None, None] * jnp.where(causal, 0.0, -diff)[None]))

    def per_lane(lg, power):
        return jnp.repeat(jnp.exp(lg[None, :] * power[:, None]), B_DK, axis=1)

    def per_row(lg):
        return jnp.broadcast_to(jnp.repeat(jnp.exp(lg * B_CHUNK), B_DK)[:, None], (B_HEADS * B_DK, B_DV))

    return (dmat, per_lane(lg_f, idx + 1), per_lane(lg_b, B_CHUNK - idx), per_lane(lg_f, B_CHUNK - 1 - idx),
            per_lane(lg_b, idx), per_row(lg_f), per_row(lg_b))


def mixer_b(z, batch, seq, logit_fwd, logit_bwd):
    n_chunks = seq // B_CHUNK
    n_blocks = n_chunks // B_BLOCK
    block_rows = B_BLOCK * B_CHUNK
    zv = z.reshape(batch, seq, MAIN_W)
    qk_w = B_HEADS * B_DK
    v_w = B_HEADS * B_DV

    def scan_block(ph, n):
        return (1 - ph) * (n_blocks - 1 - n) + ph * n

    tables = _retention_tables(logit_fwd, logit_bwd)
    in_specs = [pl.BlockSpec((1, block_rows, qk_w), lambda b, ph, n: (b, ph * n, BQ_BLK256)),
                pl.BlockSpec((1, block_rows, qk_w), lambda b, ph, n: (b, scan_block(ph, n), BK_BLK256)),
                pl.BlockSpec((1, block_rows, v_w), lambda b, ph, n: (b, scan_block(ph, n), BV_BLK)),
                pl.BlockSpec((1, block_rows, v_w), lambda b, ph, n: (b, ph * n, BG_BLK))]
    in_specs += [_const_spec(t.shape) for t in tables]
    out = pl.pallas_call(
        functools.partial(_mixer_b_kernel, n_blocks=n_blocks),
        grid=(batch, 2, n_blocks),
        in_specs=in_specs,
        out_specs=pl.BlockSpec((1, block_rows, v_w), lambda b, ph, n: (b, ph * n, 0)),
        out_shape=jax.ShapeDtypeStruct((batch, seq, v_w), BF16),
        scratch_shapes=[pltpu.VMEM((qk_w, B_DV), F32), pltpu.VMEM((qk_w, B_DV), F32),
                        pltpu.VMEM((n_chunks, qk_w, B_DV), BF16)],
        compiler_params=_params(("arbitrary", "arbitrary", "arbitrary")),
        name="mixer_b",
    )(zv, zv, zv, zv, *tables)
    return out.reshape(batch * seq, v_w)


C_TQ = C_QR * GRID_W
C_KROWS = 3 * C_QR
C_TK = C_KROWS * GRID_W
C_PAIRS = C_KROWS // 2
C_NTAB = 2 * C_KH - 2


def _mixer_c_kernel(above_ref, cur_ref, below_ref, tab_ref, o_ref, kf, vf, *, rows):
    cur = C_STEP * C_TQ
    k_cols = slice(HEAD_BLOCK, 2 * HEAD_BLOCK)
    v_cols = slice(2 * HEAD_BLOCK, 3 * HEAD_BLOCK)
    kf[0:C_TQ, :] = above_ref[0, :, k_cols]
    kf[C_TQ:C_TQ + cur, :] = cur_ref[0, :, k_cols]
    kf[C_TQ + cur:, :] = below_ref[0, :, k_cols]
    vf[0:C_TQ, :] = above_ref[0, :, v_cols]
    vf[C_TQ:C_TQ + cur, :] = cur_ref[0, :, v_cols]
    vf[C_TQ + cur:, :] = below_ref[0, :, v_cols]

    left = lax.broadcasted_iota(jnp.int32, (1, 2 * C_HD), 1) < C_HD
    for sb in range(C_STEP):
        blk = pl.program_id(1) * C_STEP + sb
        q_rows = slice(sb * C_TQ, (sb + 1) * C_TQ)
        k_rows = slice(sb * C_TQ, sb * C_TQ + C_TK)
        key_row = blk * C_QR - C_QR + lax.broadcasted_iota(jnp.int32, (1, C_TK), 1) // GRID_W
        pens = []
        for a in range(C_QR):
            row_start = jnp.clip(blk * C_QR + a - C_KH // 2, 0, rows - C_KH)
            pens.append(jnp.where((key_row >= row_start) & (key_row < row_start + C_KH), 0.0, NEG).astype(F32))

        for hp in range(C_HEADS // 2):
            cols = slice(hp * 2 * C_HD, (hp + 1) * 2 * C_HD)
            qp = cur_ref[0, q_rows, cols]
            kp = kf[k_rows, cols]
            probs = []
            for hh, s in enumerate(_pair_scores(qp, kp, left)):
                h = hp * 2 + hh
                rows_p = []
                for a in range(C_QR):
                    bias = jnp.concatenate([tab_ref[h, 2 * t - C_QR - a + C_KH - 1] for t in range(C_PAIRS)], axis=1)
                    sa = s[a * GRID_W:(a + 1) * GRID_W, :] + bias + pens[a]
                    m = jnp.max(sa, axis=-1, keepdims=True)
                    rows_p.append(jnp.exp2(sa - m).astype(BF16))
                probs.append(jnp.concatenate(rows_p, axis=0))
            num, den = _pair_weighted_sum(probs, vf[k_rows, cols], left)
            o_ref[0, q_rows, cols] = (num / den).astype(BF16)


def _neighbourhood_bias(rpb):
    qc = jnp.arange(GRID_W)[:, None]
    kc = jnp.arange(GRID_W)[None, :]
    col_start = jnp.clip(qc - C_KW // 2, 0, GRID_W - C_KW)
    col_ok = (kc >= col_start) & (kc < col_start + C_KW)
    onehot = ((kc - qc + (C_KW - 1))[:, :, None] == jnp.arange(2 * C_KW - 1)[None, None, :]).astype(F32)
    band = jnp.einsum("hrd,qkd->hrqk", rpb.astype(F32), onehot, precision=lax.Precision.HIGHEST)
    band = jnp.where(col_ok[None, None], band * LOG2E, NEG)
    return jnp.concatenate([band[:, :C_NTAB], band[:, 1:C_NTAB + 1]], axis=-1)


def mixer_c(z, batch, seq, rpb):
    rows = seq // GRID_W
    n_blk = rows // C_QR
    zv = z.reshape(batch, seq, MAIN_W)
    tab = _neighbourhood_bias(rpb)
    halo = (1, C_TQ, A_QKV_W)
    tile = (1, C_STEP * C_TQ, A_QKV_W)
    out = pl.pallas_call(
        functools.partial(_mixer_c_kernel, rows=rows),
        grid=(batch, n_blk // C_STEP),
        in_specs=[pl.BlockSpec(halo, lambda b, i: (b, jnp.maximum(i * C_STEP - 1, 0), C_QKV_BLK)),
                  pl.BlockSpec(tile, lambda b, i: (b, i, C_QKV_BLK)),
                  pl.BlockSpec(halo, lambda b, i: (b, jnp.minimum((i + 1) * C_STEP, n_blk - 1), C_QKV_BLK)),
                  _const_spec(tab.shape)],
        out_specs=pl.BlockSpec((1, C_STEP * C_TQ, HEAD_BLOCK), lambda b, i: (b, i, 0)),
        out_shape=jax.ShapeDtypeStruct((batch, seq, HEAD_BLOCK), BF16),
        scratch_shapes=[pltpu.VMEM(((C_STEP + 2) * C_TQ, HEAD_BLOCK), BF16),
                        pltpu.VMEM(((C_STEP + 2) * C_TQ, HEAD_BLOCK), BF16)],
        compiler_params=_params(("arbitrary", "arbitrary")),
        name="mixer_c",
    )(zv, zv, zv, tab)
    return out.reshape(batch * seq, HEAD_BLOCK)


def _merge_kernel(x_ref, oa0_ref, oa1_ref, oa2_ref, la0_ref, la1_ref, la2_ref, yb_ref, yc_ref, gates_ref, spread_ref,
                  wa_ref, wb_ref, wc_ref, wo_ref, g_ref, b_ref, o_ref):
    lses = [la0_ref[...], la1_ref[...], la2_ref[...]]
    top = jnp.maximum(jnp.maximum(lses[0], lses[1]), lses[2])
    weights = [jnp.exp2(lse - top) for lse in lses]
    total = weights[0] + weights[1] + weights[2]
    ya = None
    for o_g_ref, w in zip((oa0_ref, oa1_ref, oa2_ref), weights):
        share = w / total
        hi = share.astype(BF16)
        lo = (share - hi.astype(F32)).astype(BF16)
        wide = jnp.dot(jnp.concatenate([hi, lo], axis=1), spread_ref[...], preferred_element_type=F32)
        term = wide * o_g_ref[...].astype(F32)
        ya = term if ya is None else ya + term
    ya = ya.astype(BF16)

    merged = None
    for br, (y, w_ref) in enumerate(((ya, wa_ref), (yb_ref[...], wb_ref), (yc_ref[...], wc_ref))):
        proj = jnp.dot(y, w_ref[...], preferred_element_type=F32)
        gate = 0.5 * jnp.tanh(0.5 * gates_ref[:, br * D_MODEL:(br + 1) * D_MODEL].astype(F32)) + 0.5
        merged = gate * proj if merged is None else merged + gate * proj
    merged = merged.astype(BF16)
    for r0 in range(0, x_ref.shape[0], MERGE_ROWS):
        rows = slice(r0, r0 + MERGE_ROWS)
        out = jnp.dot(merged[rows], wo_ref[...], preferred_element_type=F32)
        o_ref[rows, :] = _layer_norm(ALPHA * x_ref[rows, :] + out, g_ref[...], b_ref[...])


def _lse_spread():
    src = jnp.arange(2 * A_HD)[:, None]
    head = jnp.arange(HEAD_BLOCK)[None, :] // A_HD
    first_lane = (head % 2) * A_HD + (head // 2) * LSE_LANES
    once = (src == first_lane).astype(BF16)
    return jnp.concatenate([once, once], axis=0)


def merge_out_ln(x, oa, la, yb, yc, z, wa, wb, wc, wo, g, b):
    t = x.shape[0]
    tm = MERGE_TILE_ROWS
    row = pl.BlockSpec((tm, D_MODEL), lambda i: (i, 0))
    br = pl.BlockSpec((tm, HEAD_BLOCK), lambda i: (i, 0))
    lse = pl.BlockSpec((tm, 2 * A_HD), lambda i: (i, 0))
    w_br = _const_spec((HEAD_BLOCK, D_MODEL))
    return pl.pallas_call(
        _merge_kernel,
        grid=(t // tm,),
        in_specs=[row] + [br] * 3 + [lse] * 3 + [br] * 2 + [
            pl.BlockSpec((tm, GATE_W), lambda i: (i, 0)), _const_spec((4 * A_HD, HEAD_BLOCK)), w_br, w_br, w_br,
            _const_spec((D_MODEL, D_MODEL)), _const_spec((1, D_MODEL)), _const_spec((1, D_MODEL))],
        out_specs=row,
        out_shape=jax.ShapeDtypeStruct((t, D_MODEL), F32),
        compiler_params=_params(("arbitrary",)),
        name="merge_out_ln",
    )(x, *oa, *la, yb, yc, z, _lse_spread(), wa, wb, wc, wo, g, b)


def _split_in_weights(w_in):
    a_w = A_GROUPS * HEAD_BLOCK
    aq, ak, av = w_in[:, :a_w] * (A_HD ** -0.5 * LOG2E), w_in[:, a_w:2 * a_w], w_in[:, 2 * a_w:3 * a_w]
    b_w = 2 * B_HEADS * (B_DK + B_DV)
    rest_b = w_in[:, 3 * a_w:3 * a_w + b_w]
    cq = w_in[:, 3 * a_w + b_w:3 * a_w + b_w + HEAD_BLOCK] * (C_HD ** -0.5 * LOG2E)
    rest = jnp.concatenate([rest_b, cq, w_in[:, 3 * a_w + b_w + HEAD_BLOCK:D_IN - GATE_W]], axis=1)

    def group(g):
        cols = slice(g * HEAD_BLOCK, (g + 1) * HEAD_BLOCK)
        return jnp.concatenate([aq[:, cols], ak[:, cols], av[:, cols]], axis=1)

    main = jnp.concatenate([w_in[:, D_IN - GATE_W:], group(0), rest], axis=1)
    return main.astype(BF16), [group(g).astype(BF16) for g in range(1, A_GROUPS)]


def _trunk(x, layers):
    batch, seq, _ = x.shape
    x = x.reshape(batch * seq, D_MODEL)
    for p in layers:
        dils = tuple(d for _, d in A_PATTERNS[1:])
        x1, *x1_by_residue = ffn_ln(x, p["wg1"], p["wu1"], p["wd1"], p["g1"], p["b1"], batch, dils)
        z = in_proj(x1, p["w_main"])
        oa, la = [], []
        o, l = mixer_a_group(z.reshape(batch, 1, seq, MAIN_W), 1, A_QKV_BLK)
        oa.append(o.reshape(batch * seq, HEAD_BLOCK))
        la.append(l.reshape(batch * seq, 2 * A_HD))
        for dilation, xg, w_g in zip(dils, x1_by_residue, p["w_groups"]):
            zg = in_proj(xg.reshape(batch * seq, D_MODEL), w_g)
            o, l = mixer_a_group(zg.reshape(batch, dilation, seq // dilation, A_QKV_W), dilation, 0)
            oa.append(_by_token(o, batch, seq))
            la.append(_by_token(l, batch, seq))
        yb = mixer_b(z, batch, seq, p["logit_fwd"], p["logit_bwd"])
        yc = mixer_c(z, batch, seq, p["rpb"])
        x2 = merge_out_ln(x1, oa, la, yb, yc, z, p["wa"], p["wb"], p["wc"], p["wo"], p["g2"], p["b2"])
        (x,) = ffn_ln(x2, p["wg2"], p["wu2"], p["wd2"], p["g3"], p["b3"])
    return x.reshape(batch, seq, D_MODEL)


def kernel(x_prompt, x_sample, ffn1_w_gate, ffn1_w_up, ffn1_w_down, ln1_g, ln1_b, w_in, ret_logit_fwd, ret_logit_bwd, na_rpb, w_branch_a, w_branch_b, w_branch_c, w_out, ln2_g, ln2_b, ffn2_w_gate, ffn2_w_up, ffn2_w_down, ln3_g, ln3_b):
    def vec(v):
        return v.astype(F32).reshape(1, D_MODEL)

    layers = []
    for i in range(DEPTH):
        w_main, w_groups = _split_in_weights(w_in[i])
        layers.append(dict(
            wg1=ffn1_w_gate[i].astype(BF16), wu1=ffn1_w_up[i].astype(BF16), wd1=ffn1_w_down[i].astype(BF16),
            g1=vec(ln1_g[i]), b1=vec(ln1_b[i]),
            w_main=w_main, w_groups=w_groups,
            logit_fwd=ret_logit_fwd[i], logit_bwd=ret_logit_bwd[i], rpb=na_rpb[i],
            wa=w_branch_a[i].astype(BF16), wb=w_branch_b[i].astype(BF16), wc=w_branch_c[i].astype(BF16),
            wo=w_out[i].astype(BF16), g2=vec(ln2_g[i]), b2=vec(ln2_b[i]),
            wg2=ffn2_w_gate[i].astype(BF16), wu2=ffn2_w_up[i].astype(BF16), wd2=ffn2_w_down[i].astype(BF16),
            g3=vec(ln3_g[i]), b3=vec(ln3_b[i])))
    return (_trunk(x_prompt, layers), _trunk(x_sample, layers))
```
